```python
import math
import jax, jax.numpy as jnp
from jax import lax
import numpy as np

D_MODEL = 1024
BATCH = 16
SEQ = 2048
DEPTH = 2

HEAD_DIM = 64
N_HEADS = D_MODEL // HEAD_DIM
N_SB_HEADS = N_HEADS // 2
N_SWA_HEADS = N_HEADS - N_SB_HEADS
N_SWA_KV = 2
N_FOX_HEADS = N_HEADS
WINDOW = 128
BLOCK = 128
N_BUCKETS = 32
MAX_DISTANCE = 128
D_FF = 2816
ALPHA = (2 * DEPTH) ** 0.25
INIT_BETA = (8 * DEPTH) ** -0.25
LN_EPS = 1e-5
N_EVEN = (DEPTH + 1) // 2
N_ODD = DEPTH // 2

SB_W = N_SB_HEADS * HEAD_DIM
SWA_QW = N_SWA_HEADS * HEAD_DIM
SWA_KVW = N_SWA_KV * HEAD_DIM
AB_IN = 3 * SB_W + SWA_QW + 2 * SWA_KVW
AB_OUT = SB_W + SWA_QW
FOX_W = N_FOX_HEADS * HEAD_DIM
FOX_IN = 3 * FOX_W + N_FOX_HEADS

kernel_name = "hybrid_stickbreak_swa_fox_deepnorm"


def layer_norm(x, g, b):
    xf = x.astype(jnp.float32)
    mu = jnp.mean(xf, axis=-1, keepdims=True)
    var = jnp.mean(jnp.square(xf - mu), axis=-1, keepdims=True)
    return ((xf - mu) * lax.rsqrt(var + LN_EPS)).astype(x.dtype) * g + b


def swiglu_ffn(x, w_in, w_out):
    gate, up = jnp.split(x @ w_in, 2, axis=-1)
    return (jax.nn.silu(gate) * up) @ w_out


def t5_causal_bucket(rel):
    n = jnp.maximum(rel, 0)
    max_exact = N_BUCKETS // 2
    nf = jnp.maximum(n, 1).astype(jnp.float32)
    large = max_exact + (jnp.log(nf / max_exact) / math.log(MAX_DISTANCE / max_exact)
                         * (N_BUCKETS - max_exact)).astype(jnp.int32)
    large = jnp.minimum(large, N_BUCKETS - 1)
    return jnp.where(n < max_exact, n, large)


def stick_breaking_attention(q, k, v):
    S, Dh = q.shape[1], q.shape[3]
    scale = Dh ** -0.5
    outs = []
    for blk in range(S // BLOCK):
        q0, end = blk * BLOCK, (blk + 1) * BLOCK
        z = jnp.einsum('bqhd,bkhd->bhqk', q[:, q0:end], k[:, :end]).astype(jnp.float32) * scale
        t_pos = q0 + jnp.arange(BLOCK)[:, None]
        s_pos = jnp.arange(end)[None, :]
        strict = s_pos < t_pos
        log_keep = jnp.where(strict, jax.nn.log_sigmoid(-z), 0.0)
        after = lax.cumsum(log_keep, axis=3, reverse=True) - log_keep
        w = jnp.where(strict, jnp.exp(jax.nn.log_sigmoid(z) + after), 0.0)
        outs.append(jnp.einsum('bhqk,bkhd->bqhd', w.astype(v.dtype), v[:, :end]))
    return jnp.concatenate(outs, axis=1)


def sliding_window_sink_attention(q, k, v, sinks, rel_bias):
    B, S, Hq, Dh = q.shape
    Hkv = k.shape[2]
    G = Hq // Hkv
    nb = S // BLOCK
    qb = q.reshape(B, nb, BLOCK, Hkv, G, Dh)

    def band(t):
        tb = t.reshape(B, nb, BLOCK, Hkv, Dh)
        prev = jnp.pad(tb, ((0, 0), (1, 0), (0, 0), (0, 0), (0, 0)))[:, :-1]
        return jnp.concatenate([prev, tb], axis=2)

    kb, vb = band(k), band(v)
    logits = jnp.einsum('bnqhgd,bnkhd->bnhgqk', qb, kb).astype(jnp.float32) * Dh ** -0.5
    qi = jnp.arange(BLOCK)[:, None]
    kj = jnp.arange(2 * BLOCK)[None, :]
    rel = qi + BLOCK - kj
    bias = rel_bias[t5_causal_bucket(rel)].astype(jnp.float32)
    bias = bias.transpose(2, 0, 1).reshape(Hkv, G, BLOCK, 2 * BLOCK)
    in_window = (rel >= 0) & (rel < WINDOW)
    key_pos = jnp.arange(nb)[:, None, None] * BLOCK - BLOCK + kj[None]
    valid = in_window[None] & (key_pos >= 0)
    logits = jnp.where(valid[None, :, None, None], logits + bias, -jnp.inf)
    sink = sinks.astype(jnp.float32).reshape(Hkv, G, 1, 1)
    m = jnp.maximum(jnp.max(logits, axis=-1, keepdims=True), sink)
    p = jnp.exp(logits - m)
    w = p / (jnp.sum(p, axis=-1, keepdims=True) + jnp.exp(sink - m))
    out = jnp.einsum('bnhgqk,bnkhd->bnqhgd', w.astype(v.dtype), vb)
    return out.reshape(B, S, Hq, Dh)


def forgetting_attention(q, k, v, log_f):
    S, Dh = q.shape[1], q.shape[3]
    scale = Dh ** -0.5
    c = lax.cumsum(log_f, axis=1).transpose(0, 2, 1)
    outs = []
    for blk in range(S // BLOCK):
        q0, end = blk * BLOCK, (blk + 1) * BLOCK
        logits = jnp.einsum('bqhd,bkhd->bhqk', q[:, q0:end], k[:, :end]).astype(jnp.float32) * scale
        logits = logits + c[:, :, q0:end, None] - c[:, :, None, :end]
        causal = jnp.arange(end)[None, :] <= (q0 + jnp.arange(BLOCK)[:, None])
        w = jax.nn.softmax(jnp.where(causal, logits, -jnp.inf), axis=-1)
        outs.append(jnp.einsum('bhqk,bkhd->bqhd', w.astype(v.dtype), v[:, :end]))
    return jnp.concatenate(outs, axis=1)


def even_mixer(h, w_in, w_out, sinks, rel_bias):
    B, S, _ = h.shape
    proj = h @ w_in
    cuts = [SB_W, 2 * SB_W, 3 * SB_W, 3 * SB_W + SWA_QW, 3 * SB_W + SWA_QW + SWA_KVW]
    sb_q, sb_k, sb_v, sw_q, sw_k, sw_v = jnp.split(proj, cuts, axis=-1)
    heads = lambda t, n: t.reshape(B, S, n, HEAD_DIM)
    o_sb = stick_breaking_attention(heads(sb_q, N_SB_HEADS), heads(sb_k, N_SB_HEADS),
                                    heads(sb_v, N_SB_HEADS))
    o_sw = sliding_window_sink_attention(heads(sw_q, N_SWA_HEADS), heads(sw_k, N_SWA_KV),
                                         heads(sw_v, N_SWA_KV), sinks, rel_bias)
    o = jnp.concatenate([o_sb.reshape(B, S, SB_W), o_sw.reshape(B, S, SWA_QW)], axis=-1)
    return o @ w_out


def odd_mixer(h, w_in, b_f, w_out):
    B, S, _ = h.shape
    proj = h @ w_in
    q, k, v, f = jnp.split(proj, [FOX_W, 2 * FOX_W, 3 * FOX_W], axis=-1)
    heads = lambda t: t.reshape(B, S, N_FOX_HEADS, HEAD_DIM)
    log_f = jax.nn.log_sigmoid((f + b_f).astype(jnp.float32))
    o = forgetting_attention(heads(q), heads(k), heads(v), log_f)
    return o.reshape(B, S, FOX_W) @ w_out


def setup_inputs(seed: int = 0) -> dict:
    key = jax.random.key(seed)
    ks = jax.random.split(key, 16)
    nrm = lambda k, shape, s: jax.random.normal(k, shape, jnp.float32) * s
    return {
        "x": nrm(ks[0], (BATCH, SEQ, D_MODEL), 1.0),
        "ln_g": 1.0 + nrm(ks[1], (DEPTH, 3, D_MODEL), 0.05),
        "ln_b": nrm(ks[2], (DEPTH, 3, D_MODEL), 0.02),
        "ffn1_in": nrm(ks[3], (DEPTH, D_MODEL, 2 * D_FF), D_MODEL ** -0.5),
        "ffn1_out": nrm(ks[4], (DEPTH, D_FF, D_MODEL), D_FF ** -0.5 * INIT_BETA),
        "ffn2_in": nrm(ks[5], (DEPTH, D_MODEL, 2 * D_FF), D_MODEL ** -0.5),
        "ffn2_out": nrm(ks[6], (DEPTH, D_FF, D_MODEL), D_FF ** -0.5 * INIT_BETA),
        "ab_w_in": nrm(ks[7], (N_EVEN, D_MODEL, AB_IN), D_MODEL ** -0.5),
        "ab_w_out": nrm(ks[8], (N_EVEN, AB_OUT, D_MODEL), AB_OUT ** -0.5 * INIT_BETA),
        "ab_sinks": nrm(ks[9], (N_EVEN, N_SWA_HEADS), 0.1),
        "fox_w_in": nrm(ks[10], (N_ODD, D_MODEL, FOX_IN), D_MODEL ** -0.5),
        "fox_b_f": nrm(ks[11], (N_ODD, N_FOX_HEADS), 0.1),
        "fox_w_out": nrm(ks[12], (N_ODD, FOX_W, D_MODEL), FOX_W ** -0.5 * INIT_BETA),
        "rel_bias": nrm(ks[13], (N_BUCKETS, N_SWA_HEADS), 0.2),
    }


def reference(x, ln_g, ln_b, ffn1_in, ffn1_out, ffn2_in, ffn2_out, ab_w_in, ab_w_out,
              ab_sinks, fox_w_in, fox_b_f, fox_w_out, rel_bias):
    h = x
    for layer in range(DEPTH):
        h = layer_norm(ALPHA * h + 0.5 * swiglu_ffn(h, ffn1_in[layer], ffn1_out[layer]),
                       ln_g[layer, 0], ln_b[layer, 0])
        if layer % 2 == 0:
            i = layer // 2
            mix = even_mixer(h, ab_w_in[i], ab_w_out[i], ab_sinks[i], rel_bias)
        else:
            i = layer // 2
            mix = odd_mixer(h, fox_w_in[i], fox_b_f[i], fox_w_out[i])
        h = layer_norm(ALPHA * h + mix, ln_g[layer, 1], ln_b[layer, 1])
        h = layer_norm(ALPHA * h + 0.5 * swiglu_ffn(h, ffn2_in[layer], ffn2_out[layer]),
                       ln_g[layer, 2], ln_b[layer, 2])
    return h
```

```python
import functools
import math

import numpy as np
import jax
import jax.numpy as jnp
from jax import lax
from jax.experimental import pallas as pl
from jax.experimental.pallas import tpu as pltpu

D_MODEL = 1024
SEQ = 2048
DEPTH = 2
HEAD_DIM = 64
LANES = 128
N_SB_PAIRS = 4
N_SWA_HEADS = 8
N_SWA_KV = 2
N_FOX_HEADS = 16
N_FOX_PAIRS = 8
WINDOW = 128
N_BUCKETS = 32
MAX_DISTANCE = 128
D_FF = 2816
ALPHA = (2 * DEPTH) ** 0.25
LN_EPS = 1e-5
SB_W = 512
SWA_QW = 512
SWA_KVW = 128
FOX_W = 1024
NEG_BIG = -1e30
VMEM_LIMIT = 56 * 1024 * 1024

BF16 = jnp.bfloat16
F32 = jnp.float32


def _dot(a, b):
    return jnp.dot(a, b, preferred_element_type=F32)


def _dot_nt(a, b):
    return lax.dot_general(a, b, (((1,), (1,)), ((), ())), preferred_element_type=F32)


def _layer_norm(z, g, b):
    mu = jnp.mean(z, axis=-1, keepdims=True)
    d = z - mu
    var = jnp.mean(d * d, axis=-1, keepdims=True)
    return d * lax.rsqrt(var + LN_EPS) * g + b


def _softplus(z):
    return jnp.maximum(z, 0.0) + jnp.log1p(jnp.exp(-jnp.abs(z)))


def _split_bf16(x, parts):
    out = []
    r = x
    for i in range(parts):
        t = r.astype(BF16)
        out.append(t)
        if i + 1 < parts:
            r = r - t.astype(F32)
    return out


def _params(sem):
    return pltpu.CompilerParams(dimension_semantics=sem, vmem_limit_bytes=VMEM_LIMIT)


def _resident(shape, index_map):
    return pl.BlockSpec(shape, index_map, pipeline_mode=pl.Buffered(1))


def _ffn_ln_kernel(x_ref, wg_ref, wu_ref, wo_ref, g_ref, b_ref, o_ref):
    x = x_ref[...]
    xb = x.astype(BF16)
    gate = _dot(xb, wg_ref[...])
    up = _dot(xb, wu_ref[...])
    h = (gate * jax.nn.sigmoid(gate) * up).astype(BF16)
    y = _dot(h, wo_ref[...])
    o_ref[...] = _layer_norm(ALPHA * x + 0.5 * y, g_ref[...], b_ref[...])


def _ffn_ln(x, w_in, w_out, g, b, tm=256):
    n = x.shape[0]
    return pl.pallas_call(
        _ffn_ln_kernel,
        grid=(n // tm,),
        in_specs=[
            pl.BlockSpec((tm, D_MODEL), lambda i: (i, 0)),
            _resident((D_MODEL, D_FF), lambda i: (0, 0)),
            _resident((D_MODEL, D_FF), lambda i: (0, 1)),
            _resident((D_FF, D_MODEL), lambda i: (0, 0)),
            _resident((1, D_MODEL), lambda i: (0, 0)),
            _resident((1, D_MODEL), lambda i: (0, 0)),
        ],
        out_specs=pl.BlockSpec((tm, D_MODEL), lambda i: (i, 0)),
        out_shape=jax.ShapeDtypeStruct((n, D_MODEL), F32),
        compiler_params=_params(("parallel",)),
        name="ffn_ln",
    )(x, w_in, w_in, w_out, g, b)


def _proj_kernel(x_ref, w_ref, o_ref):
    o_ref[...] = _dot(x_ref[...].astype(BF16), w_ref[...]).astype(o_ref.dtype)


def _proj(x, w, tm=512):
    n, width = x.shape[0], w.shape[1]
    return pl.pallas_call(
        _proj_kernel,
        grid=(n // tm,),
        in_specs=[
            pl.BlockSpec((tm, D_MODEL), lambda i: (i, 0)),
            _resident((D_MODEL, width), lambda i: (0, 0)),
        ],
        out_specs=pl.BlockSpec((tm, width), lambda i: (i, 0)),
        out_shape=jax.ShapeDtypeStruct((n, width), BF16),
        compiler_params=_params(("parallel",)),
        name="in_proj",
    )(x, w)


def _fox_proj_kernel(x_ref, w_ref, wf_ref, o_ref, f_ref):
    xb = x_ref[...].astype(BF16)
    o_ref[...] = _dot(xb, w_ref[...]).astype(o_ref.dtype)
    f_ref[...] = _dot_nt(wf_ref[...], xb)


def _fox_proj(x, w, wf_t, tm=512):
    b, s, _ = x.shape
    return pl.pallas_call(
        _fox_proj_kernel,
        grid=(b, s // tm),
        in_specs=[
            pl.BlockSpec((None, tm, D_MODEL), lambda i, j: (i, j, 0)),
            _resident((D_MODEL, 3 * FOX_W), lambda i, j: (0, 0)),
            _resident((N_FOX_HEADS, D_MODEL), lambda i, j: (0, 0)),
        ],
        out_specs=[
            pl.BlockSpec((None, tm, 3 * FOX_W), lambda i, j: (i, j, 0)),
            pl.BlockSpec((None, N_FOX_HEADS, tm), lambda i, j: (i, 0, j)),
        ],
        out_shape=[
            jax.ShapeDtypeStruct((b, s, 3 * FOX_W), BF16),
            jax.ShapeDtypeStruct((b, N_FOX_HEADS, s), F32),
        ],
        compiler_params=_params(("parallel", "parallel")),
        name="fox_in_proj",
    )(x, w, wf_t)


def _outproj_ln_kernel(o_ref, h_ref, w_ref, g_ref, b_ref, out_ref):
    mix = _dot(o_ref[...], w_ref[...])
    out_ref[...] = _layer_norm(ALPHA * h_ref[...] + mix, g_ref[...], b_ref[...])


def _outproj_ln(o, h, w, g, b, tm=512):
    n = h.shape[0]
    return pl.pallas_call(
        _outproj_ln_kernel,
        grid=(n // tm,),
        in_specs=[
            pl.BlockSpec((tm, D_MODEL), lambda i: (i, 0)),
            pl.BlockSpec((tm, D_MODEL), lambda i: (i, 0)),
            _resident((D_MODEL, D_MODEL), lambda i: (0, 0)),
            _resident((1, D_MODEL), lambda i: (0, 0)),
            _resident((1, D_MODEL), lambda i: (0, 0)),
        ],
        out_specs=pl.BlockSpec((tm, D_MODEL), lambda i: (i, 0)),
        out_shape=jax.ShapeDtypeStruct((n, D_MODEL), F32),
        compiler_params=_params(("parallel",)),
        name="out_proj_ln",
    )(o, h, w, g, b)


def _head_pair(q):
    lo = lax.broadcasted_iota(jnp.int32, q.shape, 1) < HEAD_DIM
    zero = jnp.zeros_like(q)
    return jnp.where(lo, q, zero), jnp.where(lo, zero, q)


def _merge_pair(a0, a1):
    lo = lax.broadcasted_iota(jnp.int32, a0.shape, 1) < HEAD_DIM
    return jnp.where(lo, a0, a1)


def _sb_kernel(q_ref, k_ref, v_ref, o_ref, *, t):
    qi = pl.program_id(2)
    scale = jnp.asarray(HEAD_DIM ** -0.5, BF16)
    q_heads = _head_pair(q_ref[...] * scale)
    row = lax.broadcasted_iota(jnp.int32, (t, t), 0)
    col = lax.broadcasted_iota(jnp.int32, (t, t), 1)
    strict = col < row
    later = jnp.where(row > col, 1.0, 0.0).astype(BF16)

    def block(q_h, kblk, vblk, carry, masked):
        z = _dot_nt(q_h, kblk)
        sp = _softplus(z)
        log_keep = -sp
        if masked:
            log_keep = jnp.where(strict, log_keep, 0.0)
        hi, lo = _split_bf16(log_keep, 2)
        after = _dot(hi, later) + _dot(lo, later)
        w = jnp.exp(z - sp + after + carry)
        if masked:
            w = jnp.where(strict, w, 0.0)
        pv = _dot(w.astype(BF16), vblk)
        return pv, carry + jnp.sum(log_keep, axis=-1, keepdims=True)

    def kv(kb):
        start = pl.multiple_of(kb * t, t)
        return k_ref[pl.ds(start, t), :], v_ref[pl.ds(start, t), :]

    kblk, vblk = kv(qi)
    zero_c = jnp.zeros((t, 1), F32)
    acc0, c0 = block(q_heads[0], kblk, vblk, zero_c, True)
    acc1, c1 = block(q_heads[1], kblk, vblk, zero_c, True)

    def body(j, state):
        acc0, c0, acc1, c1 = state
        kblk, vblk = kv(qi - 1 - j)
        pv0, c0 = block(q_heads[0], kblk, vblk, c0, False)
        pv1, c1 = block(q_heads[1], kblk, vblk, c1, False)
        return acc0 + pv0, c0, acc1 + pv1, c1

    acc0, _, acc1, _ = lax.fori_loop(0, qi, body, (acc0, c0, acc1, c1))
    o_ref[...] = _merge_pair(acc0, acc1).astype(o_ref.dtype)


def _sb_attention(proj, out_width, t=256):
    b, s, _ = proj.shape
    return pl.pallas_call(
        functools.partial(_sb_kernel, t=t),
        grid=(b, N_SB_PAIRS, s // t),
        in_specs=[
            pl.BlockSpec((None, t, LANES), lambda i, h, j: (i, j, h)),
            pl.BlockSpec((None, s, LANES), lambda i, h, j: (i, 0, N_SB_PAIRS + h)),
            pl.BlockSpec((None, s, LANES), lambda i, h, j: (i, 0, 2 * N_SB_PAIRS + h)),
        ],
        out_specs=pl.BlockSpec((None, t, LANES), lambda i, h, j: (i, j, h)),
        out_shape=jax.ShapeDtypeStruct((b, s, out_width), BF16),
        compiler_params=_params(("parallel", "parallel", "arbitrary")),
        name="sb_attention",
    )(proj, proj, proj)


def _t5_bucket_table():
    n = np.arange(WINDOW)
    max_exact = N_BUCKETS // 2
    nf = np.maximum(n, 1).astype(np.float64)
    val = np.log(nf / max_exact) / math.log(MAX_DISTANCE / max_exact) * (N_BUCKETS - max_exact)
    frac = np.abs(val - np.round(val))[n > max_exact]
    assert frac.min() > 1e-3, "bucket boundary too close to an integer distance for float32"
    large = np.minimum(max_exact + val.astype(np.int64), N_BUCKETS - 1)
    return np.where(n < max_exact, n, large)


def _swa_bucket_tiles():
    table = _t5_bucket_table()
    qi = np.arange(WINDOW)[:, None]
    kj = np.arange(2 * WINDOW)[None, :]
    tiles = []
    for offset in (WINDOW, 0):
        rel = qi + offset - kj
        valid = (rel >= 0) & (rel < WINDOW)
        tiles.append(np.where(valid, table[np.clip(rel, 0, WINDOW - 1)], -1))
    return np.stack(tiles).astype(np.int32)


def _swa_bias_kernel(rb_ref, idx_ref, o_ref):
    h = pl.program_id(0)
    for tile in range(2):
        idx = idx_ref[tile]
        acc = jnp.full(idx.shape, NEG_BIG, F32)
        for bucket in range(N_BUCKETS):
            acc = jnp.where(idx == bucket, rb_ref[bucket, h], acc)
        o_ref[tile, 0] = acc


def _swa_bias(rel_bias):
    idx = jnp.asarray(_swa_bucket_tiles())
    return pl.pallas_call(
        _swa_bias_kernel,
        grid=(N_SWA_HEADS,),
        in_specs=[
            pl.BlockSpec(memory_space=pltpu.SMEM),
            pl.BlockSpec((2, WINDOW, 2 * WINDOW), lambda h: (0, 0, 0)),
        ],
        out_specs=pl.BlockSpec((2, 1, WINDOW, 2 * WINDOW), lambda h: (0, h, 0, 0)),
        out_shape=jax.ShapeDtypeStruct((2, N_SWA_HEADS, WINDOW, 2 * WINDOW), F32),
        compiler_params=_params(("arbitrary",)),
        name="swa_bias",
    )(rel_bias, idx)


def _swa_kernel(sink_ref, q_ref, k_ref, v_ref, bias_ref, o_prev_ref, o_ref, *, tq):
    del o_prev_ref
    qt = pl.program_id(1)
    scale = jnp.asarray(HEAD_DIM ** -0.5, BF16)
    group = N_SWA_HEADS // N_SWA_KV
    for j in range(tq // WINDOW):
        nb = qt * (tq // WINDOW) + j
        first = (nb == 0).astype(jnp.int32) if j == 0 else 0
        w0 = pl.multiple_of(jnp.maximum(nb - 1, 0) * WINDOW, WINDOW)
        rows = slice(j * WINDOW, (j + 1) * WINDOW)
        for g in range(N_SWA_KV):
            kwin = k_ref[pl.ds(w0, 2 * WINDOW), g * LANES:(g + 1) * LANES]
            vwin = v_ref[pl.ds(w0, 2 * WINDOW), g * LANES:(g + 1) * LANES]
            for c in range(group // 2):
                pair = g * (group // 2) + c
                cols = slice(pair * LANES, (pair + 1) * LANES)
                q_heads = _head_pair(q_ref[rows, cols] * scale)
                outs = []
                for i in range(2):
                    head = 2 * pair + i
                    sink = sink_ref[head]
                    logits = _dot_nt(q_heads[i], kwin) + bias_ref[first, head]
                    m = jnp.maximum(jnp.max(logits, axis=-1, keepdims=True), sink)
                    p = jnp.exp(logits - m)
                    denom = jnp.sum(p, axis=-1, keepdims=True) + jnp.exp(sink - m)
                    outs.append(_dot(p.astype(BF16), vwin) / denom)
                o_ref[rows, cols] = _merge_pair(outs[0], outs[1]).astype(o_ref.dtype)


def _swa_attention(proj, bias, sinks, o_prev, tq=512):
    b, s, _ = proj.shape
    q_blk = (3 * SB_W) // SWA_QW
    k_blk = (3 * SB_W + SWA_QW) // (2 * LANES)
    return pl.pallas_call(
        functools.partial(_swa_kernel, tq=tq),
        grid=(b, s // tq),
        in_specs=[
            pl.BlockSpec(memory_space=pltpu.SMEM),
            pl.BlockSpec((None, tq, SWA_QW), lambda i, j: (i, j, q_blk)),
            pl.BlockSpec((None, s, 2 * LANES), lambda i, j: (i, 0, k_blk)),
            pl.BlockSpec((None, s, 2 * LANES), lambda i, j: (i, 0, k_blk + 1)),
            _resident((2, N_SWA_HEADS, WINDOW, 2 * WINDOW), lambda i, j: (0, 0, 0, 0)),
            pl.BlockSpec(memory_space=pl.ANY),
        ],
        out_specs=pl.BlockSpec((None, tq, SWA_QW), lambda i, j: (i, j, SB_W // SWA_QW)),
        out_shape=jax.ShapeDtypeStruct(o_prev.shape, o_prev.dtype),
        input_output_aliases={5: 0},
        compiler_params=_params(("parallel", "arbitrary")),
        name="swa_attention",
    )(sinks, proj, proj, proj, bias, o_prev)


def _fox_cumsum_kernel(f_ref, bf_ref, c_ref, *, chunk):
    rows, s = f_ref.shape
    r = lax.broadcasted_iota(jnp.int32, (chunk, chunk), 0)
    c = lax.broadcasted_iota(jnp.int32, (chunk, chunk), 1)
    upto = jnp.where(r <= c, 1.0, 0.0).astype(BF16)
    carry = jnp.zeros((rows, 1), F32)
    for j in range(s // chunk):
        cols = slice(j * chunk, (j + 1) * chunk)
        log_f = -_softplus(-(f_ref[:, cols] + bf_ref[...]))
        cs = carry
        for part in _split_bf16(log_f, 3):
            cs = cs + _dot(part, upto)
        c_ref[:, cols] = cs
        carry = cs[:, chunk - 1:chunk]


def _fox_cumsum(f_t, b_f, chunk=256):
    b, h, s = f_t.shape
    rows = b * h
    bias = jnp.tile(b_f.astype(F32), b).reshape(rows, 1)
    out = pl.pallas_call(
        functools.partial(_fox_cumsum_kernel, chunk=chunk),
        out_shape=jax.ShapeDtypeStruct((rows, s), F32),
        compiler_params=pltpu.CompilerParams(vmem_limit_bytes=VMEM_LIMIT),
        name="fox_cumsum",
    )(f_t.reshape(rows, s), bias)
    return out.reshape(b, h // 2, 2, s)


def _fox_kernel(q_ref, k_ref, v_ref, c_ref, o_ref, *, t):
    qi = pl.program_id(2)
    scale = jnp.asarray(HEAD_DIM ** -0.5, BF16)
    q_heads = _head_pair(q_ref[...] * scale)
    row = lax.broadcasted_iota(jnp.int32, (t, t), 0)
    col = lax.broadcasted_iota(jnp.int32, (t, t), 1)
    causal = col <= row
    q_start = pl.multiple_of(qi * t, t)
    c_base = [c_ref[i:i + 1, pl.ds(q_start, LANES)][:, 0:1] for i in range(2)]

    def block(i, kb, state, masked):
        m, l, acc = state
        start = pl.multiple_of(kb * t, t)
        kblk = k_ref[pl.ds(start, t), :]
        vblk = v_ref[pl.ds(start, t), :]
        decay = c_ref[i:i + 1, pl.ds(start, t)] - c_base[i]
        s = _dot_nt(q_heads[i], kblk) - decay
        if masked:
            s = jnp.where(causal, s, NEG_BIG)
        m_new = jnp.maximum(m, jnp.max(s, axis=-1, keepdims=True))
        a = jnp.exp(m - m_new)
        p = jnp.exp(s - m_new)
        l = a * l + jnp.sum(p, axis=-1, keepdims=True)
        acc = a * acc + _dot(p.astype(BF16), vblk)
        return m_new, l, acc

    init = (jnp.full((t, 1), NEG_BIG, F32), jnp.zeros((t, 1), F32), jnp.zeros((t, LANES), F32))

    def body(kb, state):
        return block(0, kb, state[0], False), block(1, kb, state[1], False)

    s0, s1 = lax.fori_loop(0, qi, body, (init, init))
    s0 = block(0, qi, s0, True)
    s1 = block(1, qi, s1, True)
    o_ref[...] = _merge_pair(s0[2] / s0[1], s1[2] / s1[1]).astype(o_ref.dtype)


def _fox_attention(proj, c, t=256):
    b, s, _ = proj.shape
    return pl.pallas_call(
        functools.partial(_fox_kernel, t=t),
        grid=(b, N_FOX_PAIRS, s // t),
        in_specs=[
            pl.BlockSpec((None, t, LANES), lambda i, h, j: (i, j, h)),
            pl.BlockSpec((None, s, LANES), lambda i, h, j: (i, 0, N_FOX_PAIRS + h)),
            pl.BlockSpec((None, s, LANES), lambda i, h, j: (i, 0, 2 * N_FOX_PAIRS + h)),
            pl.BlockSpec((None, None, 2, s), lambda i, h, j: (i, h, 0, 0)),
        ],
        out_specs=pl.BlockSpec((None, t, LANES), lambda i, h, j: (i, j, h)),
        out_shape=jax.ShapeDtypeStruct((b, s, FOX_W), BF16),
        compiler_params=_params(("parallel", "parallel", "arbitrary")),
        name="fox_attention",
    )(proj, proj, proj, c)


def _even_mixer(h, w_in, w_out, sinks, rel_bias, g, b):
    bsz, s, _ = h.shape
    kv0 = 3 * SB_W + SWA_QW
    dup = lambda w: jnp.repeat(w.reshape(D_MODEL, N_SWA_KV, HEAD_DIM), 2, axis=1).reshape(D_MODEL, -1)
    w_cat = jnp.concatenate(
        [w_in[:, :kv0], dup(w_in[:, kv0:kv0 + SWA_KVW]), dup(w_in[:, kv0 + SWA_KVW:])], axis=1)
    proj = _proj(h.reshape(bsz * s, D_MODEL), w_cat.astype(BF16)).reshape(bsz, s, -1)
    o = _sb_attention(proj, SB_W + SWA_QW)
    o = _swa_attention(proj, _swa_bias(rel_bias), sinks.astype(F32), o)
    return _outproj_ln(o.reshape(bsz * s, -1), h.reshape(bsz * s, D_MODEL), w_out.astype(BF16), g, b)


def _odd_mixer(h, w_in, b_f, w_out, g, b):
    bsz, s, _ = h.shape
    proj, f_t = _fox_proj(h, w_in[:, :3 * FOX_W].astype(BF16), w_in[:, 3 * FOX_W:].T.astype(BF16))
    o = _fox_attention(proj, _fox_cumsum(f_t, b_f))
    return _outproj_ln(o.reshape(bsz * s, -1), h.reshape(bsz * s, D_MODEL), w_out.astype(BF16), g, b)


def kernel(x, ln_g, ln_b, ffn1_in, ffn1_out, ffn2_in, ffn2_out, ab_w_in, ab_w_out, ab_sinks,
           fox_w_in, fox_b_f, fox_w_out, rel_bias):
    bsz, s, d = x.shape
    assert (s, d) == (SEQ, D_MODEL) and ln_g.shape[0] == DEPTH
    row = lambda v: v.reshape(1, D_MODEL).astype(F32)
    h = x.reshape(bsz * s, d)
    for layer in range(DEPTH):
        g, b = ln_g[layer], ln_b[layer]
        h = _ffn_ln(h, ffn1_in[layer].astype(BF16), ffn1_out[layer].astype(BF16), row(g[0]), row(b[0]))
        i = layer // 2
        if layer % 2 == 0:
            h = _even_mixer(h.reshape(bsz, s, d), ab_w_in[i], ab_w_out[i], ab_sinks[i], rel_bias,
                            row(g[1]), row(b[1]))
        else:
            h = _odd_mixer(h.reshape(bsz, s, d), fox_w_in[i], fox_b_f[i], fox_w_out[i],
                           row(g[1]), row(b[1]))
        h = _ffn_ln(h, ffn2_in[layer].astype(BF16), ffn2_out[layer].astype(BF16), row(g[2]), row(b[2]))
    return h.reshape(bsz, s, d)
```

```python
import functools
import math

import numpy as np
import jax
import jax.numpy as jnp
from jax import lax
from jax.experimental import pallas as pl
from jax.experimental.pallas import tpu as pltpu

D_MODEL = 1024
SEQ = 2048
DEPTH = 2
HEAD_DIM = 64
LANES = 128
N_SB_PAIRS = 4
N_SWA_HEADS = 8
N_SWA_KV = 2
N_FOX_HEADS = 16
N_FOX_PAIRS = 8
WINDOW = 128
N_BUCKETS = 32
MAX_DISTANCE = 128
D_FF = 2816
ALPHA = (2 * DEPTH) ** 0.25
LN_EPS = 1e-5
SB_W = 512
SWA_QW = 512
SWA_KVW = 128
FOX_W = 1024
NEG_BIG = -1e30
LOG2E = math.log2(math.e)
QK_SCALE = HEAD_DIM ** -0.5
GATE_PARTS = 3
VMEM_LIMIT = 56 * 1024 * 1024

BF16 = jnp.bfloat16
F32 = jnp.float32


def _dot(a, b):
    return jnp.dot(a, b, preferred_element_type=F32)


def _dot_nt(a, b):
    return lax.dot_general(a, b, (((1,), (1,)), ((), ())), preferred_element_type=F32)


def _dot_tn(a, b):
    return lax.dot_general(a, b, (((0,), (0,)), ((), ())), preferred_element_type=F32)


def _layer_norm(z, g, b):
    mu = jnp.mean(z, axis=-1, keepdims=True)
    d = z - mu
    var = jnp.mean(d * d, axis=-1, keepdims=True)
    return d * lax.rsqrt(var + LN_EPS) * g + b


def _log2_sigmoid(x2):
    return jnp.minimum(x2, 0.0) - jnp.log2(1.0 + jnp.exp2(-jnp.abs(x2)))


def _split_bf16(x, parts):
    out = []
    r = x
    for i in range(parts):
        t = r.astype(BF16)
        out.append(t)
        if i + 1 < parts:
            r = r - t.astype(F32)
    return out


def _params(sem):
    return pltpu.CompilerParams(dimension_semantics=sem, vmem_limit_bytes=VMEM_LIMIT)


def _resident(shape, index_map):
    return pl.BlockSpec(shape, index_map, pipeline_mode=pl.Buffered(1))


def _ffn_ln_kernel(x_ref, wg_ref, wu_ref, wo_ref, g_ref, b_ref, o_ref):
    x = x_ref[...]
    xb = x.astype(BF16)
    gate = _dot(xb, wg_ref[...])
    up = _dot(xb, wu_ref[...])
    h = (gate * jax.nn.sigmoid(gate) * up).astype(BF16)
    y = _dot(h, wo_ref[...])
    o_ref[...] = _layer_norm(ALPHA * x + 0.5 * y, g_ref[...], b_ref[...])


def _ffn_ln(x, w_in, w_out, g, b, tm=256):
    n = x.shape[0]
    return pl.pallas_call(
        _ffn_ln_kernel,
        grid=(n // tm,),
        in_specs=[
            pl.BlockSpec((tm, D_MODEL), lambda i: (i, 0)),
            _resident((D_MODEL, D_FF), lambda i: (0, 0)),
            _resident((D_MODEL, D_FF), lambda i: (0, 1)),
            _resident((D_FF, D_MODEL), lambda i: (0, 0)),
            _resident((1, D_MODEL), lambda i: (0, 0)),
            _resident((1, D_MODEL), lambda i: (0, 0)),
        ],
        out_specs=pl.BlockSpec((tm, D_MODEL), lambda i: (i, 0)),
        out_shape=jax.ShapeDtypeStruct((n, D_MODEL), F32),
        compiler_params=_params(("parallel",)),
        name="ffn_ln",
    )(x, w_in, w_in, w_out, g, b)


def _proj_kernel(x_ref, w_ref, wq_ref, o_ref, qt_ref, *tail_ref, q_scale):
    xb = x_ref[...].astype(BF16)
    width = o_ref.shape[-1]
    acc = _dot(xb, w_ref[...])
    o_ref[...] = acc[:, :width].astype(o_ref.dtype)
    for ref in tail_ref:
        ref[...] = acc[:, width:]
    qt_ref[...] = (_dot_nt(wq_ref[...], xb) * q_scale).astype(qt_ref.dtype)


def _proj(x, w, wq_t, q_scale, tail=0, tm=512):
    b, s, _ = x.shape
    width, qw = w.shape[1] - tail, wq_t.shape[0]
    tok = lambda n: pl.BlockSpec((None, tm, n), lambda i, j: (i, j, 0))
    out_specs = [tok(width), pl.BlockSpec((None, qw, tm), lambda i, j: (i, 0, j))]
    out_shape = [jax.ShapeDtypeStruct((b, s, width), BF16), jax.ShapeDtypeStruct((b, qw, s), BF16)]
    if tail:
        out_specs.append(tok(tail))
        out_shape.append(jax.ShapeDtypeStruct((b, s, tail), F32))
    return pl.pallas_call(
        functools.partial(_proj_kernel, q_scale=q_scale),
        grid=(b, s // tm),
        in_specs=[
            tok(D_MODEL),
            _resident((D_MODEL, width + tail), lambda i, j: (0, 0)),
            _resident((qw, D_MODEL), lambda i, j: (0, 0)),
        ],
        out_specs=out_specs,
        out_shape=out_shape,
        compiler_params=_params(("parallel", "parallel")),
        name="in_proj",
    )(x, w, wq_t)


def _outproj_ln_kernel(ot_ref, h_ref, w_ref, g_ref, b_ref, out_ref):
    mix = _dot_tn(ot_ref[...], w_ref[...])
    out_ref[...] = _layer_norm(ALPHA * h_ref[...] + mix, g_ref[...], b_ref[...])


def _outproj2_ln_kernel(ot_ref, o_ref, h_ref, wt_ref, w_ref, g_ref, b_ref, out_ref):
    mix = _dot_tn(ot_ref[...], wt_ref[...]) + _dot(o_ref[...], w_ref[...])
    out_ref[...] = _layer_norm(ALPHA * h_ref[...] + mix, g_ref[...], b_ref[...])


def _outproj_ln(o_t, o, h, w, g, b, tm=512):
    bsz, s, _ = h.shape
    wt = o_t.shape[1]
    tok = lambda width: pl.BlockSpec((None, tm, width), lambda i, j: (i, j, 0))
    row = _resident((1, D_MODEL), lambda i, j: (0, 0))
    ot_spec = pl.BlockSpec((None, wt, tm), lambda i, j: (i, 0, j))
    if o is None:
        body = _outproj_ln_kernel
        in_specs = [ot_spec, tok(D_MODEL), _resident((wt, D_MODEL), lambda i, j: (0, 0)), row, row]
        args = (o_t, h, w, g, b)
    else:
        wn = o.shape[2]
        body = _outproj2_ln_kernel
        in_specs = [ot_spec, tok(wn), tok(D_MODEL), _resident((wt, D_MODEL), lambda i, j: (0, 0)),
                    _resident((wn, D_MODEL), lambda i, j: (wt // wn, 0)), row, row]
        args = (o_t, o, h, w, w, g, b)
    return pl.pallas_call(
        body,
        grid=(bsz, s // tm),
        in_specs=in_specs,
        out_specs=tok(D_MODEL),
        out_shape=jax.ShapeDtypeStruct((bsz, s, D_MODEL), F32),
        compiler_params=_params(("parallel", "parallel")),
        name="out_proj_ln",
    )(*args)


def _head_pair(q):
    lo = lax.broadcasted_iota(jnp.int32, q.shape, 1) < HEAD_DIM
    zero = jnp.zeros_like(q)
    return jnp.where(lo, q, zero), jnp.where(lo, zero, q)


def _merge_pair(a0, a1):
    lo = lax.broadcasted_iota(jnp.int32, a0.shape, 1) < HEAD_DIM
    return jnp.where(lo, a0, a1)


def _head_operands_t(q_t):
    top = lax.broadcasted_iota(jnp.int32, (LANES, q_t.shape[1]), 0) < HEAD_DIM
    out = []
    for p in range(q_t.shape[0] // LANES):
        tile = q_t[p * LANES:(p + 1) * LANES]
        zero = jnp.zeros_like(tile)
        out += [jnp.where(top, tile, zero), jnp.where(top, zero, tile)]
    return out


def _head_rows(pv, h):
    return pv[(h % 2) * HEAD_DIM:(h % 2 + 1) * HEAD_DIM]


def _pair_cols(ref, start, t, h):
    return ref[pl.ds(start, t), (h // 2) * LANES:(h // 2 + 1) * LANES]


def _sb_kernel(qt_ref, k_ref, v_ref, o_ref, *, t):
    qi = pl.program_id(2)
    w_heads = _head_operands_t(qt_ref[...])
    heads = range(len(w_heads))
    row = lax.broadcasted_iota(jnp.int32, (t, t), 0)
    col = lax.broadcasted_iota(jnp.int32, (t, t), 1)
    strict = row < col
    suffix = jnp.where(col >= row, 1.0, 0.0).astype(BF16)
    suffix2 = jnp.concatenate([suffix, suffix], axis=1)
    hi_mask = jnp.uint32(0xFFFF0000)

    def step(kb, carries, masked):
        start = pl.multiple_of(kb * t, t)
        nz = [_dot(_pair_cols(k_ref, start, t, h), w_heads[h]) for h in heads]
        incl = []
        for h in heads:
            log_keep = _log2_sigmoid(nz[h])
            if masked:
                log_keep = jnp.where(strict, log_keep, 0.0)
            hi = lax.bitcast_convert_type(lax.bitcast_convert_type(log_keep, jnp.uint32) & hi_mask, F32)
            parts = jnp.concatenate([hi.astype(BF16), (log_keep - hi).astype(BF16)], axis=0)
            incl.append(_dot(suffix2, parts))
        pv = []
        for h in heads:
            w = jnp.exp2(incl[h] + carries[h] - nz[h])
            if masked:
                w = jnp.where(strict, w, 0.0)
            pv.append(_head_rows(_dot_tn(_pair_cols(v_ref, start, t, h), w.astype(BF16)), h))
        return pv, [carries[h] + incl[h][0:1, :] for h in heads]

    acc, carries = step(qi, [jnp.zeros((1, t), F32) for _ in heads], True)

    def body(j, state):
        acc, carries = state
        pv, carries = step(qi - 1 - j, carries, False)
        return [a + p for a, p in zip(acc, pv)], carries

    acc, _ = lax.fori_loop(0, qi, body, (acc, carries))
    for h in heads:
        o_ref[h * HEAD_DIM:(h + 1) * HEAD_DIM, :] = acc[h].astype(o_ref.dtype)


def _sb_attention(proj, q_t, t=256, pairs=2):
    b, s, _ = proj.shape
    groups = N_SB_PAIRS // pairs
    return pl.pallas_call(
        functools.partial(_sb_kernel, t=t),
        grid=(b, groups, s // t),
        in_specs=[
            pl.BlockSpec((None, pairs * LANES, t), lambda i, h, j: (i, h, j)),
            pl.BlockSpec((None, s, pairs * LANES), lambda i, h, j: (i, 0, h)),
            pl.BlockSpec((None, s, pairs * LANES), lambda i, h, j: (i, 0, groups + h)),
        ],
        out_specs=pl.BlockSpec((None, pairs * LANES, t), lambda i, h, j: (i, h, j)),
        out_shape=jax.ShapeDtypeStruct((b, SB_W, s), BF16),
        compiler_params=_params(("parallel", "parallel", "arbitrary")),
        name="sb_attention",
    )(q_t, proj, proj)


def _t5_bucket_table():
    n = np.arange(WINDOW)
    max_exact = N_BUCKETS // 2
    nf = np.maximum(n, 1).astype(np.float64)
    val = np.log(nf / max_exact) / math.log(MAX_DISTANCE / max_exact) * (N_BUCKETS - max_exact)
    frac = np.abs(val - np.round(val))[n > max_exact]
    assert frac.min() > 1e-3, "bucket boundary too close to an integer distance for float32"
    large = np.minimum(max_exact + val.astype(np.int64), N_BUCKETS - 1)
    return np.where(n < max_exact, n, large)


def _swa_bucket_tiles():
    table = _t5_bucket_table()
    qi = np.arange(WINDOW)[:, None]
    kj = np.arange(2 * WINDOW)[None, :]
    tiles = []
    for offset in (WINDOW, 0):
        rel = qi + offset - kj
        valid = (rel >= 0) & (rel < WINDOW)
        tiles.append(np.where(valid, table[np.clip(rel, 0, WINDOW - 1)], -1))
    return np.stack(tiles).astype(np.int32)


def _swa_bias_kernel(rb_ref, idx_ref, o_ref):
    h = pl.program_id(0)
    for tile in range(2):
        idx = idx_ref[tile]
        acc = jnp.full(idx.shape, NEG_BIG, F32)
        for bucket in range(N_BUCKETS):
            acc = jnp.where(idx == bucket, rb_ref[bucket, h], acc)
        o_ref[tile, 0] = acc


def _swa_bias(rel_bias):
    idx = jnp.asarray(_swa_bucket_tiles())
    return pl.pallas_call(
        _swa_bias_kernel,
        grid=(N_SWA_HEADS,),
        in_specs=[
            pl.BlockSpec(memory_space=pltpu.SMEM),
            pl.BlockSpec((2, WINDOW, 2 * WINDOW), lambda h: (0, 0, 0)),
        ],
        out_specs=pl.BlockSpec((2, 1, WINDOW, 2 * WINDOW), lambda h: (0, h, 0, 0)),
        out_shape=jax.ShapeDtypeStruct((2, N_SWA_HEADS, WINDOW, 2 * WINDOW), F32),
        compiler_params=_params(("arbitrary",)),
        name="swa_bias",
    )(rel_bias, idx)


def _swa_kernel(sink_ref, q_ref, k_ref, v_ref, bias_ref, o_ref, *, tq):
    qt = pl.program_id(1)
    scale = jnp.asarray(QK_SCALE, BF16)
    group = N_SWA_HEADS // N_SWA_KV
    for j in range(tq // WINDOW):
        nb = qt * (tq // WINDOW) + j
        first = (nb == 0).astype(jnp.int32) if j == 0 else 0
        w0 = pl.multiple_of(jnp.maximum(nb - 1, 0) * WINDOW, WINDOW)
        rows = slice(j * WINDOW, (j + 1) * WINDOW)
        for g in range(N_SWA_KV):
            kwin = k_ref[pl.ds(w0, 2 * WINDOW), g * LANES:(g + 1) * LANES]
            vwin = v_ref[pl.ds(w0, 2 * WINDOW), g * LANES:(g + 1) * LANES]
            for c in range(group // 2):
                pair = g * (group // 2) + c
                cols = slice(pair * LANES, (pair + 1) * LANES)
                q_heads = _head_pair(q_ref[rows, cols] * scale)
                outs = []
                for i in range(2):
                    head = 2 * pair + i
                    sink = sink_ref[head]
                    logits = _dot_nt(q_heads[i], kwin) + bias_ref[first, head]
                    m = jnp.maximum(jnp.max(logits, axis=-1, keepdims=True), sink)
                    p = jnp.exp(logits - m)
                    denom = jnp.sum(p, axis=-1, keepdims=True) + jnp.exp(sink - m)
                    outs.append(_dot(p.astype(BF16), vwin) / denom)
                o_ref[rows, cols] = _merge_pair(outs[0], outs[1]).astype(o_ref.dtype)


def _swa_attention(proj, bias, sinks, tq=512):
    b, s, width = proj.shape
    q_blk = (width - SWA_QW - 4 * LANES) // SWA_QW
    k_blk = (width - 4 * LANES) // (2 * LANES)
    return pl.pallas_call(
        functools.partial(_swa_kernel, tq=tq),
        grid=(b, s // tq),
        in_specs=[
            pl.BlockSpec(memory_space=pltpu.SMEM),
            pl.BlockSpec((None, tq, SWA_QW), lambda i, j: (i, j, q_blk)),
            pl.BlockSpec((None, s, 2 * LANES), lambda i, j: (i, 0, k_blk)),
            pl.BlockSpec((None, s, 2 * LANES), lambda i, j: (i, 0, k_blk + 1)),
            _resident((2, N_SWA_HEADS, WINDOW, 2 * WINDOW), lambda i, j: (0, 0, 0, 0)),
        ],
        out_specs=pl.BlockSpec((None, tq, SWA_QW), lambda i, j: (i, j, 0)),
        out_shape=jax.ShapeDtypeStruct((b, s, SWA_QW), BF16),
        compiler_params=_params(("parallel", "arbitrary")),
        name="swa_attention",
    )(sinks, proj, proj, proj, bias)


def _fox_gate_kernel(f_ref, bf_ref, c_ref, *, chunk):
    s = f_ref.shape[0]
    r = lax.broadcasted_iota(jnp.int32, (chunk, chunk), 0)
    c = lax.broadcasted_iota(jnp.int32, (chunk, chunk), 1)
    upto = jnp.where(c <= r, 1.0, 0.0).astype(BF16)
    lane = lax.broadcasted_iota(jnp.int32, (chunk, LANES), 1)
    carry = jnp.zeros((1, LANES), F32)
    for j in range(s // chunk):
        rows = slice(j * chunk, (j + 1) * chunk)
        log_f = _log2_sigmoid((f_ref[rows, :] + bf_ref[...]) * LOG2E)
        cs = carry
        for part in _split_bf16(log_f, 3):
            cs = cs + _dot(upto, part)
        carry = cs[chunk - 1:chunk, :]
        out = jnp.zeros((chunk, LANES), BF16)
        for k, part in reversed(list(enumerate(_split_bf16(cs, GATE_PARTS)))):
            out = jnp.where(lane < (k + 1) * N_FOX_HEADS, part, out)
        c_ref[rows, :] = out


def _fox_gates(f, b_f, chunk=256):
    b, s, _ = f.shape
    return pl.pallas_call(
        functools.partial(_fox_gate_kernel, chunk=chunk),
        grid=(b,),
        in_specs=[
            pl.BlockSpec((None, s, LANES), lambda i: (i, 0, 0)),
            _resident((1, LANES), lambda i: (0, 0)),
        ],
        out_specs=pl.BlockSpec((None, s, LANES), lambda i: (i, 0, 0)),
        out_shape=jax.ShapeDtypeStruct((b, s, LANES), BF16),
        compiler_params=_params(("parallel",)),
        name="fox_gates",
    )(f, b_f)


def _fox_kernel(qt_ref, k_ref, v_ref, c_ref, o_ref, *, t):
    qi = pl.program_id(2)
    q_heads = _head_operands_t(qt_ref[...])
    heads = range(len(q_heads))
    first_head = pl.program_id(1) * len(q_heads)
    gate_row = lax.broadcasted_iota(jnp.int32, (LANES, t), 0)
    row = lax.broadcasted_iota(jnp.int32, (t, t), 0)
    col = lax.broadcasted_iota(jnp.int32, (t, t), 1)
    causal = row <= col
    w_heads = []
    for h in heads:
        picks = ((gate_row & (N_FOX_HEADS - 1)) == first_head + h) & (gate_row < GATE_PARTS * N_FOX_HEADS)
        minus_gate = jnp.where(picks, -1.0, 0.0).astype(BF16)
        w_heads.append(jnp.concatenate([q_heads[h], minus_gate], axis=0))

    def step(kb, state, masked):
        start = pl.multiple_of(kb * t, t)
        gates = c_ref[pl.ds(start, t), :]
        s2 = [_dot(jnp.concatenate([_pair_cols(k_ref, start, t, h), gates], axis=1), w_heads[h])
              for h in heads]
        out = []
        for h in heads:
            m, l, acc = state[h]
            s2h = jnp.where(causal, s2[h], NEG_BIG) if masked else s2[h]
            m_new = jnp.maximum(m, jnp.max(s2h, axis=0, keepdims=True))
            a = jnp.exp2(m - m_new)
            p = jnp.exp2(s2h - m_new)
            l = a * l + jnp.sum(p, axis=0, keepdims=True)
            pv = _head_rows(_dot_tn(_pair_cols(v_ref, start, t, h), p.astype(BF16)), h)
            out.append((m_new, l, a * acc + pv))
        return out

    init = (jnp.full((1, t), NEG_BIG, F32), jnp.zeros((1, t), F32), jnp.zeros((HEAD_DIM, t), F32))
    state = lax.fori_loop(0, qi, lambda kb, st: step(kb, st, False), [init for _ in heads])
    state = step(qi, state, True)
    for h in heads:
        _, l, acc = state[h]
        o_ref[h * HEAD_DIM:(h + 1) * HEAD_DIM, :] = (acc / l).astype(o_ref.dtype)


def _fox_attention(proj, q_t, gates, t=256, pairs=2):
    b, s, _ = proj.shape
    groups = N_FOX_PAIRS // pairs
    return pl.pallas_call(
        functools.partial(_fox_kernel, t=t),
        grid=(b, groups, s // t),
        in_specs=[
            pl.BlockSpec((None, pairs * LANES, t), lambda i, h, j: (i, h, j)),
            pl.BlockSpec((None, s, pairs * LANES), lambda i, h, j: (i, 0, h)),
            pl.BlockSpec((None, s, pairs * LANES), lambda i, h, j: (i, 0, groups + h)),
            pl.BlockSpec((None, s, LANES), lambda i, h, j: (i, 0, 0)),
        ],
        out_specs=pl.BlockSpec((None, pairs * LANES, t), lambda i, h, j: (i, h, j)),
        out_shape=jax.ShapeDtypeStruct((b, FOX_W, s), BF16),
        compiler_params=_params(("parallel", "parallel", "arbitrary")),
        name="fox_attention",
    )(q_t, proj, proj, gates)


def _even_mixer(h, w_in, w_out, sinks, rel_bias, g, b):
    kv0 = 3 * SB_W + SWA_QW
    dup = lambda w: jnp.repeat(w.reshape(D_MODEL, N_SWA_KV, HEAD_DIM), 2, axis=1).reshape(D_MODEL, -1)
    w_cat = jnp.concatenate(
        [w_in[:, SB_W:kv0], dup(w_in[:, kv0:kv0 + SWA_KVW]), dup(w_in[:, kv0 + SWA_KVW:])], axis=1)
    proj, q_t = _proj(h, w_cat.astype(BF16), w_in[:, :SB_W].T.astype(BF16), -QK_SCALE * LOG2E)
    o_sb = _sb_attention(proj, q_t)
    o_sw = _swa_attention(proj, _swa_bias(rel_bias), sinks.astype(F32))
    return _outproj_ln(o_sb, o_sw, h, w_out.astype(BF16), g, b)


def _odd_mixer(h, w_in, b_f, w_out, g, b):
    spread = lambda v: jnp.pad(jnp.tile(v, (1, GATE_PARTS)), ((0, 0), (0, LANES - GATE_PARTS * N_FOX_HEADS)))
    w_cat = jnp.concatenate([w_in[:, FOX_W:3 * FOX_W], spread(w_in[:, 3 * FOX_W:])], axis=1)
    proj, q_t, f = _proj(h, w_cat.astype(BF16), w_in[:, :FOX_W].T.astype(BF16), QK_SCALE * LOG2E,
                         tail=LANES)
    gates = _fox_gates(f, spread(b_f.reshape(1, -1).astype(F32)))
    o = _fox_attention(proj, q_t, gates)
    return _outproj_ln(o, None, h, w_out.astype(BF16), g, b)


def kernel(x, ln_g, ln_b, ffn1_in, ffn1_out, ffn2_in, ffn2_out, ab_w_in, ab_w_out, ab_sinks,
           fox_w_in, fox_b_f, fox_w_out, rel_bias):
    bsz, s, d = x.shape
    assert (s, d) == (SEQ, D_MODEL) and ln_g.shape[0] == DEPTH
    row = lambda v: v.reshape(1, D_MODEL).astype(F32)
    ffn = lambda h, w_in, w_out, g, b: _ffn_ln(
        h.reshape(bsz * s, d), w_in.astype(BF16), w_out.astype(BF16), row(g), row(b)).reshape(bsz, s, d)
    h = x
    for layer in range(DEPTH):
        g, b = ln_g[layer], ln_b[layer]
        h = ffn(h, ffn1_in[layer], ffn1_out[layer], g[0], b[0])
        i = layer // 2
        if layer % 2 == 0:
            h = _even_mixer(h, ab_w_in[i], ab_w_out[i], ab_sinks[i], rel_bias, row(g[1]), row(b[1]))
        else:
            h = _odd_mixer(h, fox_w_in[i], fox_b_f[i], fox_w_out[i], row(g[1]), row(b[1]))
        h = ffn(h, ffn2_in[layer], ffn2_out[layer], g[2], b[2])
    return h
```

```python
import functools
import math

import numpy as np
import jax
import jax.numpy as jnp
from jax import lax
from jax.experimental import pallas as pl
from jax.experimental.pallas import tpu as pltpu

D_MODEL = 1024
SEQ = 2048
DEPTH = 2
HEAD_DIM = 64
LANES = 128
N_SB_PAIRS = 4
N_SWA_HEADS = 8
N_SWA_KV = 2
N_FOX_HEADS = 16
N_FOX_PAIRS = 8
WINDOW = 128
N_BUCKETS = 32
MAX_DISTANCE = 128
D_FF = 2816
ALPHA = (2 * DEPTH) ** 0.25
LN_EPS = 1e-5
SB_W = 512
SWA_QW = 512
SWA_KVW = 128
FOX_W = 1024
NEG_BIG = -1e30
LOG2E = math.log2(math.e)
QK_SCALE = HEAD_DIM ** -0.5
GATE_PARTS = 3
VMEM_LIMIT = 56 * 1024 * 1024

BF16 = jnp.bfloat16
F32 = jnp.float32


def _dot(a, b):
    return jnp.dot(a, b, preferred_element_type=F32)


def _dot_nt(a, b):
    return lax.dot_general(a, b, (((1,), (1,)), ((), ())), preferred_element_type=F32)


def _dot_tn(a, b):
    return lax.dot_general(a, b, (((0,), (0,)), ((), ())), preferred_element_type=F32)


def _layer_norm(z, g, b):
    mu = jnp.mean(z, axis=-1, keepdims=True)
    d = z - mu
    var = jnp.mean(d * d, axis=-1, keepdims=True)
    return d * lax.rsqrt(var + LN_EPS) * g + b


def _log2_sigmoid(x2):
    sign = jnp.uint32(0x80000000)
    neg_abs = lax.bitcast_convert_type(lax.bitcast_convert_type(x2, jnp.uint32) | sign, F32)
    return jnp.minimum(x2, 0.0) - jnp.log2(1.0 + jnp.exp2(neg_abs))


def _split_bf16(x, parts):
    out = []
    r = x
    for i in range(parts):
        t = r.astype(BF16)
        out.append(t)
        if i + 1 < parts:
            r = r - t.astype(F32)
    return out


def _params(sem):
    return pltpu.CompilerParams(dimension_semantics=sem, vmem_limit_bytes=VMEM_LIMIT)


def _resident(shape, index_map):
    return pl.BlockSpec(shape, index_map, pipeline_mode=pl.Buffered(1))


def _ffn_ln_kernel(x_ref, wg_ref, wu_ref, wo_ref, g_ref, b_ref, o_ref):
    x = x_ref[...]
    xb = x.astype(BF16)
    gate = _dot(xb, wg_ref[...])
    up = _dot(xb, wu_ref[...])
    h = (gate * jax.nn.sigmoid(gate) * up).astype(BF16)
    y = _dot(h, wo_ref[...])
    o_ref[...] = _layer_norm(ALPHA * x + 0.5 * y, g_ref[...], b_ref[...])


def _ffn_ln(x, w_in, w_out, g, b, tm=512):
    n = x.shape[0]
    return pl.pallas_call(
        _ffn_ln_kernel,
        grid=(n // tm,),
        in_specs=[
            pl.BlockSpec((tm, D_MODEL), lambda i: (i, 0)),
            _resident((D_MODEL, D_FF), lambda i: (0, 0)),
            _resident((D_MODEL, D_FF), lambda i: (0, 1)),
            _resident((D_FF, D_MODEL), lambda i: (0, 0)),
            _resident((1, D_MODEL), lambda i: (0, 0)),
            _resident((1, D_MODEL), lambda i: (0, 0)),
        ],
        out_specs=pl.BlockSpec((tm, D_MODEL), lambda i: (i, 0)),
        out_shape=jax.ShapeDtypeStruct((n, D_MODEL), F32),
        compiler_params=_params(("parallel",)),
        name="ffn_ln",
    )(x, w_in, w_in, w_out, g, b)


def _proj_kernel(x_ref, w_ref, wq_ref, o_ref, qt_ref, *tail_ref, q_scale):
    xb = x_ref[...].astype(BF16)
    width = o_ref.shape[-1]
    acc = _dot(xb, w_ref[...])
    o_ref[...] = acc[:, :width].astype(o_ref.dtype)
    for ref in tail_ref:
        ref[...] = acc[:, width:]
    qt_ref[...] = (_dot_nt(wq_ref[...], xb) * q_scale).astype(qt_ref.dtype)


def _proj(x, w, wq_t, q_scale, tail=0, tm=512):
    b, s, _ = x.shape
    width, qw = w.shape[1] - tail, wq_t.shape[0]
    tok = lambda n: pl.BlockSpec((None, tm, n), lambda i, j: (i, j, 0))
    out_specs = [tok(width), pl.BlockSpec((None, qw, tm), lambda i, j: (i, 0, j))]
    out_shape = [jax.ShapeDtypeStruct((b, s, width), BF16), jax.ShapeDtypeStruct((b, qw, s), BF16)]
    if tail:
        out_specs.append(tok(tail))
        out_shape.append(jax.ShapeDtypeStruct((b, s, tail), F32))
    return pl.pallas_call(
        functools.partial(_proj_kernel, q_scale=q_scale),
        grid=(b, s // tm),
        in_specs=[
            tok(D_MODEL),
            _resident((D_MODEL, width + tail), lambda i, j: (0, 0)),
            _resident((qw, D_MODEL), lambda i, j: (0, 0)),
        ],
        out_specs=out_specs,
        out_shape=out_shape,
        compiler_params=_params(("parallel", "parallel")),
        name="in_proj",
    )(x, w, wq_t)


def _outproj_ln_kernel(ot_ref, h_ref, w_ref, g_ref, b_ref, out_ref):
    mix = _dot_tn(ot_ref[...], w_ref[...])
    out_ref[...] = _layer_norm(ALPHA * h_ref[...] + mix, g_ref[...], b_ref[...])


def _outproj2_ln_kernel(ot_ref, o_ref, h_ref, wt_ref, w_ref, g_ref, b_ref, out_ref):
    mix = _dot_tn(ot_ref[...], wt_ref[...]) + _dot(o_ref[...], w_ref[...])
    out_ref[...] = _layer_norm(ALPHA * h_ref[...] + mix, g_ref[...], b_ref[...])


def _outproj_ln(o_t, o, h, w, g, b, tm=512):
    bsz, s, _ = h.shape
    wt = o_t.shape[1]
    tok = lambda width: pl.BlockSpec((None, tm, width), lambda i, j: (i, j, 0))
    row = _resident((1, D_MODEL), lambda i, j: (0, 0))
    ot_spec = pl.BlockSpec((None, wt, tm), lambda i, j: (i, 0, j))
    if o is None:
        body = _outproj_ln_kernel
        in_specs = [ot_spec, tok(D_MODEL), _resident((wt, D_MODEL), lambda i, j: (0, 0)), row, row]
        args = (o_t, h, w, g, b)
    else:
        wn = o.shape[2]
        body = _outproj2_ln_kernel
        in_specs = [ot_spec, tok(wn), tok(D_MODEL), _resident((wt, D_MODEL), lambda i, j: (0, 0)),
                    _resident((wn, D_MODEL), lambda i, j: (wt // wn, 0)), row, row]
        args = (o_t, o, h, w, w, g, b)
    return pl.pallas_call(
        body,
        grid=(bsz, s // tm),
        in_specs=in_specs,
        out_specs=tok(D_MODEL),
        out_shape=jax.ShapeDtypeStruct((bsz, s, D_MODEL), F32),
        compiler_params=_params(("parallel", "parallel")),
        name="out_proj_ln",
    )(*args)


def _head_pair(q):
    lo = lax.broadcasted_iota(jnp.int32, q.shape, 1) < HEAD_DIM
    zero = jnp.zeros_like(q)
    return jnp.where(lo, q, zero), jnp.where(lo, zero, q)


def _merge_pair(a0, a1):
    lo = lax.broadcasted_iota(jnp.int32, a0.shape, 1) < HEAD_DIM
    return jnp.where(lo, a0, a1)


def _head_operands_t(q_t):
    top = lax.broadcasted_iota(jnp.int32, (LANES, q_t.shape[1]), 0) < HEAD_DIM
    out = []
    for p in range(q_t.shape[0] // LANES):
        tile = q_t[p * LANES:(p + 1) * LANES]
        zero = jnp.zeros_like(tile)
        out += [jnp.where(top, tile, zero), jnp.where(top, zero, tile)]
    return out


def _head_rows(pv, h):
    return pv[(h % 2) * HEAD_DIM:(h % 2 + 1) * HEAD_DIM]


def _pair_cols(ref, start, t, h):
    return ref[pl.ds(start, t), (h // 2) * LANES:(h // 2 + 1) * LANES]


def _sb_kernel(qt_ref, k_ref, v_ref, o_ref, nza_ref, nzb_ref, wa_ref, wb_ref, acc_ref, *, t):
    qi = pl.program_id(2)
    w_heads = _head_operands_t(qt_ref[...])
    heads = range(len(w_heads))
    row = lax.broadcasted_iota(jnp.int32, (t, t), 0)
    col = lax.broadcasted_iota(jnp.int32, (t, t), 1)
    strict = row < col
    suffix = jnp.where(col >= row, 1.0, 0.0).astype(BF16)
    suffix2 = jnp.concatenate([suffix, suffix], axis=1)
    hi_mask = jnp.uint32(0xFFFF0000)

    def scores(kb, nz_ref):
        start = pl.multiple_of(kb * t, t)
        for h in heads:
            nz_ref[h] = _dot(_pair_cols(k_ref, start, t, h), w_heads[h])

    def accumulate(kb, w_ref):
        start = pl.multiple_of(kb * t, t)
        for h in heads:
            acc_ref[h] += _head_rows(_dot_tn(_pair_cols(v_ref, start, t, h), w_ref[h]), h)

    def weights(nz_ref, w_ref, carries, masked):
        incl = []
        for h in heads:
            log_keep = _log2_sigmoid(nz_ref[h])
            if masked:
                log_keep = jnp.where(strict, log_keep, 0.0)
            hi = lax.bitcast_convert_type(lax.bitcast_convert_type(log_keep, jnp.uint32) & hi_mask, F32)
            parts = jnp.concatenate([hi.astype(BF16), (log_keep - hi).astype(BF16)], axis=0)
            incl.append(_dot(suffix2, parts))
        for h in heads:
            w = jnp.exp2(incl[h] + carries[h] - nz_ref[h])
            if masked:
                w = jnp.where(strict, w, 0.0)
            w_ref[h] = w.astype(BF16)
        return [carries[h] + incl[h][0:1, :] for h in heads]

    def half_step(j, cur, oth, carries, *, prefetch, masked=False, first=False):
        kb = qi - j
        if prefetch:
            scores(jnp.maximum(kb - 1, 0), oth[0])
        if not first:
            accumulate(kb + 1, oth[1])
        return weights(cur[0], cur[1], carries, masked)

    slot_a, slot_b = (nza_ref, wa_ref), (nzb_ref, wb_ref)
    acc_ref[...] = jnp.zeros_like(acc_ref)
    scores(qi, nza_ref)
    carries = half_step(0, slot_a, slot_b, [jnp.zeros((1, t), F32) for _ in heads],
                        prefetch=True, masked=True, first=True)

    def body(i, carries):
        carries = half_step(2 * i + 1, slot_b, slot_a, carries, prefetch=True)
        return half_step(2 * i + 2, slot_a, slot_b, carries, prefetch=True)

    carries = lax.fori_loop(0, lax.shift_right_logical(qi, 1), body, carries)

    @pl.when((qi & 1) == 1)
    def _():
        half_step(qi, slot_b, slot_a, carries, prefetch=False)
        accumulate(0, wb_ref)

    @pl.when((qi & 1) == 0)
    def _():
        accumulate(0, wa_ref)

    for h in heads:
        o_ref[h * HEAD_DIM:(h + 1) * HEAD_DIM, :] = acc_ref[h].astype(o_ref.dtype)


def _sb_attention(proj, q_t, t=256, pairs=2):
    b, s, _ = proj.shape
    groups = N_SB_PAIRS // pairs
    return pl.pallas_call(
        functools.partial(_sb_kernel, t=t),
        grid=(b, groups, s // t),
        in_specs=[
            pl.BlockSpec((None, pairs * LANES, t), lambda i, h, j: (i, h, j)),
            pl.BlockSpec((None, s, pairs * LANES), lambda i, h, j: (i, 0, h)),
            pl.BlockSpec((None, s, pairs * LANES), lambda i, h, j: (i, 0, groups + h)),
        ],
        out_specs=pl.BlockSpec((None, pairs * LANES, t), lambda i, h, j: (i, h, j)),
        out_shape=jax.ShapeDtypeStruct((b, SB_W, s), BF16),
        scratch_shapes=[
            pltpu.VMEM((2 * pairs, t, t), F32), pltpu.VMEM((2 * pairs, t, t), F32),
            pltpu.VMEM((2 * pairs, t, t), BF16), pltpu.VMEM((2 * pairs, t, t), BF16),
            pltpu.VMEM((2 * pairs, HEAD_DIM, t), F32),
        ],
        compiler_params=_params(("parallel", "parallel", "arbitrary")),
        name="sb_attention",
    )(q_t, proj, proj)


def _t5_bucket_table():
    n = np.arange(WINDOW)
    max_exact = N_BUCKETS // 2
    nf = np.maximum(n, 1).astype(np.float64)
    val = np.log(nf / max_exact) / math.log(MAX_DISTANCE / max_exact) * (N_BUCKETS - max_exact)
    frac = np.abs(val - np.round(val))[n > max_exact]
    assert frac.min() > 1e-3, "bucket boundary too close to an integer distance for float32"
    large = np.minimum(max_exact + val.astype(np.int64), N_BUCKETS - 1)
    return np.where(n < max_exact, n, large)


def _swa_bucket_tiles():
    table = _t5_bucket_table()
    qi = np.arange(WINDOW)[:, None]
    kj = np.arange(2 * WINDOW)[None, :]
    tiles = []
    for offset in (WINDOW, 0):
        rel = qi + offset - kj
        valid = (rel >= 0) & (rel < WINDOW)
        tiles.append(np.where(valid, table[np.clip(rel, 0, WINDOW - 1)], -1))
    return np.stack(tiles).astype(np.int32)


def _swa_bias_kernel(rb_ref, idx_ref, o_ref):
    h = pl.program_id(0)
    for tile in range(2):
        idx = idx_ref[tile]
        acc = jnp.full(idx.shape, NEG_BIG, F32)
        for bucket in range(N_BUCKETS):
            acc = jnp.where(idx == bucket, rb_ref[bucket, h], acc)
        o_ref[tile, 0] = acc


def _swa_bias(rel_bias):
    idx = jnp.asarray(_swa_bucket_tiles())
    return pl.pallas_call(
        _swa_bias_kernel,
        grid=(N_SWA_HEADS,),
        in_specs=[
            pl.BlockSpec(memory_space=pltpu.SMEM),
            pl.BlockSpec((2, WINDOW, 2 * WINDOW), lambda h: (0, 0, 0)),
        ],
        out_specs=pl.BlockSpec((2, 1, WINDOW, 2 * WINDOW), lambda h: (0, h, 0, 0)),
        out_shape=jax.ShapeDtypeStruct((2, N_SWA_HEADS, WINDOW, 2 * WINDOW), F32),
        compiler_params=_params(("arbitrary",)),
        name="swa_bias",
    )(rel_bias, idx)


def _swa_kernel(sink_ref, q_ref, k_ref, v_ref, bias_ref, o_ref, *, tq):
    qt = pl.program_id(1)
    scale = jnp.asarray(QK_SCALE, BF16)
    group = N_SWA_HEADS // N_SWA_KV
    for j in range(tq // WINDOW):
        nb = qt * (tq // WINDOW) + j
        first = (nb == 0).astype(jnp.int32) if j == 0 else 0
        w0 = pl.multiple_of(jnp.maximum(nb - 1, 0) * WINDOW, WINDOW)
        rows = slice(j * WINDOW, (j + 1) * WINDOW)
        for g in range(N_SWA_KV):
            kwin = k_ref[pl.ds(w0, 2 * WINDOW), g * LANES:(g + 1) * LANES]
            vwin = v_ref[pl.ds(w0, 2 * WINDOW), g * LANES:(g + 1) * LANES]
            for c in range(group // 2):
                pair = g * (group // 2) + c
                cols = slice(pair * LANES, (pair + 1) * LANES)
                q_heads = _head_pair(q_ref[rows, cols] * scale)
                outs = []
                for i in range(2):
                    head = 2 * pair + i
                    sink = sink_ref[head]
                    logits = _dot_nt(q_heads[i], kwin) + bias_ref[first, head]
                    m = jnp.maximum(jnp.max(logits, axis=-1, keepdims=True), sink)
                    p = jnp.exp(logits - m)
                    denom = jnp.sum(p, axis=-1, keepdims=True) + jnp.exp(sink - m)
                    outs.append(_dot(p.astype(BF16), vwin) / denom)
                o_ref[rows, cols] = _merge_pair(outs[0], outs[1]).astype(o_ref.dtype)


def _swa_attention(proj, bias, sinks, tq=512):
    b, s, width = proj.shape
    q_blk = (width - SWA_QW - 4 * LANES) // SWA_QW
    k_blk = (width - 4 * LANES) // (2 * LANES)
    return pl.pallas_call(
        functools.partial(_swa_kernel, tq=tq),
        grid=(b, s // tq),
        in_specs=[
            pl.BlockSpec(memory_space=pltpu.SMEM),
            pl.BlockSpec((None, tq, SWA_QW), lambda i, j: (i, j, q_blk)),
            pl.BlockSpec((None, s, 2 * LANES), lambda i, j: (i, 0, k_blk)),
            pl.BlockSpec((None, s, 2 * LANES), lambda i, j: (i, 0, k_blk + 1)),
            _resident((2, N_SWA_HEADS, WINDOW, 2 * WINDOW), lambda i, j: (0, 0, 0, 0)),
        ],
        out_specs=pl.BlockSpec((None, tq, SWA_QW), lambda i, j: (i, j, 0)),
        out_shape=jax.ShapeDtypeStruct((b, s, SWA_QW), BF16),
        compiler_params=_params(("parallel", "arbitrary")),
        name="swa_attention",
    )(sinks, proj, proj, proj, bias)


def _fox_gate_kernel(f_ref, bf_ref, c_ref, *, chunk):
    s = f_ref.shape[0]
    r = lax.broadcasted_iota(jnp.int32, (chunk, chunk), 0)
    c = lax.broadcasted_iota(jnp.int32, (chunk, chunk), 1)
    upto = jnp.where(c <= r, 1.0, 0.0).astype(BF16)
    lane = lax.broadcasted_iota(jnp.int32, (chunk, LANES), 1)
    carry = jnp.zeros((1, LANES), F32)
    for j in range(s // chunk):
        rows = slice(j * chunk, (j + 1) * chunk)
        log_f = _log2_sigmoid((f_ref[rows, :] + bf_ref[...]) * LOG2E)
        cs = carry
        for part in _split_bf16(log_f, 3):
            cs = cs + _dot(upto, part)
        carry = cs[chunk - 1:chunk, :]
        out = jnp.zeros((chunk, LANES), BF16)
        for k, part in reversed(list(enumerate(_split_bf16(cs, GATE_PARTS)))):
            out = jnp.where(lane < (k + 1) * N_FOX_HEADS, part, out)
        c_ref[rows, :] = out


def _fox_gates(f, b_f, chunk=256):
    b, s, _ = f.shape
    return pl.pallas_call(
        functools.partial(_fox_gate_kernel, chunk=chunk),
        grid=(b,),
        in_specs=[
            pl.BlockSpec((None, s, LANES), lambda i: (i, 0, 0)),
            _resident((1, LANES), lambda i: (0, 0)),
        ],
        out_specs=pl.BlockSpec((None, s, LANES), lambda i: (i, 0, 0)),
        out_shape=jax.ShapeDtypeStruct((b, s, LANES), BF16),
        compiler_params=_params(("parallel",)),
        name="fox_gates",
    )(f, b_f)


def _fox_kernel(qt_ref, k_ref, v_ref, c_ref, o_ref, sa_ref, sb_ref, pa_ref, pb_ref, acc_ref, *, t):
    qi = pl.program_id(2)
    q_heads = _head_operands_t(qt_ref[...])
    heads = range(len(q_heads))
    first_head = pl.program_id(1) * len(q_heads)
    gate_row = lax.broadcasted_iota(jnp.int32, (LANES, t), 0)
    row = lax.broadcasted_iota(jnp.int32, (t, t), 0)
    col = lax.broadcasted_iota(jnp.int32, (t, t), 1)
    causal = row <= col
    w_heads = []
    for h in heads:
        picks = ((gate_row & (N_FOX_HEADS - 1)) == first_head + h) & (gate_row < GATE_PARTS * N_FOX_HEADS)
        minus_gate = jnp.where(picks, -1.0, 0.0).astype(BF16)
        w_heads.append(jnp.concatenate([q_heads[h], minus_gate], axis=0))

    def scores(kb, s_ref):
        start = pl.multiple_of(kb * t, t)
        gates = c_ref[pl.ds(start, t), :]
        for h in heads:
            s_ref[h] = _dot(jnp.concatenate([_pair_cols(k_ref, start, t, h), gates], axis=1), w_heads[h])

    def accumulate(kb, p_ref, a):
        start = pl.multiple_of(kb * t, t)
        for h in heads:
            pv = _head_rows(_dot_tn(_pair_cols(v_ref, start, t, h), p_ref[h]), h)
            acc_ref[h] = a[h] * acc_ref[h] + pv

    def softmax(s_ref, p_ref, stats, masked):
        new_stats, a = [], []
        for h in heads:
            m, l = stats[h]
            s2 = jnp.where(causal, s_ref[h], NEG_BIG) if masked else s_ref[h]
            m_new = jnp.maximum(m, jnp.max(s2, axis=0, keepdims=True))
            ah = jnp.exp2(m - m_new)
            p = jnp.exp2(s2 - m_new)
            p_ref[h] = p.astype(BF16)
            new_stats.append((m_new, ah * l + jnp.sum(p, axis=0, keepdims=True)))
            a.append(ah)
        return new_stats, a

    def half_step(kb, cur, oth, stats, a_prev, *, prefetch, masked=False):
        if prefetch:
            scores(kb + 1, oth[0])
        accumulate(jnp.maximum(kb - 1, 0), oth[1], a_prev)
        return softmax(cur[0], cur[1], stats, masked)

    slot_a, slot_b = (sa_ref, pa_ref), (sb_ref, pb_ref)
    pb_ref[...] = jnp.zeros_like(pb_ref)
    acc_ref[...] = jnp.zeros_like(acc_ref)
    scores(0, sa_ref)
    stats = [(jnp.full((1, t), NEG_BIG, F32), jnp.zeros((1, t), F32)) for _ in heads]
    ones = [jnp.ones((1, t), F32) for _ in heads]

    def body(i, carry):
        stats, a = carry
        stats, a = half_step(2 * i, slot_a, slot_b, stats, a, prefetch=True)
        return half_step(2 * i + 1, slot_b, slot_a, stats, a, prefetch=True)

    stats, a = lax.fori_loop(0, lax.shift_right_logical(qi, 1), body, (stats, ones))

    def odd_tail(stats, a):
        stats, a = half_step(qi - 1, slot_a, slot_b, stats, a, prefetch=True)
        stats, a = half_step(qi, slot_b, slot_a, stats, a, prefetch=False, masked=True)
        accumulate(qi, pb_ref, a)
        return [l for _, l in stats]

    def even_tail(stats, a):
        stats, a = half_step(qi, slot_a, slot_b, stats, a, prefetch=False, masked=True)
        accumulate(qi, pa_ref, a)
        return [l for _, l in stats]

    denom = lax.cond((qi & 1) == 1, odd_tail, even_tail, stats, a)
    for h in heads:
        o_ref[h * HEAD_DIM:(h + 1) * HEAD_DIM, :] = (acc_ref[h] / denom[h]).astype(o_ref.dtype)


def _fox_attention(proj, q_t, gates, t=256, pairs=2):
    b, s, _ = proj.shape
    groups = N_FOX_PAIRS // pairs
    return pl.pallas_call(
        functools.partial(_fox_kernel, t=t),
        grid=(b, groups, s // t),
        in_specs=[
            pl.BlockSpec((None, pairs * LANES, t), lambda i, h, j: (i, h, j)),
            pl.BlockSpec((None, s, pairs * LANES), lambda i, h, j: (i, 0, h)),
            pl.BlockSpec((None, s, pairs * LANES), lambda i, h, j: (i, 0, groups + h)),
            pl.BlockSpec((None, s, LANES), lambda i, h, j: (i, 0, 0)),
        ],
        out_specs=pl.BlockSpec((None, pairs * LANES, t), lambda i, h, j: (i, h, j)),
        out_shape=jax.ShapeDtypeStruct((b, FOX_W, s), BF16),
        scratch_shapes=[
            pltpu.VMEM((2 * pairs, t, t), F32), pltpu.VMEM((2 * pairs, t, t), F32),
            pltpu.VMEM((2 * pairs, t, t), BF16), pltpu.VMEM((2 * pairs, t, t), BF16),
            pltpu.VMEM((2 * pairs, HEAD_DIM, t), F32),
        ],
        compiler_params=_params(("parallel", "parallel", "arbitrary")),
        name="fox_attention",
    )(q_t, proj, proj, gates)


def _even_mixer(h, w_in, w_out, sinks, rel_bias, g, b):
    kv0 = 3 * SB_W + SWA_QW
    dup = lambda w: jnp.repeat(w.reshape(D_MODEL, N_SWA_KV, HEAD_DIM), 2, axis=1).reshape(D_MODEL, -1)
    w_cat = jnp.concatenate(
        [w_in[:, SB_W:kv0], dup(w_in[:, kv0:kv0 + SWA_KVW]), dup(w_in[:, kv0 + SWA_KVW:])], axis=1)
    proj, q_t = _proj(h, w_cat.astype(BF16), w_in[:, :SB_W].T.astype(BF16), -QK_SCALE * LOG2E)
    o_sb = _sb_attention(proj, q_t)
    o_sw = _swa_attention(proj, _swa_bias(rel_bias), sinks.astype(F32))
    return _outproj_ln(o_sb, o_sw, h, w_out.astype(BF16), g, b)


def _odd_mixer(h, w_in, b_f, w_out, g, b):
    spread = lambda v: jnp.pad(jnp.tile(v, (1, GATE_PARTS)), ((0, 0), (0, LANES - GATE_PARTS * N_FOX_HEADS)))
    w_cat = jnp.concatenate([w_in[:, FOX_W:3 * FOX_W], spread(w_in[:, 3 * FOX_W:])], axis=1)
    proj, q_t, f = _proj(h, w_cat.astype(BF16), w_in[:, :FOX_W].T.astype(BF16), QK_SCALE * LOG2E,
                         tail=LANES)
    gates = _fox_gates(f, spread(b_f.reshape(1, -1).astype(F32)))
    o = _fox_attention(proj, q_t, gates)
    return _outproj_ln(o, None, h, w_out.astype(BF16), g, b)


def kernel(x, ln_g, ln_b, ffn1_in, ffn1_out, ffn2_in, ffn2_out, ab_w_in, ab_w_out, ab_sinks,
           fox_w_in, fox_b_f, fox_w_out, rel_bias):
    bsz, s, d = x.shape
    assert (s, d) == (SEQ, D_MODEL) and ln_g.shape[0] == DEPTH
    row = lambda v: v.reshape(1, D_MODEL).astype(F32)
    ffn = lambda h, w_in, w_out, g, b: _ffn_ln(
        h.reshape(bsz * s, d), w_in.astype(BF16), w_out.astype(BF16), row(g), row(b)).reshape(bsz, s, d)
    h = x
    for layer in range(DEPTH):
        g, b = ln_g[layer], ln_b[layer]
        h = ffn(h, ffn1_in[layer], ffn1_out[layer], g[0], b[0])
        i = layer // 2
        if layer % 2 == 0:
            h = _even_mixer(h, ab_w_in[i], ab_w_out[i], ab_sinks[i], rel_bias, row(g[1]), row(b[1]))
        else:
            h = _odd_mixer(h, fox_w_in[i], fox_b_f[i], fox_w_out[i], row(g[1]), row(b[1]))
        h = ffn(h, ffn2_in[layer], ffn2_out[layer], g[2], b[2])
    return h
```

```python
import functools
import math

import numpy as np
import jax
import jax.numpy as jnp
from jax import lax
from jax.experimental import pallas as pl
from jax.experimental.pallas import tpu as pltpu

D_MODEL = 1024
SEQ = 2048
DEPTH = 2
HEAD_DIM = 64
LANES = 128
N_SB_PAIRS = 4
N_SWA_HEADS = 8
N_SWA_KV = 2
N_FOX_HEADS = 16
N_FOX_PAIRS = 8
WINDOW = 128
N_BUCKETS = 32
MAX_DISTANCE = 128
D_FF = 2816
ALPHA = (2 * DEPTH) ** 0.25
LN_EPS = 1e-5
SB_W = 512
SWA_QW = 512
SWA_KVW = 128
FOX_W = 1024
NEG_BIG = -1e30
LOG2E = math.log2(math.e)
QK_SCALE = HEAD_DIM ** -0.5
GATE_PARTS = 3
KEY_BLOCK = 256
QUERY_TILE = 2 * KEY_BLOCK
VMEM_LIMIT = 56 * 1024 * 1024

BF16 = jnp.bfloat16
F32 = jnp.float32


def _dot(a, b):
    return jnp.dot(a, b, preferred_element_type=F32)


def _dot_nt(a, b):
    return lax.dot_general(a, b, (((1,), (1,)), ((), ())), preferred_element_type=F32)


def _dot_tn(a, b):
    return lax.dot_general(a, b, (((0,), (0,)), ((), ())), preferred_element_type=F32)


def _layer_norm(z, g, b):
    mu = jnp.mean(z, axis=-1, keepdims=True)
    d = z - mu
    var = jnp.mean(d * d, axis=-1, keepdims=True)
    return d * lax.rsqrt(var + LN_EPS) * g + b


def _log2_sigmoid(x2):
    sign = jnp.uint32(0x80000000)
    neg_abs = lax.bitcast_convert_type(lax.bitcast_convert_type(x2, jnp.uint32) | sign, F32)
    return jnp.minimum(x2, 0.0) - jnp.log2(1.0 + jnp.exp2(neg_abs))


def _split_bf16(x, parts):
    out = []
    r = x
    for i in range(parts):
        t = r.astype(BF16)
        out.append(t)
        if i + 1 < parts:
            r = r - t.astype(F32)
    return out


def _params(sem):
    return pltpu.CompilerParams(dimension_semantics=sem, vmem_limit_bytes=VMEM_LIMIT)


def _resident(shape, index_map):
    return pl.BlockSpec(shape, index_map, pipeline_mode=pl.Buffered(1))


def _ffn_ln_kernel(x_ref, wg_ref, wu_ref, wo_ref, g_ref, b_ref, o_ref):
    x = x_ref[...]
    xb = x.astype(BF16)
    gate = _dot(xb, wg_ref[...])
    up = _dot(xb, wu_ref[...])
    h = (gate * jax.nn.sigmoid(gate) * up).astype(BF16)
    y = _dot(h, wo_ref[...])
    o_ref[...] = _layer_norm(ALPHA * x + 0.5 * y, g_ref[...], b_ref[...])


def _ffn_ln(x, w_in, w_out, g, b, tm=512):
    n = x.shape[0]
    return pl.pallas_call(
        _ffn_ln_kernel,
        grid=(n // tm,),
        in_specs=[
            pl.BlockSpec((tm, D_MODEL), lambda i: (i, 0)),
            _resident((D_MODEL, D_FF), lambda i: (0, 0)),
            _resident((D_MODEL, D_FF), lambda i: (0, 1)),
            _resident((D_FF, D_MODEL), lambda i: (0, 0)),
            _resident((1, D_MODEL), lambda i: (0, 0)),
            _resident((1, D_MODEL), lambda i: (0, 0)),
        ],
        out_specs=pl.BlockSpec((tm, D_MODEL), lambda i: (i, 0)),
        out_shape=jax.ShapeDtypeStruct((n, D_MODEL), F32),
        compiler_params=_params(("parallel",)),
        name="ffn_ln",
    )(x, w_in, w_in, w_out, g, b)


def _proj_kernel(x_ref, w_ref, wt_ref, o_ref, qt_ref, vt_ref, *tail_ref, q_scale):
    xb = x_ref[...].astype(BF16)
    width = o_ref.shape[-1]
    acc = _dot(xb, w_ref[...])
    o_ref[...] = acc[:, :width].astype(o_ref.dtype)
    for ref in tail_ref:
        ref[...] = acc[:, width:]
    acc_t = _dot_nt(wt_ref[...], xb)
    qw = qt_ref.shape[0]
    qt_ref[...] = (acc_t[:qw] * q_scale).astype(qt_ref.dtype)
    for j in range(vt_ref.shape[0]):
        vt_ref[j] = acc_t[qw:, j * KEY_BLOCK:(j + 1) * KEY_BLOCK].astype(vt_ref.dtype)


def _proj(x, w, wq_t, wv_t, q_scale, tail=0, tm=512):
    b, s, _ = x.shape
    width, qw, vw = w.shape[1] - tail, wq_t.shape[0], wv_t.shape[0]
    tok = lambda n: pl.BlockSpec((None, tm, n), lambda i, j: (i, j, 0))
    out_specs = [tok(width), pl.BlockSpec((None, qw, tm), lambda i, j: (i, 0, j)),
                 pl.BlockSpec((None, tm // KEY_BLOCK, vw, KEY_BLOCK), lambda i, j: (i, j, 0, 0))]
    out_shape = [jax.ShapeDtypeStruct((b, s, width), BF16), jax.ShapeDtypeStruct((b, qw, s), BF16),
                 jax.ShapeDtypeStruct((b, s // KEY_BLOCK, vw, KEY_BLOCK), BF16)]
    if tail:
        out_specs.append(tok(tail))
        out_shape.append(jax.ShapeDtypeStruct((b, s, tail), F32))
    return pl.pallas_call(
        functools.partial(_proj_kernel, q_scale=q_scale),
        grid=(b, s // tm),
        in_specs=[
            tok(D_MODEL),
            _resident((D_MODEL, width + tail), lambda i, j: (0, 0)),
            _resident((qw + vw, D_MODEL), lambda i, j: (0, 0)),
        ],
        out_specs=out_specs,
        out_shape=out_shape,
        compiler_params=_params(("parallel", "parallel")),
        name="in_proj",
    )(x, w, jnp.concatenate([wq_t, wv_t], axis=0))


def _outproj_ln_kernel(ot_ref, h_ref, w_ref, g_ref, b_ref, out_ref):
    mix = _dot_tn(ot_ref[...], w_ref[...])
    out_ref[...] = _layer_norm(ALPHA * h_ref[...] + mix, g_ref[...], b_ref[...])


def _outproj2_ln_kernel(ot_ref, o_ref, h_ref, wt_ref, w_ref, g_ref, b_ref, out_ref):
    mix = _dot_tn(ot_ref[...], wt_ref[...]) + _dot(o_ref[...], w_ref[...])
    out_ref[...] = _layer_norm(ALPHA * h_ref[...] + mix, g_ref[...], b_ref[...])


def _outproj_ln(o_t, o, h, w, g, b, tm=512):
    bsz, s, _ = h.shape
    wt = o_t.shape[1]
    tok = lambda width: pl.BlockSpec((None, tm, width), lambda i, j: (i, j, 0))
    row = _resident((1, D_MODEL), lambda i, j: (0, 0))
    ot_spec = pl.BlockSpec((None, wt, tm), lambda i, j: (i, 0, j))
    if o is None:
        body = _outproj_ln_kernel
        in_specs = [ot_spec, tok(D_MODEL), _resident((wt, D_MODEL), lambda i, j: (0, 0)), row, row]
        args = (o_t, h, w, g, b)
    else:
        wn = o.shape[2]
        body = _outproj2_ln_kernel
        in_specs = [ot_spec, tok(wn), tok(D_MODEL), _resident((wt, D_MODEL), lambda i, j: (0, 0)),
                    _resident((wn, D_MODEL), lambda i, j: (wt // wn, 0)), row, row]
        args = (o_t, o, h, w, w, g, b)
    return pl.pallas_call(
        body,
        grid=(bsz, s // tm),
        in_specs=in_specs,
        out_specs=tok(D_MODEL),
        out_shape=jax.ShapeDtypeStruct((bsz, s, D_MODEL), F32),
        compiler_params=_params(("parallel", "parallel")),
        name="out_proj_ln",
    )(*args)


def _head_pair(q):
    lo = lax.broadcasted_iota(jnp.int32, q.shape, 1) < HEAD_DIM
    zero = jnp.zeros_like(q)
    return jnp.where(lo, q, zero), jnp.where(lo, zero, q)


def _merge_pair(a0, a1):
    lo = lax.broadcasted_iota(jnp.int32, a0.shape, 1) < HEAD_DIM
    return jnp.where(lo, a0, a1)


def _head_operands_t(q_t):
    top = lax.broadcasted_iota(jnp.int32, (LANES, q_t.shape[1]), 0) < HEAD_DIM
    out = []
    for p in range(q_t.shape[0] // LANES):
        tile = q_t[p * LANES:(p + 1) * LANES]
        zero = jnp.zeros_like(tile)
        out += [jnp.where(top, tile, zero), jnp.where(top, zero, tile)]
    return out


def _key_rows(k_ref, kb, h):
    start = pl.multiple_of(kb * KEY_BLOCK, KEY_BLOCK)
    return k_ref[pl.ds(start, KEY_BLOCK), (h // 2) * LANES:(h // 2 + 1) * LANES]


def _value_rows_t(vt_ref, kb, h):
    return vt_ref[kb, h * HEAD_DIM:(h + 1) * HEAD_DIM, :]


def _diagonal_masks(strict):
    row = lax.broadcasted_iota(jnp.int32, (KEY_BLOCK, QUERY_TILE), 0)
    col = lax.broadcasted_iota(jnp.int32, (KEY_BLOCK, QUERY_TILE), 1)
    if strict:
        return row < col, row + KEY_BLOCK < col
    return row <= col, row + KEY_BLOCK <= col


def _attention_scratch(pairs):
    tile = (2 * pairs, KEY_BLOCK, QUERY_TILE)
    return [pltpu.VMEM(tile, F32), pltpu.VMEM(tile, F32),
            pltpu.VMEM(tile, F32), pltpu.VMEM(tile, F32),
            pltpu.VMEM((2 * pairs, HEAD_DIM, QUERY_TILE), F32)]


def _sb_kernel(qt_ref, k_ref, vt_ref, o_ref, nza_ref, nzb_ref, wa_ref, wb_ref, acc_ref):
    qi = pl.program_id(2)
    w_heads = _head_operands_t(qt_ref[...])
    heads = range(len(w_heads))
    masks = _diagonal_masks(strict=True)
    row = lax.broadcasted_iota(jnp.int32, (KEY_BLOCK, KEY_BLOCK), 0)
    col = lax.broadcasted_iota(jnp.int32, (KEY_BLOCK, KEY_BLOCK), 1)
    suffix = jnp.where(col >= row, 1.0, 0.0).astype(BF16)
    suffix2 = jnp.concatenate([suffix, suffix], axis=1)
    hi_mask = jnp.uint32(0xFFFF0000)
    last = 2 * qi + 1

    def scores(kb, nz_ref):
        for h in heads:
            nz_ref[h] = _dot(_key_rows(k_ref, kb, h), w_heads[h])

    def accumulate(kb, w_ref):
        for h in heads:
            acc_ref[h] += _dot(_value_rows_t(vt_ref, kb, h), w_ref[h].astype(BF16))

    def weights(nz_ref, w_ref, carries, mask):
        incl = []
        for h in heads:
            log_keep = _log2_sigmoid(nz_ref[h])
            if mask is not None:
                log_keep = jnp.where(mask, log_keep, 0.0)
            hi = lax.bitcast_convert_type(lax.bitcast_convert_type(log_keep, jnp.uint32) & hi_mask, F32)
            parts = jnp.concatenate([hi.astype(BF16), (log_keep - hi).astype(BF16)], axis=0)
            incl.append(_dot(suffix2, parts))
        for h in heads:
            w = jnp.exp2(incl[h] + carries[h] - nz_ref[h])
            if mask is not None:
                w = jnp.where(mask, w, 0.0)
            w_ref[h] = w
        return [carries[h] + incl[h][0:1, :] for h in heads]

    def half_step(j, cur, oth, carries, *, mask=None, first=False):
        kb = last - j
        scores(jnp.maximum(kb - 1, 0), oth[0])
        if not first:
            accumulate(kb + 1, oth[1])
        return weights(cur[0], cur[1], carries, mask)

    slot_a, slot_b = (nza_ref, wa_ref), (nzb_ref, wb_ref)
    acc_ref[...] = jnp.zeros_like(acc_ref)
    scores(last, nza_ref)
    carries = [jnp.zeros((1, QUERY_TILE), F32) for _ in heads]
    carries = half_step(0, slot_a, slot_b, carries, mask=masks[1], first=True)
    carries = half_step(1, slot_b, slot_a, carries, mask=masks[0])

    def body(i, carries):
        carries = half_step(2 * i + 2, slot_a, slot_b, carries)
        return half_step(2 * i + 3, slot_b, slot_a, carries)

    lax.fori_loop(0, qi, body, carries)
    accumulate(0, wb_ref)
    for h in heads:
        o_ref[h * HEAD_DIM:(h + 1) * HEAD_DIM, :] = acc_ref[h].astype(o_ref.dtype)


def _sb_attention(proj, q_t, v_t, pairs=2):
    b, s, _ = proj.shape
    groups = N_SB_PAIRS // pairs
    return pl.pallas_call(
        _sb_kernel,
        grid=(b, groups, s // QUERY_TILE),
        in_specs=[
            pl.BlockSpec((None, pairs * LANES, QUERY_TILE), lambda i, h, j: (i, h, j)),
            pl.BlockSpec((None, s, pairs * LANES), lambda i, h, j: (i, 0, h)),
            pl.BlockSpec((None, s // KEY_BLOCK, pairs * LANES, KEY_BLOCK), lambda i, h, j: (i, 0, h, 0)),
        ],
        out_specs=pl.BlockSpec((None, pairs * LANES, QUERY_TILE), lambda i, h, j: (i, h, j)),
        out_shape=jax.ShapeDtypeStruct((b, SB_W, s), BF16),
        scratch_shapes=_attention_scratch(pairs),
        compiler_params=_params(("parallel", "parallel", "arbitrary")),
        name="sb_attention",
    )(q_t, proj, v_t)


def _t5_bucket_table():
    n = np.arange(WINDOW)
    max_exact = N_BUCKETS // 2
    nf = np.maximum(n, 1).astype(np.float64)
    val = np.log(nf / max_exact) / math.log(MAX_DISTANCE / max_exact) * (N_BUCKETS - max_exact)
    frac = np.abs(val - np.round(val))[n > max_exact]
    assert frac.min() > 1e-3, "bucket boundary too close to an integer distance for float32"
    large = np.minimum(max_exact + val.astype(np.int64), N_BUCKETS - 1)
    return np.where(n < max_exact, n, large)


def _swa_bucket_tiles():
    table = _t5_bucket_table()
    qi = np.arange(WINDOW)[:, None]
    kj = np.arange(2 * WINDOW)[None, :]
    tiles = []
    for offset in (WINDOW, 0):
        rel = qi + offset - kj
        valid = (rel >= 0) & (rel < WINDOW)
        tiles.append(np.where(valid, table[np.clip(rel, 0, WINDOW - 1)], -1))
    return np.stack(tiles).astype(np.int32)


def _swa_bias_kernel(rb_ref, idx_ref, o_ref):
    h = pl.program_id(0)
    for tile in range(2):
        idx = idx_ref[tile]
        acc = jnp.full(idx.shape, NEG_BIG, F32)
        for bucket in range(N_BUCKETS):
            acc = jnp.where(idx == bucket, rb_ref[bucket, h], acc)
        o_ref[tile, 0] = acc


def _swa_bias(rel_bias):
    idx = jnp.asarray(_swa_bucket_tiles())
    return pl.pallas_call(
        _swa_bias_kernel,
        grid=(N_SWA_HEADS,),
        in_specs=[
            pl.BlockSpec(memory_space=pltpu.SMEM),
            pl.BlockSpec((2, WINDOW, 2 * WINDOW), lambda h: (0, 0, 0)),
        ],
        out_specs=pl.BlockSpec((2, 1, WINDOW, 2 * WINDOW), lambda h: (0, h, 0, 0)),
        out_shape=jax.ShapeDtypeStruct((2, N_SWA_HEADS, WINDOW, 2 * WINDOW), F32),
        compiler_params=_params(("arbitrary",)),
        name="swa_bias",
    )(rel_bias, idx)


def _swa_kernel(sink_ref, q_ref, k_ref, v_ref, bias_ref, o_ref, *, tq):
    qt = pl.program_id(1)
    scale = jnp.asarray(QK_SCALE, BF16)
    group = N_SWA_HEADS // N_SWA_KV
    for j in range(tq // WINDOW):
        nb = qt * (tq // WINDOW) + j
        first = (nb == 0).astype(jnp.int32) if j == 0 else 0
        w0 = pl.multiple_of(jnp.maximum(nb - 1, 0) * WINDOW, WINDOW)
        rows = slice(j * WINDOW, (j + 1) * WINDOW)
        for g in range(N_SWA_KV):
            kwin = k_ref[pl.ds(w0, 2 * WINDOW), g * LANES:(g + 1) * LANES]
            vwin = v_ref[pl.ds(w0, 2 * WINDOW), g * LANES:(g + 1) * LANES]
            for c in range(group // 2):
                pair = g * (group // 2) + c
                cols = slice(pair * LANES, (pair + 1) * LANES)
                q_heads = _head_pair(q_ref[rows, cols] * scale)
                outs = []
                for i in range(2):
                    head = 2 * pair + i
                    sink = sink_ref[head]
                    logits = _dot_nt(q_heads[i], kwin) + bias_ref[first, head]
                    m = jnp.maximum(jnp.max(logits, axis=-1, keepdims=True), sink)
                    p = jnp.exp(logits - m)
                    denom = jnp.sum(p, axis=-1, keepdims=True) + jnp.exp(sink - m)
                    outs.append(_dot(p.astype(BF16), vwin) / denom)
                o_ref[rows, cols] = _merge_pair(outs[0], outs[1]).astype(o_ref.dtype)


def _swa_attention(proj, bias, sinks, tq=512):
    b, s, width = proj.shape
    q_blk = (width - SWA_QW - 4 * LANES) // SWA_QW
    k_blk = (width - 4 * LANES) // (2 * LANES)
    return pl.pallas_call(
        functools.partial(_swa_kernel, tq=tq),
        grid=(b, s // tq),
        in_specs=[
            pl.BlockSpec(memory_space=pltpu.SMEM),
            pl.BlockSpec((None, tq, SWA_QW), lambda i, j: (i, j, q_blk)),
            pl.BlockSpec((None, s, 2 * LANES), lambda i, j: (i, 0, k_blk)),
            pl.BlockSpec((None, s, 2 * LANES), lambda i, j: (i, 0, k_blk + 1)),
            _resident((2, N_SWA_HEADS, WINDOW, 2 * WINDOW), lambda i, j: (0, 0, 0, 0)),
        ],
        out_specs=pl.BlockSpec((None, tq, SWA_QW), lambda i, j: (i, j, 0)),
        out_shape=jax.ShapeDtypeStruct((b, s, SWA_QW), BF16),
        compiler_params=_params(("parallel", "arbitrary")),
        name="swa_attention",
    )(sinks, proj, proj, proj, bias)


def _fox_gate_kernel(f_ref, bf_ref, c_ref, *, chunk):
    s = f_ref.shape[0]
    r = lax.broadcasted_iota(jnp.int32, (chunk, chunk), 0)
    c = lax.broadcasted_iota(jnp.int32, (chunk, chunk), 1)
    upto = jnp.where(c <= r, 1.0, 0.0).astype(BF16)
    lane = lax.broadcasted_iota(jnp.int32, (chunk, LANES), 1)
    carry = jnp.zeros((1, LANES), F32)
    for j in range(s // chunk):
        rows = slice(j * chunk, (j + 1) * chunk)
        log_f = _log2_sigmoid((f_ref[rows, :] + bf_ref[...]) * LOG2E)
        cs = carry
        for part in _split_bf16(log_f, 3):
            cs = cs + _dot(upto, part)
        carry = cs[chunk - 1:chunk, :]
        out = jnp.zeros((chunk, LANES), BF16)
        for k, part in reversed(list(enumerate(_split_bf16(cs, GATE_PARTS)))):
            out = jnp.where(lane < (k + 1) * N_FOX_HEADS, part, out)
        c_ref[rows, :] = out


def _fox_gates(f, b_f, chunk=256):
    b, s, _ = f.shape
    return pl.pallas_call(
        functools.partial(_fox_gate_kernel, chunk=chunk),
        grid=(b,),
        in_specs=[
            pl.BlockSpec((None, s, LANES), lambda i: (i, 0, 0)),
            _resident((1, LANES), lambda i: (0, 0)),
        ],
        out_specs=pl.BlockSpec((None, s, LANES), lambda i: (i, 0, 0)),
        out_shape=jax.ShapeDtypeStruct((b, s, LANES), BF16),
        compiler_params=_params(("parallel",)),
        name="fox_gates",
    )(f, b_f)


def _fox_kernel(qt_ref, k_ref, vt_ref, c_ref, o_ref, sa_ref, sb_ref, pa_ref, pb_ref, acc_ref):
    qi = pl.program_id(2)
    q_heads = _head_operands_t(qt_ref[...])
    heads = range(len(q_heads))
    first_head = pl.program_id(1) * len(q_heads)
    gate_row = lax.broadcasted_iota(jnp.int32, (LANES, QUERY_TILE), 0)
    masks = _diagonal_masks(strict=False)
    w_heads = []
    for h in heads:
        picks = ((gate_row & (N_FOX_HEADS - 1)) == first_head + h) & (gate_row < GATE_PARTS * N_FOX_HEADS)
        minus_gate = jnp.where(picks, -1.0, 0.0).astype(BF16)
        w_heads.append(jnp.concatenate([q_heads[h], minus_gate], axis=0))

    def scores(kb, s_ref):
        start = pl.multiple_of(kb * KEY_BLOCK, KEY_BLOCK)
        gates = c_ref[pl.ds(start, KEY_BLOCK), :]
        for h in heads:
            s_ref[h] = _dot(jnp.concatenate([_key_rows(k_ref, kb, h), gates], axis=1), w_heads[h])

    def accumulate(kb, p_ref, a):
        for h in heads:
            pv = _dot(_value_rows_t(vt_ref, kb, h), p_ref[h].astype(BF16))
            acc_ref[h] = a[h] * acc_ref[h] + pv

    def softmax(s_ref, p_ref, stats, mask):
        new_stats, a = [], []
        for h in heads:
            m, l = stats[h]
            s2 = s_ref[h] if mask is None else jnp.where(mask, s_ref[h], NEG_BIG)
            m_new = jnp.maximum(m, jnp.max(s2, axis=0, keepdims=True))
            ah = jnp.exp2(m - m_new)
            p = jnp.exp2(s2 - m_new)
            p_ref[h] = p
            new_stats.append((m_new, ah * l + jnp.sum(p, axis=0, keepdims=True)))
            a.append(ah)
        return new_stats, a

    def half_step(kb, cur, oth, stats, a_prev, *, prefetch=True, mask=None):
        if prefetch:
            scores(kb + 1, oth[0])
        accumulate(jnp.maximum(kb - 1, 0), oth[1], a_prev)
        return softmax(cur[0], cur[1], stats, mask)

    slot_a, slot_b = (sa_ref, pa_ref), (sb_ref, pb_ref)
    pb_ref[...] = jnp.zeros_like(pb_ref)
    acc_ref[...] = jnp.zeros_like(acc_ref)
    scores(0, sa_ref)
    stats = [(jnp.full((1, QUERY_TILE), NEG_BIG, F32), jnp.zeros((1, QUERY_TILE), F32)) for _ in heads]
    ones = [jnp.ones((1, QUERY_TILE), F32) for _ in heads]

    def body(i, carry):
        stats, a = carry
        stats, a = half_step(2 * i, slot_a, slot_b, stats, a)
        return half_step(2 * i + 1, slot_b, slot_a, stats, a)

    stats, a = lax.fori_loop(0, qi, body, (stats, ones))
    stats, a = half_step(2 * qi, slot_a, slot_b, stats, a, mask=masks[0])
    stats, a = half_step(2 * qi + 1, slot_b, slot_a, stats, a, prefetch=False, mask=masks[1])
    accumulate(2 * qi + 1, pb_ref, a)
    for h in heads:
        o_ref[h * HEAD_DIM:(h + 1) * HEAD_DIM, :] = (acc_ref[h] / stats[h][1]).astype(o_ref.dtype)


def _fox_attention(proj, q_t, v_t, gates, pairs=2):
    b, s, _ = proj.shape
    groups = N_FOX_PAIRS // pairs
    return pl.pallas_call(
        _fox_kernel,
        grid=(b, groups, s // QUERY_TILE),
        in_specs=[
            pl.BlockSpec((None, pairs * LANES, QUERY_TILE), lambda i, h, j: (i, h, j)),
            pl.BlockSpec((None, s, pairs * LANES), lambda i, h, j: (i, 0, h)),
            pl.BlockSpec((None, s // KEY_BLOCK, pairs * LANES, KEY_BLOCK), lambda i, h, j: (i, 0, h, 0)),
            pl.BlockSpec((None, s, LANES), lambda i, h, j: (i, 0, 0)),
        ],
        out_specs=pl.BlockSpec((None, pairs * LANES, QUERY_TILE), lambda i, h, j: (i, h, j)),
        out_shape=jax.ShapeDtypeStruct((b, FOX_W, s), BF16),
        scratch_shapes=_attention_scratch(pairs),
        compiler_params=_params(("parallel", "parallel", "arbitrary")),
        name="fox_attention",
    )(q_t, proj, v_t, gates)


def _even_mixer(h, w_in, w_out, sinks, rel_bias, g, b):
    kv0 = 3 * SB_W + SWA_QW
    dup = lambda w: jnp.repeat(w.reshape(D_MODEL, N_SWA_KV, HEAD_DIM), 2, axis=1).reshape(D_MODEL, -1)
    w_cat = jnp.concatenate(
        [w_in[:, SB_W:2 * SB_W], w_in[:, 3 * SB_W:kv0],
         dup(w_in[:, kv0:kv0 + SWA_KVW]), dup(w_in[:, kv0 + SWA_KVW:])], axis=1)
    proj, q_t, v_t = _proj(h, w_cat.astype(BF16), w_in[:, :SB_W].T.astype(BF16),
                           w_in[:, 2 * SB_W:3 * SB_W].T.astype(BF16), -QK_SCALE * LOG2E)
    o_sb = _sb_attention(proj, q_t, v_t)
    o_sw = _swa_attention(proj, _swa_bias(rel_bias), sinks.astype(F32))
    return _outproj_ln(o_sb, o_sw, h, w_out.astype(BF16), g, b)


def _odd_mixer(h, w_in, b_f, w_out, g, b):
    spread = lambda v: jnp.pad(jnp.tile(v, (1, GATE_PARTS)), ((0, 0), (0, LANES - GATE_PARTS * N_FOX_HEADS)))
    w_cat = jnp.concatenate([w_in[:, FOX_W:2 * FOX_W], spread(w_in[:, 3 * FOX_W:])], axis=1)
    proj, q_t, v_t, f = _proj(h, w_cat.astype(BF16), w_in[:, :FOX_W].T.astype(BF16),
                              w_in[:, 2 * FOX_W:3 * FOX_W].T.astype(BF16), QK_SCALE * LOG2E, tail=LANES)
    gates = _fox_gates(f, spread(b_f.reshape(1, -1).astype(F32)))
    o = _fox_attention(proj, q_t, v_t, gates)
    return _outproj_ln(o, None, h, w_out.astype(BF16), g, b)


def kernel(x, ln_g, ln_b, ffn1_in, ffn1_out, ffn2_in, ffn2_out, ab_w_in, ab_w_out, ab_sinks,
           fox_w_in, fox_b_f, fox_w_out, rel_bias):
    bsz, s, d = x.shape
    assert (s, d) == (SEQ, D_MODEL) and ln_g.shape[0] == DEPTH
    row = lambda v: v.reshape(1, D_MODEL).astype(F32)
    ffn = lambda h, w_in, w_out, g, b: _ffn_ln(
        h.reshape(bsz * s, d), w_in.astype(BF16), w_out.astype(BF16), row(g), row(b)).reshape(bsz, s, d)
    h = x
    for layer in range(DEPTH):
        g, b = ln_g[layer], ln_b[layer]
        h = ffn(h, ffn1_in[layer], ffn1_out[layer], g[0], b[0])
        i = layer // 2
        if layer % 2 == 0:
            h = _even_mixer(h, ab_w_in[i], ab_w_out[i], ab_sinks[i], rel_bias, row(g[1]), row(b[1]))
        else:
            h = _odd_mixer(h, fox_w_in[i], fox_b_f[i], fox_w_out[i], row(g[1]), row(b[1]))
        h = ffn(h, ffn2_in[layer], ffn2_out[layer], g[2], b[2])
    return h
```

```python
import functools
import math

import numpy as np
import jax
import jax.numpy as jnp
from jax import lax
from jax.experimental import pallas as pl
from jax.experimental.pallas import tpu as pltpu

D_MODEL = 1024
SEQ = 2048
DEPTH = 2
HEAD_DIM = 64
LANES = 128
N_SB_PAIRS = 4
N_SWA_HEADS = 8
N_SWA_KV = 2
N_FOX_HEADS = 16
N_FOX_PAIRS = 8
WINDOW = 128
N_BUCKETS = 32
MAX_DISTANCE = 128
D_FF = 2816
ALPHA = (2 * DEPTH) ** 0.25
LN_EPS = 1e-5
SB_W = 512
SWA_QW = 512
SWA_KVW = 128
FOX_W = 1024
NEG_BIG = -1e30
LOG2E = math.log2(math.e)
QK_SCALE = HEAD_DIM ** -0.5
GATE_PARTS = 3
KEY_BLOCK = 256
QUERY_TILE = 2 * KEY_BLOCK
VMEM_LIMIT = 56 * 1024 * 1024

BF16 = jnp.bfloat16
F32 = jnp.float32


def _dot(a, b):
    return jnp.dot(a, b, preferred_element_type=F32)


def _dot_nt(a, b):
    return lax.dot_general(a, b, (((1,), (1,)), ((), ())), preferred_element_type=F32)


def _dot_tn(a, b):
    return lax.dot_general(a, b, (((0,), (0,)), ((), ())), preferred_element_type=F32)


def _layer_norm(z, g, b):
    mu = jnp.mean(z, axis=-1, keepdims=True)
    d = z - mu
    var = jnp.mean(d * d, axis=-1, keepdims=True)
    return d * lax.rsqrt(var + LN_EPS) * g + b


def _log2_sigmoid(x2):
    sign = jnp.uint32(0x80000000)
    neg_abs = lax.bitcast_convert_type(lax.bitcast_convert_type(x2, jnp.uint32) | sign, F32)
    return jnp.minimum(x2, 0.0) - jnp.log2(1.0 + jnp.exp2(neg_abs))


def _split_bf16(x, parts):
    out = []
    r = x
    for i in range(parts):
        t = r.astype(BF16)
        out.append(t)
        if i + 1 < parts:
            r = r - t.astype(F32)
    return out


def _params(sem):
    return pltpu.CompilerParams(dimension_semantics=sem, vmem_limit_bytes=VMEM_LIMIT)


def _resident(shape, index_map):
    return pl.BlockSpec(shape, index_map, pipeline_mode=pl.Buffered(1))


def _ffn_ln_kernel(x_ref, wg_ref, wu_ref, wo_ref, g_ref, b_ref, o_ref):
    x = x_ref[...]
    xb = x.astype(BF16)
    gate = _dot(xb, wg_ref[...])
    up = _dot(xb, wu_ref[...])
    h = (gate * jax.nn.sigmoid(gate) * up).astype(BF16)
    y = _dot(h, wo_ref[...])
    o_ref[...] = _layer_norm(ALPHA * x + 0.5 * y, g_ref[...], b_ref[...])


def _ffn_ln(x, w_in, w_out, g, b, tm=512):
    n = x.shape[0]
    return pl.pallas_call(
        _ffn_ln_kernel,
        grid=(n // tm,),
        in_specs=[
            pl.BlockSpec((tm, D_MODEL), lambda i: (i, 0)),
            _resident((D_MODEL, D_FF), lambda i: (0, 0)),
            _resident((D_MODEL, D_FF), lambda i: (0, 1)),
            _resident((D_FF, D_MODEL), lambda i: (0, 0)),
            _resident((1, D_MODEL), lambda i: (0, 0)),
            _resident((1, D_MODEL), lambda i: (0, 0)),
        ],
        out_specs=pl.BlockSpec((tm, D_MODEL), lambda i: (i, 0)),
        out_shape=jax.ShapeDtypeStruct((n, D_MODEL), F32),
        compiler_params=_params(("parallel",)),
        name="ffn_ln",
    )(x, w_in, w_in, w_out, g, b)


def _proj_kernel(x_ref, w_ref, wt_ref, o_ref, qt_ref, vt_ref, *tail_ref, q_scale):
    xb = x_ref[...].astype(BF16)
    width = o_ref.shape[-1]
    acc = _dot(xb, w_ref[...])
    o_ref[...] = acc[:, :width].astype(o_ref.dtype)
    for ref in tail_ref:
        ref[...] = acc[:, width:]
    acc_t = _dot_nt(wt_ref[...], xb)
    qw = qt_ref.shape[0]
    qt_ref[...] = (acc_t[:qw] * q_scale).astype(qt_ref.dtype)
    for j in range(vt_ref.shape[0]):
        vt_ref[j] = acc_t[qw:, j * KEY_BLOCK:(j + 1) * KEY_BLOCK].astype(vt_ref.dtype)


def _proj(x, w, wq_t, wv_t, q_scale, tail=0, tm=512):
    b, s, _ = x.shape
    width, qw, vw = w.shape[1] - tail, wq_t.shape[0], wv_t.shape[0]
    tok = lambda n: pl.BlockSpec((None, tm, n), lambda i, j: (i, j, 0))
    out_specs = [tok(width), pl.BlockSpec((None, qw, tm), lambda i, j: (i, 0, j)),
                 pl.BlockSpec((None, tm // KEY_BLOCK, vw, KEY_BLOCK), lambda i, j: (i, j, 0, 0))]
    out_shape = [jax.ShapeDtypeStruct((b, s, width), BF16), jax.ShapeDtypeStruct((b, qw, s), BF16),
                 jax.ShapeDtypeStruct((b, s // KEY_BLOCK, vw, KEY_BLOCK), BF16)]
    if tail:
        out_specs.append(tok(tail))
        out_shape.append(jax.ShapeDtypeStruct((b, s, tail), F32))
    return pl.pallas_call(
        functools.partial(_proj_kernel, q_scale=q_scale),
        grid=(b, s // tm),
        in_specs=[
            tok(D_MODEL),
            _resident((D_MODEL, width + tail), lambda i, j: (0, 0)),
            _resident((qw + vw, D_MODEL), lambda i, j: (0, 0)),
        ],
        out_specs=out_specs,
        out_shape=out_shape,
        compiler_params=_params(("parallel", "parallel")),
        name="in_proj",
    )(x, w, jnp.concatenate([wq_t, wv_t], axis=0))


def _outproj_ln_kernel(ot_ref, h_ref, w_ref, g_ref, b_ref, out_ref):
    mix = _dot_tn(ot_ref[...], w_ref[...])
    out_ref[...] = _layer_norm(ALPHA * h_ref[...] + mix, g_ref[...], b_ref[...])


def _outproj2_ln_kernel(ot_ref, o_ref, h_ref, wt_ref, w_ref, g_ref, b_ref, out_ref):
    mix = _dot_tn(ot_ref[...], wt_ref[...]) + _dot(o_ref[...], w_ref[...])
    out_ref[...] = _layer_norm(ALPHA * h_ref[...] + mix, g_ref[...], b_ref[...])


def _outproj_ln(o_t, o, h, w, g, b, tm=512):
    bsz, s, _ = h.shape
    wt = o_t.shape[1]
    tok = lambda width: pl.BlockSpec((None, tm, width), lambda i, j: (i, j, 0))
    row = _resident((1, D_MODEL), lambda i, j: (0, 0))
    ot_spec = pl.BlockSpec((None, wt, tm), lambda i, j: (i, 0, j))
    if o is None:
        body = _outproj_ln_kernel
        in_specs = [ot_spec, tok(D_MODEL), _resident((wt, D_MODEL), lambda i, j: (0, 0)), row, row]
        args = (o_t, h, w, g, b)
    else:
        wn = o.shape[2]
        body = _outproj2_ln_kernel
        in_specs = [ot_spec, tok(wn), tok(D_MODEL), _resident((wt, D_MODEL), lambda i, j: (0, 0)),
                    _resident((wn, D_MODEL), lambda i, j: (wt // wn, 0)), row, row]
        args = (o_t, o, h, w, w, g, b)
    return pl.pallas_call(
        body,
        grid=(bsz, s // tm),
        in_specs=in_specs,
        out_specs=tok(D_MODEL),
        out_shape=jax.ShapeDtypeStruct((bsz, s, D_MODEL), F32),
        compiler_params=_params(("parallel", "parallel")),
        name="out_proj_ln",
    )(*args)


def _head_pair(q):
    lo = lax.broadcasted_iota(jnp.int32, q.shape, 1) < HEAD_DIM
    zero = jnp.zeros_like(q)
    return jnp.where(lo, q, zero), jnp.where(lo, zero, q)


def _merge_pair(a0, a1):
    lo = lax.broadcasted_iota(jnp.int32, a0.shape, 1) < HEAD_DIM
    return jnp.where(lo, a0, a1)


def _head_operands_t(q_t):
    top = lax.broadcasted_iota(jnp.int32, (LANES, q_t.shape[1]), 0) < HEAD_DIM
    out = []
    for p in range(q_t.shape[0] // LANES):
        tile = q_t[p * LANES:(p + 1) * LANES]
        zero = jnp.zeros_like(tile)
        out += [jnp.where(top, tile, zero), jnp.where(top, zero, tile)]
    return out


def _key_rows(k_ref, kb, h):
    start = pl.multiple_of(kb * KEY_BLOCK, KEY_BLOCK)
    return k_ref[pl.ds(start, KEY_BLOCK), (h // 2) * LANES:(h // 2 + 1) * LANES]


def _value_rows_t(vt_ref, kb, h):
    return vt_ref[kb, h * HEAD_DIM:(h + 1) * HEAD_DIM, :]


def _diagonal_masks(strict):
    row = lax.broadcasted_iota(jnp.int32, (KEY_BLOCK, QUERY_TILE), 0)
    col = lax.broadcasted_iota(jnp.int32, (KEY_BLOCK, QUERY_TILE), 1)
    if strict:
        return row < col, row + KEY_BLOCK < col
    return row <= col, row + KEY_BLOCK <= col


def _attention_scratch(pairs):
    tile = (2 * pairs, KEY_BLOCK, QUERY_TILE)
    return [pltpu.VMEM(tile, F32), pltpu.VMEM(tile, F32),
            pltpu.VMEM(tile, F32), pltpu.VMEM(tile, F32),
            pltpu.VMEM((2 * pairs, HEAD_DIM, QUERY_TILE), F32)]


def _sb_kernel(qt_ref, k_ref, vt_ref, o_ref, nza_ref, nzb_ref, wa_ref, wb_ref, acc_ref):
    qi = pl.program_id(2)
    w_heads = _head_operands_t(qt_ref[...])
    heads = range(len(w_heads))
    masks = _diagonal_masks(strict=True)
    row = lax.broadcasted_iota(jnp.int32, (KEY_BLOCK, KEY_BLOCK), 0)
    col = lax.broadcasted_iota(jnp.int32, (KEY_BLOCK, KEY_BLOCK), 1)
    suffix = jnp.where(col >= row, 1.0, 0.0).astype(BF16)
    last = 2 * qi + 1

    def scores(kb, nz_ref):
        for h in heads:
            nz_ref[h] = _dot(_key_rows(k_ref, kb, h), w_heads[h])

    def accumulate(kb, w_ref):
        for h in heads:
            acc_ref[h] += _dot(_value_rows_t(vt_ref, kb, h), w_ref[h].astype(BF16))

    def weights(nz_ref, w_ref, carries, mask):
        incl = []
        for h in heads:
            log_keep = _log2_sigmoid(nz_ref[h])
            if mask is not None:
                log_keep = jnp.where(mask, log_keep, 0.0)
            incl.append(_dot(suffix, log_keep.astype(BF16)))
        for h in heads:
            w = jnp.exp2(incl[h] + carries[h] - nz_ref[h])
            if mask is not None:
                w = jnp.where(mask, w, 0.0)
            w_ref[h] = w
        return [carries[h] + incl[h][0:1, :] for h in heads]

    def half_step(j, cur, oth, carries, *, mask=None, first=False):
        kb = last - j
        scores(jnp.maximum(kb - 1, 0), oth[0])
        if not first:
            accumulate(kb + 1, oth[1])
        return weights(cur[0], cur[1], carries, mask)

    slot_a, slot_b = (nza_ref, wa_ref), (nzb_ref, wb_ref)
    acc_ref[...] = jnp.zeros_like(acc_ref)
    scores(last, nza_ref)
    carries = [jnp.zeros((1, QUERY_TILE), F32) for _ in heads]
    carries = half_step(0, slot_a, slot_b, carries, mask=masks[1], first=True)
    carries = half_step(1, slot_b, slot_a, carries, mask=masks[0])

    def body(i, carries):
        carries = half_step(2 * i + 2, slot_a, slot_b, carries)
        return half_step(2 * i + 3, slot_b, slot_a, carries)

    lax.fori_loop(0, qi, body, carries)
    accumulate(0, wb_ref)
    for h in heads:
        o_ref[h * HEAD_DIM:(h + 1) * HEAD_DIM, :] = acc_ref[h].astype(o_ref.dtype)


def _sb_attention(proj, q_t, v_t, pairs=2):
    b, s, _ = proj.shape
    groups = N_SB_PAIRS // pairs
    return pl.pallas_call(
        _sb_kernel,
        grid=(b, groups, s // QUERY_TILE),
        in_specs=[
            pl.BlockSpec((None, pairs * LANES, QUERY_TILE), lambda i, h, j: (i, h, j)),
            pl.BlockSpec((None, s, pairs * LANES), lambda i, h, j: (i, 0, h)),
            pl.BlockSpec((None, s // KEY_BLOCK, pairs * LANES, KEY_BLOCK), lambda i, h, j: (i, 0, h, 0)),
        ],
        out_specs=pl.BlockSpec((None, pairs * LANES, QUERY_TILE), lambda i, h, j: (i, h, j)),
        out_shape=jax.ShapeDtypeStruct((b, SB_W, s), BF16),
        scratch_shapes=_attention_scratch(pairs),
        compiler_params=_params(("parallel", "parallel", "arbitrary")),
        name="sb_attention",
    )(q_t, proj, v_t)


def _t5_bucket_table():
    n = np.arange(WINDOW)
    max_exact = N_BUCKETS // 2
    nf = np.maximum(n, 1).astype(np.float64)
    val = np.log(nf / max_exact) / math.log(MAX_DISTANCE / max_exact) * (N_BUCKETS - max_exact)
    frac = np.abs(val - np.round(val))[n > max_exact]
    assert frac.min() > 1e-3, "bucket boundary too close to an integer distance for float32"
    large = np.minimum(max_exact + val.astype(np.int64), N_BUCKETS - 1)
    return np.where(n < max_exact, n, large)


def _swa_bucket_tiles():
    table = _t5_bucket_table()
    qi = np.arange(WINDOW)[:, None]
    kj = np.arange(2 * WINDOW)[None, :]
    tiles = []
    for offset in (WINDOW, 0):
        rel = qi + offset - kj
        valid = (rel >= 0) & (rel < WINDOW)
        tiles.append(np.where(valid, table[np.clip(rel, 0, WINDOW - 1)], -1))
    return np.stack(tiles).astype(np.int32)


def _swa_bias_kernel(rb_ref, idx_ref, o_ref):
    h = pl.program_id(0)
    for tile in range(2):
        idx = idx_ref[tile]
        acc = jnp.full(idx.shape, NEG_BIG, F32)
        for bucket in range(N_BUCKETS):
            acc = jnp.where(idx == bucket, rb_ref[bucket, h], acc)
        o_ref[tile, 0] = acc


def _swa_bias(rel_bias):
    idx = jnp.asarray(_swa_bucket_tiles())
    return pl.pallas_call(
        _swa_bias_kernel,
        grid=(N_SWA_HEADS,),
        in_specs=[
            pl.BlockSpec(memory_space=pltpu.SMEM),
            pl.BlockSpec((2, WINDOW, 2 * WINDOW), lambda h: (0, 0, 0)),
        ],
        out_specs=pl.BlockSpec((2, 1, WINDOW, 2 * WINDOW), lambda h: (0, h, 0, 0)),
        out_shape=jax.ShapeDtypeStruct((2, N_SWA_HEADS, WINDOW, 2 * WINDOW), F32),
        compiler_params=_params(("arbitrary",)),
        name="swa_bias",
    )(rel_bias, idx)


def _swa_kernel(sink_ref, q_ref, k_ref, v_ref, bias_ref, o_ref, *, tq):
    qt = pl.program_id(1)
    scale = jnp.asarray(QK_SCALE, BF16)
    group = N_SWA_HEADS // N_SWA_KV
    for j in range(tq // WINDOW):
        nb = qt * (tq // WINDOW) + j
        first = (nb == 0).astype(jnp.int32) if j == 0 else 0
        w0 = pl.multiple_of(jnp.maximum(nb - 1, 0) * WINDOW, WINDOW)
        rows = slice(j * WINDOW, (j + 1) * WINDOW)
        for g in range(N_SWA_KV):
            kwin = k_ref[pl.ds(w0, 2 * WINDOW), g * LANES:(g + 1) * LANES]
            vwin = v_ref[pl.ds(w0, 2 * WINDOW), g * LANES:(g + 1) * LANES]
            for c in range(group // 2):
                pair = g * (group // 2) + c
                cols = slice(pair * LANES, (pair + 1) * LANES)
                q_heads = _head_pair(q_ref[rows, cols] * scale)
                outs = []
                for i in range(2):
                    head = 2 * pair + i
                    sink = sink_ref[head]
                    logits = _dot_nt(q_heads[i], kwin) + bias_ref[first, head]
                    m = jnp.maximum(jnp.max(logits, axis=-1, keepdims=True), sink)
                    p = jnp.exp(logits - m)
                    denom = jnp.sum(p, axis=-1, keepdims=True) + jnp.exp(sink - m)
                    outs.append(_dot(p.astype(BF16), vwin) / denom)
                o_ref[rows, cols] = _merge_pair(outs[0], outs[1]).astype(o_ref.dtype)


def _swa_attention(proj, bias, sinks, tq=512):
    b, s, width = proj.shape
    q_blk = (width - SWA_QW - 4 * LANES) // SWA_QW
    k_blk = (width - 4 * LANES) // (2 * LANES)
    return pl.pallas_call(
        functools.partial(_swa_kernel, tq=tq),
        grid=(b, s // tq),
        in_specs=[
            pl.BlockSpec(memory_space=pltpu.SMEM),
            pl.BlockSpec((None, tq, SWA_QW), lambda i, j: (i, j, q_blk)),
            pl.BlockSpec((None, s, 2 * LANES), lambda i, j: (i, 0, k_blk)),
            pl.BlockSpec((None, s, 2 * LANES), lambda i, j: (i, 0, k_blk + 1)),
            _resident((2, N_SWA_HEADS, WINDOW, 2 * WINDOW), lambda i, j: (0, 0, 0, 0)),
        ],
        out_specs=pl.BlockSpec((None, tq, SWA_QW), lambda i, j: (i, j, 0)),
        out_shape=jax.ShapeDtypeStruct((b, s, SWA_QW), BF16),
        compiler_params=_params(("parallel", "arbitrary")),
        name="swa_attention",
    )(sinks, proj, proj, proj, bias)


def _fox_gate_kernel(f_ref, bf_ref, c_ref, *, chunk):
    s = f_ref.shape[0]
    r = lax.broadcasted_iota(jnp.int32, (chunk, chunk), 0)
    c = lax.broadcasted_iota(jnp.int32, (chunk, chunk), 1)
    upto = jnp.where(c <= r, 1.0, 0.0).astype(BF16)
    lane = lax.broadcasted_iota(jnp.int32, (chunk, LANES), 1)
    carry = jnp.zeros((1, LANES), F32)
    for j in range(s // chunk):
        rows = slice(j * chunk, (j + 1) * chunk)
        log_f = _log2_sigmoid((f_ref[rows, :] + bf_ref[...]) * LOG2E)
        cs = carry
        for part in _split_bf16(log_f, 3):
            cs = cs + _dot(upto, part)
        carry = cs[chunk - 1:chunk, :]
        out = jnp.zeros((chunk, LANES), BF16)
        for k, part in reversed(list(enumerate(_split_bf16(cs, GATE_PARTS)))):
            out = jnp.where(lane < (k + 1) * N_FOX_HEADS, part, out)
        c_ref[rows, :] = out


def _fox_gates(f, b_f, chunk=256):
    b, s, _ = f.shape
    return pl.pallas_call(
        functools.partial(_fox_gate_kernel, chunk=chunk),
        grid=(b,),
        in_specs=[
            pl.BlockSpec((None, s, LANES), lambda i: (i, 0, 0)),
            _resident((1, LANES), lambda i: (0, 0)),
        ],
        out_specs=pl.BlockSpec((None, s, LANES), lambda i: (i, 0, 0)),
        out_shape=jax.ShapeDtypeStruct((b, s, LANES), BF16),
        compiler_params=_params(("parallel",)),
        name="fox_gates",
    )(f, b_f)


def _fox_kernel(qt_ref, k_ref, vt_ref, c_ref, o_ref, sa_ref, sb_ref, pa_ref, pb_ref, acc_ref):
    qi = pl.program_id(2)
    q_heads = _head_operands_t(qt_ref[...])
    heads = range(len(q_heads))
    first_head = pl.program_id(1) * len(q_heads)
    gate_row = lax.broadcasted_iota(jnp.int32, (LANES, QUERY_TILE), 0)
    masks = _diagonal_masks(strict=False)
    w_heads = []
    for h in heads:
        picks = ((gate_row & (N_FOX_HEADS - 1)) == first_head + h) & (gate_row < GATE_PARTS * N_FOX_HEADS)
        minus_gate = jnp.where(picks, -1.0, 0.0).astype(BF16)
        w_heads.append(jnp.concatenate([q_heads[h], minus_gate], axis=0))

    def scores(kb, s_ref):
        start = pl.multiple_of(kb * KEY_BLOCK, KEY_BLOCK)
        gates = c_ref[pl.ds(start, KEY_BLOCK), :]
        for h in heads:
            s_ref[h] = _dot(jnp.concatenate([_key_rows(k_ref, kb, h), gates], axis=1), w_heads[h])

    def accumulate(kb, p_ref, a):
        for h in heads:
            pv = _dot(_value_rows_t(vt_ref, kb, h), p_ref[h].astype(BF16))
            acc_ref[h] = a[h] * acc_ref[h] + pv

    def softmax(s_ref, p_ref, stats, mask):
        new_stats, a = [], []
        for h in heads:
            m, l = stats[h]
            s2 = s_ref[h] if mask is None else jnp.where(mask, s_ref[h], NEG_BIG)
            m_new = jnp.maximum(m, jnp.max(s2, axis=0, keepdims=True))
            ah = jnp.exp2(m - m_new)
            p = jnp.exp2(s2 - m_new)
            p_ref[h] = p
            new_stats.append((m_new, ah * l + jnp.sum(p, axis=0, keepdims=True)))
            a.append(ah)
        return new_stats, a

    def half_step(kb, cur, oth, stats, a_prev, *, prefetch=True, mask=None):
        if prefetch:
            scores(kb + 1, oth[0])
        accumulate(jnp.maximum(kb - 1, 0), oth[1], a_prev)
        return softmax(cur[0], cur[1], stats, mask)

    slot_a, slot_b = (sa_ref, pa_ref), (sb_ref, pb_ref)
    pb_ref[...] = jnp.zeros_like(pb_ref)
    acc_ref[...] = jnp.zeros_like(acc_ref)
    scores(0, sa_ref)
    stats = [(jnp.full((1, QUERY_TILE), NEG_BIG, F32), jnp.zeros((1, QUERY_TILE), F32)) for _ in heads]
    ones = [jnp.ones((1, QUERY_TILE), F32) for _ in heads]

    def body(i, carry):
        stats, a = carry
        stats, a = half_step(2 * i, slot_a, slot_b, stats, a)
        return half_step(2 * i + 1, slot_b, slot_a, stats, a)

    stats, a = lax.fori_loop(0, qi, body, (stats, ones))
    stats, a = half_step(2 * qi, slot_a, slot_b, stats, a, mask=masks[0])
    stats, a = half_step(2 * qi + 1, slot_b, slot_a, stats, a, prefetch=False, mask=masks[1])
    accumulate(2 * qi + 1, pb_ref, a)
    for h in heads:
        o_ref[h * HEAD_DIM:(h + 1) * HEAD_DIM, :] = (acc_ref[h] / stats[h][1]).astype(o_ref.dtype)


def _fox_attention(proj, q_t, v_t, gates, pairs=2):
    b, s, _ = proj.shape
    groups = N_FOX_PAIRS // pairs
    return pl.pallas_call(
        _fox_kernel,
        grid=(b, groups, s // QUERY_TILE),
        in_specs=[
            pl.BlockSpec((None, pairs * LANES, QUERY_TILE), lambda i, h, j: (i, h, j)),
            pl.BlockSpec((None, s, pairs * LANES), lambda i, h, j: (i, 0, h)),
            pl.BlockSpec((None, s // KEY_BLOCK, pairs * LANES, KEY_BLOCK), lambda i, h, j: (i, 0, h, 0)),
            pl.BlockSpec((None, s, LANES), lambda i, h, j: (i, 0, 0)),
        ],
        out_specs=pl.BlockSpec((None, pairs * LANES, QUERY_TILE), lambda i, h, j: (i, h, j)),
        out_shape=jax.ShapeDtypeStruct((b, FOX_W, s), BF16),
        scratch_shapes=_attention_scratch(pairs),
        compiler_params=_params(("parallel", "parallel", "arbitrary")),
        name="fox_attention",
    )(q_t, proj, v_t, gates)


def _even_mixer(h, w_in, w_out, sinks, rel_bias, g, b):
    kv0 = 3 * SB_W + SWA_QW
    dup = lambda w: jnp.repeat(w.reshape(D_MODEL, N_SWA_KV, HEAD_DIM), 2, axis=1).reshape(D_MODEL, -1)
    w_cat = jnp.concatenate(
        [w_in[:, SB_W:2 * SB_W], w_in[:, 3 * SB_W:kv0],
         dup(w_in[:, kv0:kv0 + SWA_KVW]), dup(w_in[:, kv0 + SWA_KVW:])], axis=1)
    proj, q_t, v_t = _proj(h, w_cat.astype(BF16), w_in[:, :SB_W].T.astype(BF16),
                           w_in[:, 2 * SB_W:3 * SB_W].T.astype(BF16), -QK_SCALE * LOG2E)
    o_sb = _sb_attention(proj, q_t, v_t)
    o_sw = _swa_attention(proj, _swa_bias(rel_bias), sinks.astype(F32))
    return _outproj_ln(o_sb, o_sw, h, w_out.astype(BF16), g, b)


def _odd_mixer(h, w_in, b_f, w_out, g, b):
    spread = lambda v: jnp.pad(jnp.tile(v, (1, GATE_PARTS)), ((0, 0), (0, LANES - GATE_PARTS * N_FOX_HEADS)))
    w_cat = jnp.concatenate([w_in[:, FOX_W:2 * FOX_W], spread(w_in[:, 3 * FOX_W:])], axis=1)
    proj, q_t, v_t, f = _proj(h, w_cat.astype(BF16), w_in[:, :FOX_W].T.astype(BF16),
                              w_in[:, 2 * FOX_W:3 * FOX_W].T.astype(BF16), QK_SCALE * LOG2E, tail=LANES)
    gates = _fox_gates(f, spread(b_f.reshape(1, -1).astype(F32)))
    o = _fox_attention(proj, q_t, v_t, gates)
    return _outproj_ln(o, None, h, w_out.astype(BF16), g, b)


def kernel(x, ln_g, ln_b, ffn1_in, ffn1_out, ffn2_in, ffn2_out, ab_w_in, ab_w_out, ab_sinks,
           fox_w_in, fox_b_f, fox_w_out, rel_bias):
    bsz, s, d = x.shape
    assert (s, d) == (SEQ, D_MODEL) and ln_g.shape[0] == DEPTH
    row = lambda v: v.reshape(1, D_MODEL).astype(F32)
    ffn = lambda h, w_in, w_out, g, b: _ffn_ln(
        h.reshape(bsz * s, d), w_in.astype(BF16), w_out.astype(BF16), row(g), row(b)).reshape(bsz, s, d)
    h = x
    for layer in range(DEPTH):
        g, b = ln_g[layer], ln_b[layer]
        h = ffn(h, ffn1_in[layer], ffn1_out[layer], g[0], b[0])
        i = layer // 2
        if layer % 2 == 0:
            h = _even_mixer(h, ab_w_in[i], ab_w_out[i], ab_sinks[i], rel_bias, row(g[1]), row(b[1]))
        else:
            h = _odd_mixer(h, fox_w_in[i], fox_b_f[i], fox_w_out[i], row(g[1]), row(b[1]))
        h = ffn(h, ffn2_in[layer], ffn2_out[layer], g[2], b[2])
    return h
```

```python
import functools
import math

import numpy as np
import jax
import jax.numpy as jnp
from jax import lax
from jax.experimental import pallas as pl
from jax.experimental.pallas import tpu as pltpu

D_MODEL = 1024
SEQ = 2048
DEPTH = 2
HEAD_DIM = 64
LANES = 128
N_SB_PAIRS = 4
N_SWA_HEADS = 8
N_SWA_KV = 2
N_FOX_HEADS = 16
N_FOX_PAIRS = 8
WINDOW = 128
N_BUCKETS = 32
MAX_DISTANCE = 128
D_FF = 2816
ALPHA = (2 * DEPTH) ** 0.25
LN_EPS = 1e-5
SB_W = 512
SWA_QW = 512
SWA_KVW = 128
FOX_W = 1024
NEG_BIG = -1e30
LOG2E = math.log2(math.e)
QK_SCALE = HEAD_DIM ** -0.5
GATE_PARTS = 3
KEY_BLOCK = 256
QUERY_TILE = 2 * KEY_BLOCK
VMEM_LIMIT = 56 * 1024 * 1024

BF16 = jnp.bfloat16
F32 = jnp.float32


def _dot(a, b):
    return jnp.dot(a, b, preferred_element_type=F32)


def _dot_nt(a, b):
    return lax.dot_general(a, b, (((1,), (1,)), ((), ())), preferred_element_type=F32)


def _dot_tn(a, b):
    return lax.dot_general(a, b, (((0,), (0,)), ((), ())), preferred_element_type=F32)


def _layer_norm(z, g, b):
    mu = jnp.mean(z, axis=-1, keepdims=True)
    d = z - mu
    var = jnp.mean(d * d, axis=-1, keepdims=True)
    return d * lax.rsqrt(var + LN_EPS) * g + b


def _log2_sigmoid(x2):
    sign = jnp.uint32(0x80000000)
    neg_abs = lax.bitcast_convert_type(lax.bitcast_convert_type(x2, jnp.uint32) | sign, F32)
    return jnp.minimum(x2, 0.0) - jnp.log2(1.0 + jnp.exp2(neg_abs))


def _split_bf16(x, parts):
    out = []
    r = x
    for i in range(parts):
        t = r.astype(BF16)
        out.append(t)
        if i + 1 < parts:
            r = r - t.astype(F32)
    return out


def _params(sem):
    return pltpu.CompilerParams(dimension_semantics=sem, vmem_limit_bytes=VMEM_LIMIT)


def _resident(shape, index_map):
    return pl.BlockSpec(shape, index_map, pipeline_mode=pl.Buffered(1))


def _swiglu_ln(x, wg_ref, wu_ref, wo_ref, g_ref, b_ref):
    xb = x.astype(BF16)
    gate = _dot(xb, wg_ref[...])
    up = _dot(xb, wu_ref[...])
    h = (gate * jax.nn.sigmoid(gate) * up).astype(BF16)
    y = _dot(h, wo_ref[...])
    return _layer_norm(ALPHA * x + 0.5 * y, g_ref[...], b_ref[...])


def _ffn_ln_kernel(x_ref, wg_ref, wu_ref, wo_ref, g_ref, b_ref, o_ref):
    o_ref[...] = _swiglu_ln(x_ref[...], wg_ref, wu_ref, wo_ref, g_ref, b_ref)


def _mix_ffn_ln_kernel(*refs, n_normal):
    ot_ref, refs = refs[0], refs[1:]
    o_refs, refs = refs[:n_normal], refs[n_normal:]
    h_ref, wt_ref, refs = refs[0], refs[1], refs[2:]
    w_refs, refs = refs[:n_normal], refs[n_normal:]
    g1_ref, b1_ref, wg_ref, wu_ref, wo_ref, g2_ref, b2_ref, out_ref = refs
    mix = _dot_tn(ot_ref[...], wt_ref[...])
    for o_ref, w_ref in zip(o_refs, w_refs):
        mix = mix + _dot(o_ref[...], w_ref[...])
    x = _layer_norm(ALPHA * h_ref[...] + mix, g1_ref[...], b1_ref[...])
    out_ref[...] = _swiglu_ln(x, wg_ref, wu_ref, wo_ref, g2_ref, b2_ref)


def _ffn_ln(x, w_in, w_out, g, b, tm=512):
    n = x.shape[0]
    return pl.pallas_call(
        _ffn_ln_kernel,
        grid=(n // tm,),
        in_specs=[
            pl.BlockSpec((tm, D_MODEL), lambda i: (i, 0)),
            _resident((D_MODEL, D_FF), lambda i: (0, 0)),
            _resident((D_MODEL, D_FF), lambda i: (0, 1)),
            _resident((D_FF, D_MODEL), lambda i: (0, 0)),
            _resident((1, D_MODEL), lambda i: (0, 0)),
            _resident((1, D_MODEL), lambda i: (0, 0)),
        ],
        out_specs=pl.BlockSpec((tm, D_MODEL), lambda i: (i, 0)),
        out_shape=jax.ShapeDtypeStruct((n, D_MODEL), F32),
        compiler_params=_params(("parallel",)),
        name="ffn_ln",
    )(x, w_in, w_in, w_out, g, b)


def _proj_kernel(x_ref, w_ref, wt_ref, o_ref, qt_ref, vt_ref, *tail_ref, q_scale):
    xb = x_ref[...].astype(BF16)
    width = o_ref.shape[-1]
    acc = _dot(xb, w_ref[...])
    o_ref[...] = acc[:, :width].astype(o_ref.dtype)
    for ref in tail_ref:
        ref[...] = acc[:, width:]
    acc_t = _dot_nt(wt_ref[...], xb)
    qw = qt_ref.shape[0]
    qt_ref[...] = (acc_t[:qw] * q_scale).astype(qt_ref.dtype)
    for j in range(vt_ref.shape[0]):
        vt_ref[j] = acc_t[qw:, j * KEY_BLOCK:(j + 1) * KEY_BLOCK].astype(vt_ref.dtype)


def _proj(x, w, wq_t, wv_t, q_scale, tail=0, tm=512):
    b, s, _ = x.shape
    width, qw, vw = w.shape[1] - tail, wq_t.shape[0], wv_t.shape[0]
    tok = lambda n: pl.BlockSpec((None, tm, n), lambda i, j: (i, j, 0))
    out_specs = [tok(width), pl.BlockSpec((None, qw, tm), lambda i, j: (i, 0, j)),
                 pl.BlockSpec((None, tm // KEY_BLOCK, vw, KEY_BLOCK), lambda i, j: (i, j, 0, 0))]
    out_shape = [jax.ShapeDtypeStruct((b, s, width), BF16), jax.ShapeDtypeStruct((b, qw, s), BF16),
                 jax.ShapeDtypeStruct((b, s // KEY_BLOCK, vw, KEY_BLOCK), BF16)]
    if tail:
        out_specs.append(tok(tail))
        out_shape.append(jax.ShapeDtypeStruct((b, s, tail), F32))
    return pl.pallas_call(
        functools.partial(_proj_kernel, q_scale=q_scale),
        grid=(b, s // tm),
        in_specs=[
            tok(D_MODEL),
            _resident((D_MODEL, width + tail), lambda i, j: (0, 0)),
            _resident((qw + vw, D_MODEL), lambda i, j: (0, 0)),
        ],
        out_specs=out_specs,
        out_shape=out_shape,
        compiler_params=_params(("parallel", "parallel")),
        name="in_proj",
    )(x, w, jnp.concatenate([wq_t, wv_t], axis=0))


def _mix_ffn_ln(o_t, o, h, w_mix, ln1, w_in, w_out, ln2, tm=512):
    bsz, s, _ = h.shape
    wt = o_t.shape[1]
    const = lambda i, j: (0, 0)
    tok = lambda width: pl.BlockSpec((None, tm, width), lambda i, j: (i, j, 0))
    row = _resident((1, D_MODEL), const)
    normal = [] if o is None else [o]
    in_specs = ([pl.BlockSpec((None, wt, tm), lambda i, j: (i, 0, j))] + [tok(a.shape[2]) for a in normal]
                + [tok(D_MODEL), _resident((wt, D_MODEL), const)]
                + [_resident((a.shape[2], D_MODEL), lambda i, j: (wt // a.shape[2], 0)) for a in normal]
                + [row, row, _resident((D_MODEL, D_FF), const), _resident((D_MODEL, D_FF), lambda i, j: (0, 1)),
                   _resident((D_FF, D_MODEL), const), row, row])
    args = [o_t] + normal + [h, w_mix] + [w_mix for _ in normal] + [*ln1, w_in, w_in, w_out, *ln2]
    return pl.pallas_call(
        functools.partial(_mix_ffn_ln_kernel, n_normal=len(normal)),
        grid=(bsz, s // tm),
        in_specs=in_specs,
        out_specs=tok(D_MODEL),
        out_shape=jax.ShapeDtypeStruct((bsz, s, D_MODEL), F32),
        compiler_params=_params(("parallel", "parallel")),
        name="mix_ffn_ln",
    )(*args)


def _head_pair(q):
    lo = lax.broadcasted_iota(jnp.int32, q.shape, 1) < HEAD_DIM
    zero = jnp.zeros_like(q)
    return jnp.where(lo, q, zero), jnp.where(lo, zero, q)


def _merge_pair(a0, a1):
    lo = lax.broadcasted_iota(jnp.int32, a0.shape, 1) < HEAD_DIM
    return jnp.where(lo, a0, a1)


def _head_operands_t(q_t):
    top = lax.broadcasted_iota(jnp.int32, (LANES, q_t.shape[1]), 0) < HEAD_DIM
    out = []
    for p in range(q_t.shape[0] // LANES):
        tile = q_t[p * LANES:(p + 1) * LANES]
        zero = jnp.zeros_like(tile)
        out += [jnp.where(top, tile, zero), jnp.where(top, zero, tile)]
    return out


def _key_rows(k_ref, kb, h):
    start = pl.multiple_of(kb * KEY_BLOCK, KEY_BLOCK)
    return k_ref[pl.ds(start, KEY_BLOCK), (h // 2) * LANES:(h // 2 + 1) * LANES]


def _value_rows_t(vt_ref, kb, h):
    return vt_ref[kb, h * HEAD_DIM:(h + 1) * HEAD_DIM, :]


def _diagonal_masks(strict):
    row = lax.broadcasted_iota(jnp.int32, (KEY_BLOCK, QUERY_TILE), 0)
    col = lax.broadcasted_iota(jnp.int32, (KEY_BLOCK, QUERY_TILE), 1)
    if strict:
        return row < col, row + KEY_BLOCK < col
    return row <= col, row + KEY_BLOCK <= col


def _attention_scratch(pairs):
    tile = (2 * pairs, KEY_BLOCK, QUERY_TILE)
    return [pltpu.VMEM(tile, F32), pltpu.VMEM(tile, F32),
            pltpu.VMEM(tile, F32), pltpu.VMEM(tile, F32),
            pltpu.VMEM((2 * pairs, HEAD_DIM, QUERY_TILE), F32)]


def _sb_kernel(qt_ref, k_ref, vt_ref, o_ref, nza_ref, nzb_ref, wa_ref, wb_ref, acc_ref):
    qi = pl.program_id(2)
    w_heads = _head_operands_t(qt_ref[...])
    heads = range(len(w_heads))
    masks = _diagonal_masks(strict=True)
    row = lax.broadcasted_iota(jnp.int32, (KEY_BLOCK, KEY_BLOCK), 0)
    col = lax.broadcasted_iota(jnp.int32, (KEY_BLOCK, KEY_BLOCK), 1)
    suffix = jnp.where(col >= row, 1.0, 0.0).astype(BF16)
    last = 2 * qi + 1

    def scores(kb, nz_ref):
        for h in heads:
            nz_ref[h] = _dot(_key_rows(k_ref, kb, h), w_heads[h])

    def accumulate(kb, w_ref):
        for h in heads:
            acc_ref[h] += _dot(_value_rows_t(vt_ref, kb, h), w_ref[h].astype(BF16))

    def weights(nz_ref, w_ref, carries, mask):
        incl = []
        for h in heads:
            log_keep = _log2_sigmoid(nz_ref[h])
            if mask is not None:
                log_keep = jnp.where(mask, log_keep, 0.0)
            incl.append(_dot(suffix, log_keep.astype(BF16)))
        for h in heads:
            w = jnp.exp2(incl[h] + carries[h] - nz_ref[h])
            if mask is not None:
                w = jnp.where(mask, w, 0.0)
            w_ref[h] = w
        return [carries[h] + incl[h][0:1, :] for h in heads]

    def half_step(j, cur, oth, carries, *, mask=None, first=False):
        kb = last - j
        scores(jnp.maximum(kb - 1, 0), oth[0])
        if not first:
            accumulate(kb + 1, oth[1])
        return weights(cur[0], cur[1], carries, mask)

    slot_a, slot_b = (nza_ref, wa_ref), (nzb_ref, wb_ref)
    acc_ref[...] = jnp.zeros_like(acc_ref)
    scores(last, nza_ref)
    carries = [jnp.zeros((1, QUERY_TILE), F32) for _ in heads]
    carries = half_step(0, slot_a, slot_b, carries, mask=masks[1], first=True)
    carries = half_step(1, slot_b, slot_a, carries, mask=masks[0])

    def body(i, carries):
        carries = half_step(2 * i + 2, slot_a, slot_b, carries)
        return half_step(2 * i + 3, slot_b, slot_a, carries)

    lax.fori_loop(0, qi, body, carries)
    accumulate(0, wb_ref)
    for h in heads:
        o_ref[h * HEAD_DIM:(h + 1) * HEAD_DIM, :] = acc_ref[h].astype(o_ref.dtype)


def _sb_attention(proj, q_t, v_t, pairs=2):
    b, s, _ = proj.shape
    groups = N_SB_PAIRS // pairs
    return pl.pallas_call(
        _sb_kernel,
        grid=(b, groups, s // QUERY_TILE),
        in_specs=[
            pl.BlockSpec((None, pairs * LANES, QUERY_TILE), lambda i, h, j: (i, h, j)),
            pl.BlockSpec((None, s, pairs * LANES), lambda i, h, j: (i, 0, h)),
            pl.BlockSpec((None, s // KEY_BLOCK, pairs * LANES, KEY_BLOCK), lambda i, h, j: (i, 0, h, 0)),
        ],
        out_specs=pl.BlockSpec((None, pairs * LANES, QUERY_TILE), lambda i, h, j: (i, h, j)),
        out_shape=jax.ShapeDtypeStruct((b, SB_W, s), BF16),
        scratch_shapes=_attention_scratch(pairs),
        compiler_params=_params(("parallel", "parallel", "arbitrary")),
        name="sb_attention",
    )(q_t, proj, v_t)


def _t5_bucket_table():
    n = np.arange(WINDOW)
    max_exact = N_BUCKETS // 2
    nf = np.maximum(n, 1).astype(np.float64)
    val = np.log(nf / max_exact) / math.log(MAX_DISTANCE / max_exact) * (N_BUCKETS - max_exact)
    frac = np.abs(val - np.round(val))[n > max_exact]
    assert frac.min() > 1e-3, "bucket boundary too close to an integer distance for float32"
    large = np.minimum(max_exact + val.astype(np.int64), N_BUCKETS - 1)
    return np.where(n < max_exact, n, large)


def _swa_bucket_tiles():
    table = _t5_bucket_table()
    qi = np.arange(WINDOW)[:, None]
    kj = np.arange(2 * WINDOW)[None, :]
    tiles = []
    for offset in (WINDOW, 0):
        rel = qi + offset - kj
        valid = (rel >= 0) & (rel < WINDOW)
        tiles.append(np.where(valid, table[np.clip(rel, 0, WINDOW - 1)], -1))
    return np.stack(tiles).astype(np.int32)


def _swa_bias_kernel(rb_ref, idx_ref, o_ref):
    h = pl.program_id(0)
    for tile in range(2):
        idx = idx_ref[tile]
        acc = jnp.full(idx.shape, NEG_BIG, F32)
        for bucket in range(N_BUCKETS):
            acc = jnp.where(idx == bucket, rb_ref[bucket, h], acc)
        o_ref[tile, 0] = acc


def _swa_bias(rel_bias):
    idx = jnp.asarray(_swa_bucket_tiles())
    return pl.pallas_call(
        _swa_bias_kernel,
        grid=(N_SWA_HEADS,),
        in_specs=[
            pl.BlockSpec(memory_space=pltpu.SMEM),
            pl.BlockSpec((2, WINDOW, 2 * WINDOW), lambda h: (0, 0, 0)),
        ],
        out_specs=pl.BlockSpec((2, 1, WINDOW, 2 * WINDOW), lambda h: (0, h, 0, 0)),
        out_shape=jax.ShapeDtypeStruct((2, N_SWA_HEADS, WINDOW, 2 * WINDOW), F32),
        compiler_params=_params(("arbitrary",)),
        name="swa_bias",
    )(rel_bias, idx)


def _swa_kernel(sink_ref, q_ref, k_ref, v_ref, bias_ref, o_ref, *, tq):
    qt = pl.program_id(1)
    scale = jnp.asarray(QK_SCALE, BF16)
    group = N_SWA_HEADS // N_SWA_KV
    for j in range(tq // WINDOW):
        nb = qt * (tq // WINDOW) + j
        first = (nb == 0).astype(jnp.int32) if j == 0 else 0
        w0 = pl.multiple_of(jnp.maximum(nb - 1, 0) * WINDOW, WINDOW)
        rows = slice(j * WINDOW, (j + 1) * WINDOW)
        for g in range(N_SWA_KV):
            kwin = k_ref[pl.ds(w0, 2 * WINDOW), g * LANES:(g + 1) * LANES]
            vwin = v_ref[pl.ds(w0, 2 * WINDOW), g * LANES:(g + 1) * LANES]
            for c in range(group // 2):
                pair = g * (group // 2) + c
                cols = slice(pair * LANES, (pair + 1) * LANES)
                q_heads = _head_pair(q_ref[rows, cols] * scale)
                outs = []
                for i in range(2):
                    head = 2 * pair + i
                    sink = sink_ref[head]
                    logits = _dot_nt(q_heads[i], kwin) + bias_ref[first, head]
                    m = jnp.maximum(jnp.max(logits, axis=-1, keepdims=True), sink)
                    p = jnp.exp(logits - m)
                    denom = jnp.sum(p, axis=-1, keepdims=True) + jnp.exp(sink - m)
                    outs.append(_dot(p.astype(BF16), vwin) / denom)
                o_ref[rows, cols] = _merge_pair(outs[0], outs[1]).astype(o_ref.dtype)


def _swa_attention(proj, bias, sinks, tq=512):
    b, s, width = proj.shape
    q_blk = (width - SWA_QW - 4 * LANES) // SWA_QW
    k_blk = (width - 4 * LANES) // (2 * LANES)
    return pl.pallas_call(
        functools.partial(_swa_kernel, tq=tq),
        grid=(b, s // tq),
        in_specs=[
            pl.BlockSpec(memory_space=pltpu.SMEM),
            pl.BlockSpec((None, tq, SWA_QW), lambda i, j: (i, j, q_blk)),
            pl.BlockSpec((None, s, 2 * LANES), lambda i, j: (i, 0, k_blk)),
            pl.BlockSpec((None, s, 2 * LANES), lambda i, j: (i, 0, k_blk + 1)),
            _resident((2, N_SWA_HEADS, WINDOW, 2 * WINDOW), lambda i, j: (0, 0, 0, 0)),
        ],
        out_specs=pl.BlockSpec((None, tq, SWA_QW), lambda i, j: (i, j, 0)),
        out_shape=jax.ShapeDtypeStruct((b, s, SWA_QW), BF16),
        compiler_params=_params(("parallel", "arbitrary")),
        name="swa_attention",
    )(sinks, proj, proj, proj, bias)


def _fox_gate_kernel(f_ref, bf_ref, c_ref, *, chunk):
    s = f_ref.shape[0]
    r = lax.broadcasted_iota(jnp.int32, (chunk, chunk), 0)
    c = lax.broadcasted_iota(jnp.int32, (chunk, chunk), 1)
    upto = jnp.where(c <= r, 1.0, 0.0).astype(BF16)
    lane = lax.broadcasted_iota(jnp.int32, (chunk, LANES), 1)
    carry = jnp.zeros((1, LANES), F32)
    for j in range(s // chunk):
        rows = slice(j * chunk, (j + 1) * chunk)
        log_f = _log2_sigmoid((f_ref[rows, :] + bf_ref[...]) * LOG2E)
        cs = carry
        for part in _split_bf16(log_f, 3):
            cs = cs + _dot(upto, part)
        carry = cs[chunk - 1:chunk, :]
        out = jnp.zeros((chunk, LANES), BF16)
        for k, part in reversed(list(enumerate(_split_bf16(cs, GATE_PARTS)))):
            out = jnp.where(lane < (k + 1) * N_FOX_HEADS, part, out)
        c_ref[rows, :] = out


def _fox_gates(f, b_f, chunk=256):
    b, s, _ = f.shape
    return pl.pallas_call(
        functools.partial(_fox_gate_kernel, chunk=chunk),
        grid=(b,),
        in_specs=[
            pl.BlockSpec((None, s, LANES), lambda i: (i, 0, 0)),
            _resident((1, LANES), lambda i: (0, 0)),
        ],
        out_specs=pl.BlockSpec((None, s, LANES), lambda i: (i, 0, 0)),
        out_shape=jax.ShapeDtypeStruct((b, s, LANES), BF16),
        compiler_params=_params(("parallel",)),
        name="fox_gates",
    )(f, b_f)


def _fox_kernel(qt_ref, k_ref, vt_ref, c_ref, o_ref, sa_ref, sb_ref, pa_ref, pb_ref, acc_ref):
    qi = pl.program_id(2)
    q_heads = _head_operands_t(qt_ref[...])
    heads = range(len(q_heads))
    first_head = pl.program_id(1) * len(q_heads)
    gate_row = lax.broadcasted_iota(jnp.int32, (LANES, QUERY_TILE), 0)
    masks = _diagonal_masks(strict=False)
    w_heads = []
    for h in heads:
        picks = ((gate_row & (N_FOX_HEADS - 1)) == first_head + h) & (gate_row < GATE_PARTS * N_FOX_HEADS)
        minus_gate = jnp.where(picks, -1.0, 0.0).astype(BF16)
        w_heads.append(jnp.concatenate([q_heads[h], minus_gate], axis=0))

    def scores(kb, s_ref):
        start = pl.multiple_of(kb * KEY_BLOCK, KEY_BLOCK)
        gates = c_ref[pl.ds(start, KEY_BLOCK), :]
        for h in heads:
            s_ref[h] = _dot(jnp.concatenate([_key_rows(k_ref, kb, h), gates], axis=1), w_heads[h])

    def accumulate(kb, p_ref, a):
        for h in heads:
            pv = _dot(_value_rows_t(vt_ref, kb, h), p_ref[h].astype(BF16))
            acc_ref[h] = a[h] * acc_ref[h] + pv

    def softmax(s_ref, p_ref, stats, mask):
        new_stats, a = [], []
        for h in heads:
            m, l = stats[h]
            s2 = s_ref[h] if mask is None else jnp.where(mask, s_ref[h], NEG_BIG)
            m_new = jnp.maximum(m, jnp.max(s2, axis=0, keepdims=True))
            ah = jnp.exp2(m - m_new)
            p = jnp.exp2(s2 - m_new)
            p_ref[h] = p
            new_stats.append((m_new, ah * l + jnp.sum(p, axis=0, keepdims=True)))
            a.append(ah)
        return new_stats, a

    def half_step(kb, cur, oth, stats, a_prev, *, prefetch=True, mask=None):
        if prefetch:
            scores(kb + 1, oth[0])
        accumulate(jnp.maximum(kb - 1, 0), oth[1], a_prev)
        return softmax(cur[0], cur[1], stats, mask)

    slot_a, slot_b = (sa_ref, pa_ref), (sb_ref, pb_ref)
    pb_ref[...] = jnp.zeros_like(pb_ref)
    acc_ref[...] = jnp.zeros_like(acc_ref)
    scores(0, sa_ref)
    stats = [(jnp.full((1, QUERY_TILE), NEG_BIG, F32), jnp.zeros((1, QUERY_TILE), F32)) for _ in heads]
    ones = [jnp.ones((1, QUERY_TILE), F32) for _ in heads]

    def body(i, carry):
        stats, a = carry
        stats, a = half_step(2 * i, slot_a, slot_b, stats, a)
        return half_step(2 * i + 1, slot_b, slot_a, stats, a)

    stats, a = lax.fori_loop(0, qi, body, (stats, ones))
    stats, a = half_step(2 * qi, slot_a, slot_b, stats, a, mask=masks[0])
    stats, a = half_step(2 * qi + 1, slot_b, slot_a, stats, a, prefetch=False, mask=masks[1])
    accumulate(2 * qi + 1, pb_ref, a)
    for h in heads:
        o_ref[h * HEAD_DIM:(h + 1) * HEAD_DIM, :] = (acc_ref[h] / stats[h][1]).astype(o_ref.dtype)


def _fox_attention(proj, q_t, v_t, gates, pairs=4):
    b, s, _ = proj.shape
    groups = N_FOX_PAIRS // pairs
    return pl.pallas_call(
        _fox_kernel,
        grid=(b, groups, s // QUERY_TILE),
        in_specs=[
            pl.BlockSpec((None, pairs * LANES, QUERY_TILE), lambda i, h, j: (i, h, j)),
            pl.BlockSpec((None, s, pairs * LANES), lambda i, h, j: (i, 0, h)),
            pl.BlockSpec((None, s // KEY_BLOCK, pairs * LANES, KEY_BLOCK), lambda i, h, j: (i, 0, h, 0)),
            pl.BlockSpec((None, s, LANES), lambda i, h, j: (i, 0, 0)),
        ],
        out_specs=pl.BlockSpec((None, pairs * LANES, QUERY_TILE), lambda i, h, j: (i, h, j)),
        out_shape=jax.ShapeDtypeStruct((b, FOX_W, s), BF16),
        scratch_shapes=_attention_scratch(pairs),
        compiler_params=_params(("parallel", "parallel", "arbitrary")),
        name="fox_attention",
    )(q_t, proj, v_t, gates)


def _even_mixer(h, w_in, sinks, rel_bias):
    kv0 = 3 * SB_W + SWA_QW
    dup = lambda w: jnp.repeat(w.reshape(D_MODEL, N_SWA_KV, HEAD_DIM), 2, axis=1).reshape(D_MODEL, -1)
    w_cat = jnp.concatenate(
        [w_in[:, SB_W:2 * SB_W], w_in[:, 3 * SB_W:kv0],
         dup(w_in[:, kv0:kv0 + SWA_KVW]), dup(w_in[:, kv0 + SWA_KVW:])], axis=1)
    proj, q_t, v_t = _proj(h, w_cat.astype(BF16), w_in[:, :SB_W].T.astype(BF16),
                           w_in[:, 2 * SB_W:3 * SB_W].T.astype(BF16), -QK_SCALE * LOG2E)
    o_sb = _sb_attention(proj, q_t, v_t)
    o_sw = _swa_attention(proj, _swa_bias(rel_bias), sinks.astype(F32))
    return o_sb, o_sw


def _odd_mixer(h, w_in, b_f):
    spread = lambda v: jnp.pad(jnp.tile(v, (1, GATE_PARTS)), ((0, 0), (0, LANES - GATE_PARTS * N_FOX_HEADS)))
    w_cat = jnp.concatenate([w_in[:, FOX_W:2 * FOX_W], spread(w_in[:, 3 * FOX_W:])], axis=1)
    proj, q_t, v_t, f = _proj(h, w_cat.astype(BF16), w_in[:, :FOX_W].T.astype(BF16),
                              w_in[:, 2 * FOX_W:3 * FOX_W].T.astype(BF16), QK_SCALE * LOG2E, tail=LANES)
    gates = _fox_gates(f, spread(b_f.reshape(1, -1).astype(F32)))
    return _fox_attention(proj, q_t, v_t, gates), None


def kernel(x, ln_g, ln_b, ffn1_in, ffn1_out, ffn2_in, ffn2_out, ab_w_in, ab_w_out, ab_sinks,
           fox_w_in, fox_b_f, fox_w_out, rel_bias):
    bsz, s, d = x.shape
    assert (s, d) == (SEQ, D_MODEL) and ln_g.shape[0] == DEPTH
    row = lambda v: v.reshape(1, D_MODEL).astype(F32)
    h = x
    for layer in range(DEPTH):
        ln = [(row(ln_g[layer, k]), row(ln_b[layer, k])) for k in range(3)]
        h = _ffn_ln(h.reshape(bsz * s, d), ffn1_in[layer].astype(BF16), ffn1_out[layer].astype(BF16),
                    *ln[0]).reshape(bsz, s, d)
        i = layer // 2
        if layer % 2 == 0:
            o_t, o = _even_mixer(h, ab_w_in[i], ab_sinks[i], rel_bias)
            w_mix = ab_w_out[i]
        else:
            o_t, o = _odd_mixer(h, fox_w_in[i], fox_b_f[i])
            w_mix = fox_w_out[i]
        h = _mix_ffn_ln(o_t, o, h, w_mix.astype(BF16), ln[1], ffn2_in[layer].astype(BF16),
                        ffn2_out[layer].astype(BF16), ln[2])
    return h
```

```python
import functools
import math

import numpy as np
import jax
import jax.numpy as jnp
from jax import lax
from jax.experimental import pallas as pl
from jax.experimental.pallas import tpu as pltpu

D_MODEL = 1024
SEQ = 2048
DEPTH = 2
HEAD_DIM = 64
LANES = 128
N_SB_PAIRS = 4
N_SWA_HEADS = 8
N_SWA_KV = 2
N_FOX_HEADS = 16
N_FOX_PAIRS = 8
WINDOW = 128
N_BUCKETS = 32
MAX_DISTANCE = 128
D_FF = 2816
ALPHA = (2 * DEPTH) ** 0.25
LN_EPS = 1e-5
SB_W = 512
SWA_QW = 512
SWA_KVW = 128
FOX_W = 1024
NEG_BIG = -1e30
LOG2E = math.log2(math.e)
QK_SCALE = HEAD_DIM ** -0.5
GATE_PARTS = 3
KEY_BLOCK = 256
QUERY_TILE = 2 * KEY_BLOCK
VMEM_LIMIT = 56 * 1024 * 1024

BF16 = jnp.bfloat16
F32 = jnp.float32


def _dot(a, b):
    return jnp.dot(a, b, preferred_element_type=F32)


def _dot_nt(a, b):
    return lax.dot_general(a, b, (((1,), (1,)), ((), ())), preferred_element_type=F32)


def _dot_tn(a, b):
    return lax.dot_general(a, b, (((0,), (0,)), ((), ())), preferred_element_type=F32)


def _layer_norm(z, g, b):
    mu = jnp.mean(z, axis=-1, keepdims=True)
    d = z - mu
    var = jnp.mean(d * d, axis=-1, keepdims=True)
    return d * lax.rsqrt(var + LN_EPS) * g + b


def _log2_sigmoid(x2):
    sign = jnp.uint32(0x80000000)
    neg_abs = lax.bitcast_convert_type(lax.bitcast_convert_type(x2, jnp.uint32) | sign, F32)
    return jnp.minimum(x2, 0.0) - jnp.log2(1.0 + jnp.exp2(neg_abs))


def _split_bf16(x, parts):
    out = []
    r = x
    for i in range(parts):
        t = r.astype(BF16)
        out.append(t)
        if i + 1 < parts:
            r = r - t.astype(F32)
    return out


def _params(sem):
    return pltpu.CompilerParams(dimension_semantics=sem, vmem_limit_bytes=VMEM_LIMIT)


def _resident(shape, index_map):
    return pl.BlockSpec(shape, index_map, pipeline_mode=pl.Buffered(1))


def _swiglu_ln(x, wg_ref, wu_ref, wo_ref, g_ref, b_ref):
    xb = x.astype(BF16)
    gate = _dot(xb, wg_ref[...])
    up = _dot(xb, wu_ref[...])
    h = (gate * jax.nn.sigmoid(gate) * up).astype(BF16)
    y = _dot(h, wo_ref[...])
    return _layer_norm(ALPHA * x + 0.5 * y, g_ref[...], b_ref[...])


def _ffn_ln_kernel(x_ref, wg_ref, wu_ref, wo_ref, g_ref, b_ref, o_ref):
    half = x_ref.shape[0] // 2
    for r in range(2):
        rows = slice(r * half, (r + 1) * half)
        o_ref[rows, :] = _swiglu_ln(x_ref[rows, :], wg_ref, wu_ref, wo_ref, g_ref, b_ref)


def _mix_ffn_ln_kernel(*refs, n_normal):
    ot_ref, refs = refs[0], refs[1:]
    o_refs, refs = refs[:n_normal], refs[n_normal:]
    h_ref, wt_ref, refs = refs[0], refs[1], refs[2:]
    w_refs, refs = refs[:n_normal], refs[n_normal:]
    g1_ref, b1_ref, wg_ref, wu_ref, wo_ref, g2_ref, b2_ref, out_ref = refs
    mix = _dot_tn(ot_ref[...], wt_ref[...])
    for o_ref, w_ref in zip(o_refs, w_refs):
        mix = mix + _dot(o_ref[...], w_ref[...])
    x = _layer_norm(ALPHA * h_ref[...] + mix, g1_ref[...], b1_ref[...])
    out_ref[...] = _swiglu_ln(x, wg_ref, wu_ref, wo_ref, g2_ref, b2_ref)


def _ffn_ln(x, w_in, w_out, g, b, tm=512):
    n = x.shape[0]
    return pl.pallas_call(
        _ffn_ln_kernel,
        grid=(n // tm,),
        in_specs=[
            pl.BlockSpec((tm, D_MODEL), lambda i: (i, 0)),
            _resident((D_MODEL, D_FF), lambda i: (0, 0)),
            _resident((D_MODEL, D_FF), lambda i: (0, 1)),
            _resident((D_FF, D_MODEL), lambda i: (0, 0)),
            _resident((1, D_MODEL), lambda i: (0, 0)),
            _resident((1, D_MODEL), lambda i: (0, 0)),
        ],
        out_specs=pl.BlockSpec((tm, D_MODEL), lambda i: (i, 0)),
        out_shape=jax.ShapeDtypeStruct((n, D_MODEL), F32),
        compiler_params=_params(("parallel",)),
        name="ffn_ln",
    )(x, w_in, w_in, w_out, g, b)


def _proj_kernel(x_ref, w_ref, wt_ref, o_ref, qt_ref, vt_ref, *tail_ref, q_scale):
    xb = x_ref[...].astype(BF16)
    width = o_ref.shape[-1]
    acc = _dot(xb, w_ref[...])
    o_ref[...] = acc[:, :width].astype(o_ref.dtype)
    for ref in tail_ref:
        ref[...] = acc[:, width:]
    acc_t = _dot_nt(wt_ref[...], xb)
    qw = qt_ref.shape[0]
    qt_ref[...] = (acc_t[:qw] * q_scale).astype(qt_ref.dtype)
    for j in range(vt_ref.shape[0]):
        vt_ref[j] = acc_t[qw:, j * KEY_BLOCK:(j + 1) * KEY_BLOCK].astype(vt_ref.dtype)


def _proj(x, w, wq_t, wv_t, q_scale, tail=0, tm=512):
    b, s, _ = x.shape
    width, qw, vw = w.shape[1] - tail, wq_t.shape[0], wv_t.shape[0]
    tok = lambda n: pl.BlockSpec((None, tm, n), lambda i, j: (i, j, 0))
    out_specs = [tok(width), pl.BlockSpec((None, qw, tm), lambda i, j: (i, 0, j)),
                 pl.BlockSpec((None, tm // KEY_BLOCK, vw, KEY_BLOCK), lambda i, j: (i, j, 0, 0))]
    out_shape = [jax.ShapeDtypeStruct((b, s, width), BF16), jax.ShapeDtypeStruct((b, qw, s), BF16),
                 jax.ShapeDtypeStruct((b, s // KEY_BLOCK, vw, KEY_BLOCK), BF16)]
    if tail:
        out_specs.append(tok(tail))
        out_shape.append(jax.ShapeDtypeStruct((b, s, tail), F32))
    return pl.pallas_call(
        functools.partial(_proj_kernel, q_scale=q_scale),
        grid=(b, s // tm),
        in_specs=[
            tok(D_MODEL),
            _resident((D_MODEL, width + tail), lambda i, j: (0, 0)),
            _resident((qw + vw, D_MODEL), lambda i, j: (0, 0)),
        ],
        out_specs=out_specs,
        out_shape=out_shape,
        compiler_params=_params(("parallel", "parallel")),
        name="in_proj",
    )(x, w, jnp.concatenate([wq_t, wv_t], axis=0))


def _mix_ffn_ln(o_t, o, h, w_mix, ln1, w_in, w_out, ln2, tm=512):
    bsz, s, _ = h.shape
    wt = o_t.shape[1]
    const = lambda i, j: (0, 0)
    tok = lambda width: pl.BlockSpec((None, tm, width), lambda i, j: (i, j, 0))
    row = _resident((1, D_MODEL), const)
    normal = [] if o is None else [o]
    in_specs = ([pl.BlockSpec((None, wt, tm), lambda i, j: (i, 0, j))] + [tok(a.shape[2]) for a in normal]
                + [tok(D_MODEL), _resident((wt, D_MODEL), const)]
                + [_resident((a.shape[2], D_MODEL), lambda i, j: (wt // a.shape[2], 0)) for a in normal]
                + [row, row, _resident((D_MODEL, D_FF), const), _resident((D_MODEL, D_FF), lambda i, j: (0, 1)),
                   _resident((D_FF, D_MODEL), const), row, row])
    args = [o_t] + normal + [h, w_mix] + [w_mix for _ in normal] + [*ln1, w_in, w_in, w_out, *ln2]
    return pl.pallas_call(
        functools.partial(_mix_ffn_ln_kernel, n_normal=len(normal)),
        grid=(bsz, s // tm),
        in_specs=in_specs,
        out_specs=tok(D_MODEL),
        out_shape=jax.ShapeDtypeStruct((bsz, s, D_MODEL), F32),
        compiler_params=_params(("parallel", "parallel")),
        name="mix_ffn_ln",
    )(*args)


def _head_pair(q):
    lo = lax.broadcasted_iota(jnp.int32, q.shape, 1) < HEAD_DIM
    zero = jnp.zeros_like(q)
    return jnp.where(lo, q, zero), jnp.where(lo, zero, q)


def _merge_pair(a0, a1):
    lo = lax.broadcasted_iota(jnp.int32, a0.shape, 1) < HEAD_DIM
    return jnp.where(lo, a0, a1)


def _head_operands_t(q_t):
    top = lax.broadcasted_iota(jnp.int32, (LANES, q_t.shape[1]), 0) < HEAD_DIM
    out = []
    for p in range(q_t.shape[0] // LANES):
        tile = q_t[p * LANES:(p + 1) * LANES]
        zero = jnp.zeros_like(tile)
        out += [jnp.where(top, tile, zero), jnp.where(top, zero, tile)]
    return out


def _key_rows(k_ref, kb, h):
    start = pl.multiple_of(kb * KEY_BLOCK, KEY_BLOCK)
    return k_ref[pl.ds(start, KEY_BLOCK), (h // 2) * LANES:(h // 2 + 1) * LANES]


def _value_rows_t(vt_ref, kb, h):
    return vt_ref[kb, h * HEAD_DIM:(h + 1) * HEAD_DIM, :]


ALL = slice(None)
LATE = slice(KEY_BLOCK, QUERY_TILE)


def _diagonal_masks(strict):
    row = lax.broadcasted_iota(jnp.int32, (KEY_BLOCK, QUERY_TILE), 0)
    col = lax.broadcasted_iota(jnp.int32, (KEY_BLOCK, QUERY_TILE), 1)
    full = row < col if strict else row <= col
    return full, full[:, :KEY_BLOCK]


def _widen(row_vec, late_part):
    return jnp.concatenate([row_vec[:, :KEY_BLOCK], late_part], axis=1)


def _attention_scratch(pairs):
    tile = (2 * pairs, KEY_BLOCK, QUERY_TILE)
    return [pltpu.VMEM(tile, F32), pltpu.VMEM(tile, F32),
            pltpu.VMEM(tile, F32), pltpu.VMEM(tile, F32),
            pltpu.VMEM((2 * pairs, HEAD_DIM, QUERY_TILE), F32)]


def _sb_kernel(qt_ref, k_ref, vt_ref, o_ref, nza_ref, nzb_ref, wa_ref, wb_ref, acc_ref):
    qi = pl.program_id(2)
    w_heads = _head_operands_t(qt_ref[...])
    heads = range(len(w_heads))
    masks = _diagonal_masks(strict=True)
    row = lax.broadcasted_iota(jnp.int32, (KEY_BLOCK, KEY_BLOCK), 0)
    col = lax.broadcasted_iota(jnp.int32, (KEY_BLOCK, KEY_BLOCK), 1)
    suffix = jnp.where(col >= row, 1.0, 0.0).astype(BF16)
    last = 2 * qi + 1

    def scores(kb, nz_ref, cols=ALL):
        for h in heads:
            nz_ref[h, :, cols] = _dot(_key_rows(k_ref, kb, h), w_heads[h][:, cols])

    def accumulate(kb, w_ref, cols=ALL):
        for h in heads:
            acc_ref[h, :, cols] += _dot(_value_rows_t(vt_ref, kb, h), w_ref[h, :, cols].astype(BF16))

    def weights(nz_ref, w_ref, carries, mask, cols=ALL):
        incl = []
        for h in heads:
            log_keep = _log2_sigmoid(nz_ref[h, :, cols])
            if mask is not None:
                log_keep = jnp.where(mask, log_keep, 0.0)
            incl.append(_dot(suffix, log_keep.astype(BF16)))
        for h in heads:
            w = jnp.exp2(incl[h] + carries[h][:, cols] - nz_ref[h, :, cols])
            if mask is not None:
                w = jnp.where(mask, w, 0.0)
            w_ref[h, :, cols] = w
        totals = [carries[h][:, cols] + incl[h][0:1, :] for h in heads]
        return totals if cols is ALL else [_widen(carries[h], totals[h]) for h in heads]

    def half_step(j, cur, oth, carries, *, mask=None, cols=ALL, prev_cols=ALL, first=False):
        kb = last - j
        scores(jnp.maximum(kb - 1, 0), oth[0])
        if not first:
            accumulate(kb + 1, oth[1], prev_cols)
        return weights(cur[0], cur[1], carries, mask, cols)

    slot_a, slot_b = (nza_ref, wa_ref), (nzb_ref, wb_ref)
    acc_ref[...] = jnp.zeros_like(acc_ref)
    scores(last, nza_ref, LATE)
    carries = [jnp.zeros((1, QUERY_TILE), F32) for _ in heads]
    carries = half_step(0, slot_a, slot_b, carries, mask=masks[1], cols=LATE, first=True)
    carries = half_step(1, slot_b, slot_a, carries, mask=masks[0], prev_cols=LATE)

    def body(i, carries):
        carries = half_step(2 * i + 2, slot_a, slot_b, carries)
        return half_step(2 * i + 3, slot_b, slot_a, carries)

    lax.fori_loop(0, qi, body, carries)
    accumulate(0, wb_ref)
    for h in heads:
        o_ref[h * HEAD_DIM:(h + 1) * HEAD_DIM, :] = acc_ref[h].astype(o_ref.dtype)


def _sb_attention(proj, q_t, v_t, pairs=2):
    b, s, _ = proj.shape
    groups = N_SB_PAIRS // pairs
    return pl.pallas_call(
        _sb_kernel,
        grid=(b, groups, s // QUERY_TILE),
        in_specs=[
            pl.BlockSpec((None, pairs * LANES, QUERY_TILE), lambda i, h, j: (i, h, j)),
            pl.BlockSpec((None, s, pairs * LANES), lambda i, h, j: (i, 0, h)),
            pl.BlockSpec((None, s // KEY_BLOCK, pairs * LANES, KEY_BLOCK), lambda i, h, j: (i, 0, h, 0)),
        ],
        out_specs=pl.BlockSpec((None, pairs * LANES, QUERY_TILE), lambda i, h, j: (i, h, j)),
        out_shape=jax.ShapeDtypeStruct((b, SB_W, s), BF16),
        scratch_shapes=_attention_scratch(pairs),
        compiler_params=_params(("parallel", "parallel", "arbitrary")),
        name="sb_attention",
    )(q_t, proj, v_t)


def _t5_bucket_table():
    n = np.arange(WINDOW)
    max_exact = N_BUCKETS // 2
    nf = np.maximum(n, 1).astype(np.float64)
    val = np.log(nf / max_exact) / math.log(MAX_DISTANCE / max_exact) * (N_BUCKETS - max_exact)
    frac = np.abs(val - np.round(val))[n > max_exact]
    assert frac.min() > 1e-3, "bucket boundary too close to an integer distance for float32"
    large = np.minimum(max_exact + val.astype(np.int64), N_BUCKETS - 1)
    return np.where(n < max_exact, n, large)


def _swa_bucket_tiles():
    table = _t5_bucket_table()
    qi = np.arange(WINDOW)[:, None]
    kj = np.arange(2 * WINDOW)[None, :]
    tiles = []
    for offset in (WINDOW, 0):
        rel = qi + offset - kj
        valid = (rel >= 0) & (rel < WINDOW)
        tiles.append(np.where(valid, table[np.clip(rel, 0, WINDOW - 1)], -1))
    return np.stack(tiles).astype(np.int32)


def _swa_bias_kernel(rb_ref, idx_ref, o_ref):
    h = pl.program_id(0)
    for tile in range(2):
        idx = idx_ref[tile]
        acc = jnp.full(idx.shape, NEG_BIG, F32)
        for bucket in range(N_BUCKETS):
            acc = jnp.where(idx == bucket, rb_ref[bucket, h], acc)
        o_ref[tile, 0] = acc


def _swa_bias(rel_bias):
    idx = jnp.asarray(_swa_bucket_tiles())
    return pl.pallas_call(
        _swa_bias_kernel,
        grid=(N_SWA_HEADS,),
        in_specs=[
            pl.BlockSpec(memory_space=pltpu.SMEM),
            pl.BlockSpec((2, WINDOW, 2 * WINDOW), lambda h: (0, 0, 0)),
        ],
        out_specs=pl.BlockSpec((2, 1, WINDOW, 2 * WINDOW), lambda h: (0, h, 0, 0)),
        out_shape=jax.ShapeDtypeStruct((2, N_SWA_HEADS, WINDOW, 2 * WINDOW), F32),
        compiler_params=_params(("arbitrary",)),
        name="swa_bias",
    )(rel_bias, idx)


def _swa_kernel(sink_ref, q_ref, k_ref, v_ref, bias_ref, o_ref, *, tq):
    qt = pl.program_id(1)
    scale = jnp.asarray(QK_SCALE, BF16)
    group = N_SWA_HEADS // N_SWA_KV
    for j in range(tq // WINDOW):
        nb = qt * (tq // WINDOW) + j
        first = (nb == 0).astype(jnp.int32) if j == 0 else 0
        w0 = pl.multiple_of(jnp.maximum(nb - 1, 0) * WINDOW, WINDOW)
        rows = slice(j * WINDOW, (j + 1) * WINDOW)
        for g in range(N_SWA_KV):
            kwin = k_ref[pl.ds(w0, 2 * WINDOW), g * LANES:(g + 1) * LANES]
            vwin = v_ref[pl.ds(w0, 2 * WINDOW), g * LANES:(g + 1) * LANES]
            for c in range(group // 2):
                pair = g * (group // 2) + c
                cols = slice(pair * LANES, (pair + 1) * LANES)
                q_heads = _head_pair(q_ref[rows, cols] * scale)
                outs = []
                for i in range(2):
                    head = 2 * pair + i
                    sink = sink_ref[head]
                    logits = _dot_nt(q_heads[i], kwin) + bias_ref[first, head]
                    m = jnp.maximum(jnp.max(logits, axis=-1, keepdims=True), sink)
                    p = jnp.exp(logits - m)
                    denom = jnp.sum(p, axis=-1, keepdims=True) + jnp.exp(sink - m)
                    outs.append(_dot(p.astype(BF16), vwin) / denom)
                o_ref[rows, cols] = _merge_pair(outs[0], outs[1]).astype(o_ref.dtype)


def _swa_attention(proj, bias, sinks, tq=512):
    b, s, width = proj.shape
    q_blk = (width - SWA_QW - 4 * LANES) // SWA_QW
    k_blk = (width - 4 * LANES) // (2 * LANES)
    return pl.pallas_call(
        functools.partial(_swa_kernel, tq=tq),
        grid=(b, s // tq),
        in_specs=[
            pl.BlockSpec(memory_space=pltpu.SMEM),
            pl.BlockSpec((None, tq, SWA_QW), lambda i, j: (i, j, q_blk)),
            pl.BlockSpec((None, s, 2 * LANES), lambda i, j: (i, 0, k_blk)),
            pl.BlockSpec((None, s, 2 * LANES), lambda i, j: (i, 0, k_blk + 1)),
            _resident((2, N_SWA_HEADS, WINDOW, 2 * WINDOW), lambda i, j: (0, 0, 0, 0)),
        ],
        out_specs=pl.BlockSpec((None, tq, SWA_QW), lambda i, j: (i, j, 0)),
        out_shape=jax.ShapeDtypeStruct((b, s, SWA_QW), BF16),
        compiler_params=_params(("parallel", "arbitrary")),
        name="swa_attention",
    )(sinks, proj, proj, proj, bias)


def _fox_gate_kernel(f_ref, bf_ref, c_ref, *, chunk):
    s = f_ref.shape[0]
    r = lax.broadcasted_iota(jnp.int32, (chunk, chunk), 0)
    c = lax.broadcasted_iota(jnp.int32, (chunk, chunk), 1)
    upto = jnp.where(c <= r, 1.0, 0.0).astype(BF16)
    lane = lax.broadcasted_iota(jnp.int32, (chunk, LANES), 1)
    carry = jnp.zeros((1, LANES), F32)
    for j in range(s // chunk):
        rows = slice(j * chunk, (j + 1) * chunk)
        log_f = _log2_sigmoid((f_ref[rows, :] + bf_ref[...]) * LOG2E)
        cs = carry
        for part in _split_bf16(log_f, 3):
            cs = cs + _dot(upto, part)
        carry = cs[chunk - 1:chunk, :]
        out = jnp.zeros((chunk, LANES), BF16)
        for k, part in reversed(list(enumerate(_split_bf16(cs, GATE_PARTS)))):
            out = jnp.where(lane < (k + 1) * N_FOX_HEADS, part, out)
        c_ref[rows, :] = out


def _fox_gates(f, b_f, chunk=256):
    b, s, _ = f.shape
    return pl.pallas_call(
        functools.partial(_fox_gate_kernel, chunk=chunk),
        grid=(b,),
        in_specs=[
            pl.BlockSpec((None, s, LANES), lambda i: (i, 0, 0)),
            _resident((1, LANES), lambda i: (0, 0)),
        ],
        out_specs=pl.BlockSpec((None, s, LANES), lambda i: (i, 0, 0)),
        out_shape=jax.ShapeDtypeStruct((b, s, LANES), BF16),
        compiler_params=_params(("parallel",)),
        name="fox_gates",
    )(f, b_f)


def _fox_kernel(qt_ref, k_ref, vt_ref, c_ref, o_ref, sa_ref, sb_ref, pa_ref, pb_ref, acc_ref):
    qi = pl.program_id(2)
    q_heads = _head_operands_t(qt_ref[...])
    heads = range(len(q_heads))
    first_head = pl.program_id(1) * len(q_heads)
    gate_row = lax.broadcasted_iota(jnp.int32, (LANES, QUERY_TILE), 0)
    masks = _diagonal_masks(strict=False)
    w_heads = []
    for h in heads:
        picks = ((gate_row & (N_FOX_HEADS - 1)) == first_head + h) & (gate_row < GATE_PARTS * N_FOX_HEADS)
        minus_gate = jnp.where(picks, -1.0, 0.0).astype(BF16)
        w_heads.append(jnp.concatenate([q_heads[h], minus_gate], axis=0))

    def scores(kb, s_ref, cols=ALL):
        start = pl.multiple_of(kb * KEY_BLOCK, KEY_BLOCK)
        gates = c_ref[pl.ds(start, KEY_BLOCK), :]
        for h in heads:
            lhs = jnp.concatenate([_key_rows(k_ref, kb, h), gates], axis=1)
            s_ref[h, :, cols] = _dot(lhs, w_heads[h][:, cols])

    def accumulate(kb, p_ref, a, cols=ALL):
        for h in heads:
            pv = _dot(_value_rows_t(vt_ref, kb, h), p_ref[h, :, cols].astype(BF16))
            acc_ref[h, :, cols] = a[h] * acc_ref[h, :, cols] + pv

    def softmax(s_ref, p_ref, stats, mask, cols=ALL):
        new_stats, a = [], []
        for h in heads:
            m, l = stats[h][0][:, cols], stats[h][1][:, cols]
            s2 = s_ref[h, :, cols] if mask is None else jnp.where(mask, s_ref[h, :, cols], NEG_BIG)
            m_new = jnp.maximum(m, jnp.max(s2, axis=0, keepdims=True))
            ah = jnp.exp2(m - m_new)
            p = jnp.exp2(s2 - m_new)
            p_ref[h, :, cols] = p
            l_new = ah * l + jnp.sum(p, axis=0, keepdims=True)
            if cols is not ALL:
                m_new, l_new = _widen(stats[h][0], m_new), _widen(stats[h][1], l_new)
            new_stats.append((m_new, l_new))
            a.append(ah)
        return new_stats, a

    def half_step(kb, cur, oth, stats, a_prev, *, mask=None, cols=ALL, next_cols=ALL):
        if next_cols is not None:
            scores(kb + 1, oth[0], next_cols)
        accumulate(jnp.maximum(kb - 1, 0), oth[1], a_prev)
        return softmax(cur[0], cur[1], stats, mask, cols)

    slot_a, slot_b = (sa_ref, pa_ref), (sb_ref, pb_ref)
    pb_ref[...] = jnp.zeros_like(pb_ref)
    acc_ref[...] = jnp.zeros_like(acc_ref)
    scores(0, sa_ref)
    stats = [(jnp.full((1, QUERY_TILE), NEG_BIG, F32), jnp.zeros((1, QUERY_TILE), F32)) for _ in heads]
    ones = [jnp.ones((1, QUERY_TILE), F32) for _ in heads]

    def body(i, carry):
        stats, a = carry
        stats, a = half_step(2 * i, slot_a, slot_b, stats, a)
        return half_step(2 * i + 1, slot_b, slot_a, stats, a)

    stats, a = lax.fori_loop(0, qi, body, (stats, ones))
    stats, a = half_step(2 * qi, slot_a, slot_b, stats, a, mask=masks[0], next_cols=LATE)
    stats, a = half_step(2 * qi + 1, slot_b, slot_a, stats, a, mask=masks[1], cols=LATE, next_cols=None)
    accumulate(2 * qi + 1, pb_ref, a, LATE)
    for h in heads:
        o_ref[h * HEAD_DIM:(h + 1) * HEAD_DIM, :] = (acc_ref[h] / stats[h][1]).astype(o_ref.dtype)


def _fox_attention(proj, q_t, v_t, gates, pairs=4):
    b, s, _ = proj.shape
    groups = N_FOX_PAIRS // pairs
    return pl.pallas_call(
        _fox_kernel,
        grid=(b, groups, s // QUERY_TILE),
        in_specs=[
            pl.BlockSpec((None, pairs * LANES, QUERY_TILE), lambda i, h, j: (i, h, j)),
            pl.BlockSpec((None, s, pairs * LANES), lambda i, h, j: (i, 0, h)),
            pl.BlockSpec((None, s // KEY_BLOCK, pairs * LANES, KEY_BLOCK), lambda i, h, j: (i, 0, h, 0)),
            pl.BlockSpec((None, s, LANES), lambda i, h, j: (i, 0, 0)),
        ],
        out_specs=pl.BlockSpec((None, pairs * LANES, QUERY_TILE), lambda i, h, j: (i, h, j)),
        out_shape=jax.ShapeDtypeStruct((b, FOX_W, s), BF16),
        scratch_shapes=_attention_scratch(pairs),
        compiler_params=_params(("parallel", "parallel", "arbitrary")),
        name="fox_attention",
    )(q_t, proj, v_t, gates)


def _even_mixer(h, w_in, sinks, rel_bias):
    kv0 = 3 * SB_W + SWA_QW
    dup = lambda w: jnp.repeat(w.reshape(D_MODEL, N_SWA_KV, HEAD_DIM), 2, axis=1).reshape(D_MODEL, -1)
    w_cat = jnp.concatenate(
        [w_in[:, SB_W:2 * SB_W], w_in[:, 3 * SB_W:kv0],
         dup(w_in[:, kv0:kv0 + SWA_KVW]), dup(w_in[:, kv0 + SWA_KVW:])], axis=1)
    proj, q_t, v_t = _proj(h, w_cat.astype(BF16), w_in[:, :SB_W].T.astype(BF16),
                           w_in[:, 2 * SB_W:3 * SB_W].T.astype(BF16), -QK_SCALE * LOG2E)
    o_sb = _sb_attention(proj, q_t, v_t)
    o_sw = _swa_attention(proj, _swa_bias(rel_bias), sinks.astype(F32))
    return o_sb, o_sw


def _odd_mixer(h, w_in, b_f):
    spread = lambda v: jnp.pad(jnp.tile(v, (1, GATE_PARTS)), ((0, 0), (0, LANES - GATE_PARTS * N_FOX_HEADS)))
    w_cat = jnp.concatenate([w_in[:, FOX_W:2 * FOX_W], spread(w_in[:, 3 * FOX_W:])], axis=1)
    proj, q_t, v_t, f = _proj(h, w_cat.astype(BF16), w_in[:, :FOX_W].T.astype(BF16),
                              w_in[:, 2 * FOX_W:3 * FOX_W].T.astype(BF16), QK_SCALE * LOG2E, tail=LANES)
    gates = _fox_gates(f, spread(b_f.reshape(1, -1).astype(F32)))
    return _fox_attention(proj, q_t, v_t, gates), None


def kernel(x, ln_g, ln_b, ffn1_in, ffn1_out, ffn2_in, ffn2_out, ab_w_in, ab_w_out, ab_sinks,
           fox_w_in, fox_b_f, fox_w_out, rel_bias):
    bsz, s, d = x.shape
    assert (s, d) == (SEQ, D_MODEL) and ln_g.shape[0] == DEPTH
    row = lambda v: v.reshape(1, D_MODEL).astype(F32)
    h = x
    for layer in range(DEPTH):
        ln = [(row(ln_g[layer, k]), row(ln_b[layer, k])) for k in range(3)]
        h = _ffn_ln(h.reshape(bsz * s, d), ffn1_in[layer].astype(BF16), ffn1_out[layer].astype(BF16),
                    *ln[0]).reshape(bsz, s, d)
        i = layer // 2
        if layer % 2 == 0:
            o_t, o = _even_mixer(h, ab_w_in[i], ab_sinks[i], rel_bias)
            w_mix = ab_w_out[i]
        else:
            o_t, o = _odd_mixer(h, fox_w_in[i], fox_b_f[i])
            w_mix = fox_w_out[i]
        h = _mix_ffn_ln(o_t, o, h, w_mix.astype(BF16), ln[1], ffn2_in[layer].astype(BF16),
                        ffn2_out[layer].astype(BF16), ln[2])
    return h
```

```python
import functools
import math

import numpy as np
import jax
import jax.numpy as jnp
from jax import lax
from jax.experimental import pallas as pl
from jax.experimental.pallas import tpu as pltpu

D_MODEL = 1024
SEQ = 2048
DEPTH = 2
HEAD_DIM = 64
LANES = 128
N_SB_PAIRS = 4
N_SWA_HEADS = 8
N_SWA_KV = 2
N_FOX_HEADS = 16
N_FOX_PAIRS = 8
WINDOW = 128
N_BUCKETS = 32
MAX_DISTANCE = 128
D_FF = 2816
ALPHA = (2 * DEPTH) ** 0.25
LN_EPS = 1e-5
SB_W = 512
SWA_QW = 512
SWA_KVW = 128
FOX_W = 1024
NEG_BIG = -1e30
LOG2E = math.log2(math.e)
QK_SCALE = HEAD_DIM ** -0.5
GATE_PARTS = 3
KEY_BLOCK = 256
QUERY_TILE = 2 * KEY_BLOCK
VMEM_LIMIT = 56 * 1024 * 1024

BF16 = jnp.bfloat16
F32 = jnp.float32


def _dot(a, b):
    return jnp.dot(a, b, preferred_element_type=F32)


def _dot_nt(a, b):
    return lax.dot_general(a, b, (((1,), (1,)), ((), ())), preferred_element_type=F32)


def _dot_tn(a, b):
    return lax.dot_general(a, b, (((0,), (0,)), ((), ())), preferred_element_type=F32)


def _layer_norm(z, g, b):
    mu = jnp.mean(z, axis=-1, keepdims=True)
    d = z - mu
    var = jnp.mean(d * d, axis=-1, keepdims=True)
    return d * lax.rsqrt(var + LN_EPS) * g + b


def _log2_sigmoid(x2):
    sign = jnp.uint32(0x80000000)
    neg_abs = lax.bitcast_convert_type(lax.bitcast_convert_type(x2, jnp.uint32) | sign, F32)
    return jnp.minimum(x2, 0.0) - jnp.log2(1.0 + jnp.exp2(neg_abs))


def _split_bf16(x, parts):
    out = []
    r = x
    for i in range(parts):
        t = r.astype(BF16)
        out.append(t)
        if i + 1 < parts:
            r = r - t.astype(F32)
    return out


def _params(sem):
    return pltpu.CompilerParams(dimension_semantics=sem, vmem_limit_bytes=VMEM_LIMIT)


def _resident(shape, index_map):
    return pl.BlockSpec(shape, index_map, pipeline_mode=pl.Buffered(1))


def _swiglu_ln(x, wg_ref, wu_ref, wo_ref, g_ref, b_ref):
    xb = x.astype(BF16)
    gate = _dot(xb, wg_ref[...])
    up = _dot(xb, wu_ref[...])
    h = (gate * jax.nn.sigmoid(gate) * up).astype(BF16)
    y = _dot(h, wo_ref[...])
    return _layer_norm(ALPHA * x + 0.5 * y, g_ref[...], b_ref[...])


def _ffn_ln_kernel(x_ref, wg_ref, wu_ref, wo_ref, g_ref, b_ref, o_ref):
    half = x_ref.shape[0] // 2
    for r in range(2):
        rows = slice(r * half, (r + 1) * half)
        o_ref[rows, :] = _swiglu_ln(x_ref[rows, :], wg_ref, wu_ref, wo_ref, g_ref, b_ref)


def _mix_ffn_ln_kernel(*refs, n_normal):
    ot_ref, refs = refs[0], refs[1:]
    o_refs, refs = refs[:n_normal], refs[n_normal:]
    h_ref, wt_ref, refs = refs[0], refs[1], refs[2:]
    w_refs, refs = refs[:n_normal], refs[n_normal:]
    g1_ref, b1_ref, wg_ref, wu_ref, wo_ref, g2_ref, b2_ref, out_ref = refs
    mix = _dot_tn(ot_ref[...], wt_ref[...])
    for o_ref, w_ref in zip(o_refs, w_refs):
        mix = mix + _dot(o_ref[...], w_ref[...])
    half = h_ref.shape[0] // 2
    xs = []
    for r in range(2):
        rows = slice(r * half, (r + 1) * half)
        xs.append(_layer_norm(ALPHA * h_ref[rows, :] + mix[rows], g1_ref[...], b1_ref[...]))
    for r in range(2):
        rows = slice(r * half, (r + 1) * half)
        out_ref[rows, :] = _swiglu_ln(xs[r], wg_ref, wu_ref, wo_ref, g2_ref, b2_ref)


def _ffn_ln(x, w_in, w_out, layer, g, b, tm=512):
    n = x.shape[0]
    return pl.pallas_call(
        _ffn_ln_kernel,
        grid=(n // tm,),
        in_specs=[
            pl.BlockSpec((tm, D_MODEL), lambda i: (i, 0)),
            _resident((None, D_MODEL, D_FF), lambda i: (layer, 0, 0)),
            _resident((None, D_MODEL, D_FF), lambda i: (layer, 0, 1)),
            _resident((None, D_FF, D_MODEL), lambda i: (layer, 0, 0)),
            _resident((1, D_MODEL), lambda i: (0, 0)),
            _resident((1, D_MODEL), lambda i: (0, 0)),
        ],
        out_specs=pl.BlockSpec((tm, D_MODEL), lambda i: (i, 0)),
        out_shape=jax.ShapeDtypeStruct((n, D_MODEL), F32),
        compiler_params=_params(("parallel",)),
        name="ffn_ln",
    )(x, w_in, w_in, w_out, g, b)


def _proj_kernel(x_ref, w_ref, wt_ref, o_ref, qt_ref, vt_ref, *tail_ref, q_scale):
    xb = x_ref[...].astype(BF16)
    width = o_ref.shape[-1]
    acc = _dot(xb, w_ref[...])
    o_ref[...] = acc[:, :width].astype(o_ref.dtype)
    for ref in tail_ref:
        ref[...] = acc[:, width:]
    acc_t = _dot_nt(wt_ref[...], xb)
    qw = qt_ref.shape[0]
    qt_ref[...] = (acc_t[:qw] * q_scale).astype(qt_ref.dtype)
    for j in range(vt_ref.shape[0]):
        vt_ref[j] = acc_t[qw:, j * KEY_BLOCK:(j + 1) * KEY_BLOCK].astype(vt_ref.dtype)


def _proj(x, w, wq_t, wv_t, q_scale, tail=0, tm=512):
    b, s, _ = x.shape
    width, qw, vw = w.shape[1] - tail, wq_t.shape[0], wv_t.shape[0]
    tok = lambda n: pl.BlockSpec((None, tm, n), lambda i, j: (i, j, 0))
    out_specs = [tok(width), pl.BlockSpec((None, qw, tm), lambda i, j: (i, 0, j)),
                 pl.BlockSpec((None, tm // KEY_BLOCK, vw, KEY_BLOCK), lambda i, j: (i, j, 0, 0))]
    out_shape = [jax.ShapeDtypeStruct((b, s, width), BF16), jax.ShapeDtypeStruct((b, qw, s), BF16),
                 jax.ShapeDtypeStruct((b, s // KEY_BLOCK, vw, KEY_BLOCK), BF16)]
    if tail:
        out_specs.append(tok(tail))
        out_shape.append(jax.ShapeDtypeStruct((b, s, tail), F32))
    return pl.pallas_call(
        functools.partial(_proj_kernel, q_scale=q_scale),
        grid=(b, s // tm),
        in_specs=[
            tok(D_MODEL),
            _resident((D_MODEL, width + tail), lambda i, j: (0, 0)),
            _resident((qw + vw, D_MODEL), lambda i, j: (0, 0)),
        ],
        out_specs=out_specs,
        out_shape=out_shape,
        compiler_params=_params(("parallel", "parallel")),
        name="in_proj",
    )(x, w, jnp.concatenate([wq_t, wv_t], axis=0))


def _mix_ffn_ln(o_t, o, h, w_mix, ln1, w_in, w_out, layer, ln2, tm=512):
    bsz, s, _ = h.shape
    wt = o_t.shape[1]
    const = lambda i, j: (0, 0)
    tok = lambda width: pl.BlockSpec((None, tm, width), lambda i, j: (i, j, 0))
    row = _resident((1, D_MODEL), const)
    normal = [] if o is None else [o]
    in_specs = ([pl.BlockSpec((None, wt, tm), lambda i, j: (i, 0, j))] + [tok(a.shape[2]) for a in normal]
                + [tok(D_MODEL), _resident((wt, D_MODEL), const)]
                + [_resident((a.shape[2], D_MODEL), lambda i, j: (wt // a.shape[2], 0)) for a in normal]
                + [row, row, _resident((None, D_MODEL, D_FF), lambda i, j: (layer, 0, 0)),
                   _resident((None, D_MODEL, D_FF), lambda i, j: (layer, 0, 1)),
                   _resident((None, D_FF, D_MODEL), lambda i, j: (layer, 0, 0)), row, row])
    args = [o_t] + normal + [h, w_mix] + [w_mix for _ in normal] + [*ln1, w_in, w_in, w_out, *ln2]
    return pl.pallas_call(
        functools.partial(_mix_ffn_ln_kernel, n_normal=len(normal)),
        grid=(bsz, s // tm),
        in_specs=in_specs,
        out_specs=tok(D_MODEL),
        out_shape=jax.ShapeDtypeStruct((bsz, s, D_MODEL), F32),
        compiler_params=_params(("parallel", "parallel")),
        name="mix_ffn_ln",
    )(*args)


def _head_pair(q):
    lo = lax.broadcasted_iota(jnp.int32, q.shape, 1) < HEAD_DIM
    zero = jnp.zeros_like(q)
    return jnp.where(lo, q, zero), jnp.where(lo, zero, q)


def _merge_pair(a0, a1):
    lo = lax.broadcasted_iota(jnp.int32, a0.shape, 1) < HEAD_DIM
    return jnp.where(lo, a0, a1)


def _head_operands_t(q_t):
    top = lax.broadcasted_iota(jnp.int32, (LANES, q_t.shape[1]), 0) < HEAD_DIM
    out = []
    for p in range(q_t.shape[0] // LANES):
        tile = q_t[p * LANES:(p + 1) * LANES]
        zero = jnp.zeros_like(tile)
        out += [jnp.where(top, tile, zero), jnp.where(top, zero, tile)]
    return out


def _key_rows(k_ref, kb, h):
    start = pl.multiple_of(kb * KEY_BLOCK, KEY_BLOCK)
    return k_ref[pl.ds(start, KEY_BLOCK), (h // 2) * LANES:(h // 2 + 1) * LANES]


def _value_rows_t(vt_ref, kb, h):
    return vt_ref[kb, h * HEAD_DIM:(h + 1) * HEAD_DIM, :]


ALL = slice(None)
LATE = slice(KEY_BLOCK, QUERY_TILE)


def _diagonal_masks(strict):
    row = lax.broadcasted_iota(jnp.int32, (KEY_BLOCK, QUERY_TILE), 0)
    col = lax.broadcasted_iota(jnp.int32, (KEY_BLOCK, QUERY_TILE), 1)
    full = row < col if strict else row <= col
    return full, full[:, :KEY_BLOCK]


def _widen(row_vec, late_part):
    return jnp.concatenate([row_vec[:, :KEY_BLOCK], late_part], axis=1)


def _attention_scratch(pairs):
    tile = (2 * pairs, KEY_BLOCK, QUERY_TILE)
    packed = (2 * pairs, KEY_BLOCK // 2, QUERY_TILE)
    return [pltpu.VMEM(tile, F32), pltpu.VMEM(tile, F32),
            pltpu.VMEM(packed, jnp.uint32), pltpu.VMEM(packed, jnp.uint32),
            pltpu.VMEM((2 * pairs, HEAD_DIM, QUERY_TILE), F32)]


def _sb_kernel(qt_ref, k_ref, vt_ref, o_ref, nza_ref, nzb_ref, wa_ref, wb_ref, acc_ref):
    qi = pl.program_id(2)
    w_heads = _head_operands_t(qt_ref[...])
    heads = range(len(w_heads))
    masks = _diagonal_masks(strict=True)
    row = lax.broadcasted_iota(jnp.int32, (KEY_BLOCK, KEY_BLOCK), 0)
    col = lax.broadcasted_iota(jnp.int32, (KEY_BLOCK, KEY_BLOCK), 1)
    suffix = jnp.where(col >= row, 1.0, 0.0).astype(BF16)
    last = 2 * qi + 1

    def scores(kb, nz_ref, cols=ALL):
        for h in heads:
            nz_ref[h, :, cols] = _dot(_key_rows(k_ref, kb, h), w_heads[h][:, cols])

    def accumulate(kb, w_ref, cols=ALL):
        for h in heads:
            acc_ref[h, :, cols] += _dot(_value_rows_t(vt_ref, kb, h), pltpu.bitcast(w_ref[h, :, cols], BF16))

    def weights(nz_ref, w_ref, carries, mask, cols=ALL):
        incl = []
        for h in heads:
            log_keep = _log2_sigmoid(nz_ref[h, :, cols])
            if mask is not None:
                log_keep = jnp.where(mask, log_keep, 0.0)
            incl.append(_dot(suffix, log_keep.astype(BF16)))
        for h in heads:
            w = jnp.exp2(incl[h] + carries[h][:, cols] - nz_ref[h, :, cols])
            if mask is not None:
                w = jnp.where(mask, w, 0.0)
            w_ref[h, :, cols] = pltpu.bitcast(w.astype(BF16), jnp.uint32)
        totals = [carries[h][:, cols] + incl[h][0:1, :] for h in heads]
        return totals if cols is ALL else [_widen(carries[h], totals[h]) for h in heads]

    def half_step(j, cur, oth, carries, *, mask=None, cols=ALL, prev_cols=ALL, first=False):
        kb = last - j
        scores(jnp.maximum(kb - 1, 0), oth[0])
        if not first:
            accumulate(kb + 1, oth[1], prev_cols)
        return weights(cur[0], cur[1], carries, mask, cols)

    slot_a, slot_b = (nza_ref, wa_ref), (nzb_ref, wb_ref)
    acc_ref[...] = jnp.zeros_like(acc_ref)
    scores(last, nza_ref, LATE)
    carries = [jnp.zeros((1, QUERY_TILE), F32) for _ in heads]
    carries = half_step(0, slot_a, slot_b, carries, mask=masks[1], cols=LATE, first=True)
    carries = half_step(1, slot_b, slot_a, carries, mask=masks[0], prev_cols=LATE)

    def body(i, carries):
        carries = half_step(2 * i + 2, slot_a, slot_b, carries)
        return half_step(2 * i + 3, slot_b, slot_a, carries)

    lax.fori_loop(0, qi, body, carries)
    accumulate(0, wb_ref)
    for h in heads:
        o_ref[h * HEAD_DIM:(h + 1) * HEAD_DIM, :] = acc_ref[h].astype(o_ref.dtype)


def _sb_attention(proj, q_t, v_t, pairs=2):
    b, s, _ = proj.shape
    groups = N_SB_PAIRS // pairs
    return pl.pallas_call(
        _sb_kernel,
        grid=(b, groups, s // QUERY_TILE),
        in_specs=[
            pl.BlockSpec((None, pairs * LANES, QUERY_TILE), lambda i, h, j: (i, h, j)),
            pl.BlockSpec((None, s, pairs * LANES), lambda i, h, j: (i, 0, h)),
            pl.BlockSpec((None, s // KEY_BLOCK, pairs * LANES, KEY_BLOCK), lambda i, h, j: (i, 0, h, 0)),
        ],
        out_specs=pl.BlockSpec((None, pairs * LANES, QUERY_TILE), lambda i, h, j: (i, h, j)),
        out_shape=jax.ShapeDtypeStruct((b, SB_W, s), BF16),
        scratch_shapes=_attention_scratch(pairs),
        compiler_params=_params(("parallel", "parallel", "arbitrary")),
        name="sb_attention",
    )(q_t, proj, v_t)


def _t5_bucket_table():
    n = np.arange(WINDOW)
    max_exact = N_BUCKETS // 2
    nf = np.maximum(n, 1).astype(np.float64)
    val = np.log(nf / max_exact) / math.log(MAX_DISTANCE / max_exact) * (N_BUCKETS - max_exact)
    frac = np.abs(val - np.round(val))[n > max_exact]
    assert frac.min() > 1e-3, "bucket boundary too close to an integer distance for float32"
    large = np.minimum(max_exact + val.astype(np.int64), N_BUCKETS - 1)
    return np.where(n < max_exact, n, large)


def _swa_bucket_tiles():
    table = _t5_bucket_table()
    qi = np.arange(WINDOW)[:, None]
    kj = np.arange(2 * WINDOW)[None, :]
    tiles = []
    for offset in (WINDOW, 0):
        rel = qi + offset - kj
        valid = (rel >= 0) & (rel < WINDOW)
        tiles.append(np.where(valid, table[np.clip(rel, 0, WINDOW - 1)], -1))
    return np.stack(tiles).astype(np.int32)


def _swa_bias_kernel(rb_ref, idx_ref, o_ref):
    h = pl.program_id(0)
    for tile in range(2):
        idx = idx_ref[tile]
        acc = jnp.full(idx.shape, NEG_BIG, F32)
        for bucket in range(N_BUCKETS):
            acc = jnp.where(idx == bucket, rb_ref[bucket, h], acc)
        o_ref[tile, 0] = acc


def _swa_bias(rel_bias):
    idx = jnp.asarray(_swa_bucket_tiles())
    return pl.pallas_call(
        _swa_bias_kernel,
        grid=(N_SWA_HEADS,),
        in_specs=[
            pl.BlockSpec(memory_space=pltpu.SMEM),
            pl.BlockSpec((2, WINDOW, 2 * WINDOW), lambda h: (0, 0, 0)),
        ],
        out_specs=pl.BlockSpec((2, 1, WINDOW, 2 * WINDOW), lambda h: (0, h, 0, 0)),
        out_shape=jax.ShapeDtypeStruct((2, N_SWA_HEADS, WINDOW, 2 * WINDOW), F32),
        compiler_params=_params(("arbitrary",)),
        name="swa_bias",
    )(rel_bias, idx)


def _swa_kernel(sink_ref, q_ref, k_ref, v_ref, bias_ref, o_ref, *, tq):
    qt = pl.program_id(1)
    scale = jnp.asarray(QK_SCALE, BF16)
    group = N_SWA_HEADS // N_SWA_KV
    for j in range(tq // WINDOW):
        nb = qt * (tq // WINDOW) + j
        first = (nb == 0).astype(jnp.int32) if j == 0 else 0
        w0 = pl.multiple_of(jnp.maximum(nb - 1, 0) * WINDOW, WINDOW)
        rows = slice(j * WINDOW, (j + 1) * WINDOW)
        for g in range(N_SWA_KV):
            kwin = k_ref[pl.ds(w0, 2 * WINDOW), g * LANES:(g + 1) * LANES]
            vwin = v_ref[pl.ds(w0, 2 * WINDOW), g * LANES:(g + 1) * LANES]
            for c in range(group // 2):
                pair = g * (group // 2) + c
                cols = slice(pair * LANES, (pair + 1) * LANES)
                q_heads = _head_pair(q_ref[rows, cols] * scale)
                outs = []
                for i in range(2):
                    head = 2 * pair + i
                    sink = sink_ref[head]
                    logits = _dot_nt(q_heads[i], kwin) + bias_ref[first, head]
                    m = jnp.maximum(jnp.max(logits, axis=-1, keepdims=True), sink)
                    p = jnp.exp(logits - m)
                    denom = jnp.sum(p, axis=-1, keepdims=True) + jnp.exp(sink - m)
                    outs.append(_dot(p.astype(BF16), vwin) / denom)
                o_ref[rows, cols] = _merge_pair(outs[0], outs[1]).astype(o_ref.dtype)


def _swa_attention(proj, bias, sinks, tq=512):
    b, s, width = proj.shape
    q_blk = (width - SWA_QW - 4 * LANES) // SWA_QW
    k_blk = (width - 4 * LANES) // (2 * LANES)
    return pl.pallas_call(
        functools.partial(_swa_kernel, tq=tq),
        grid=(b, s // tq),
        in_specs=[
            pl.BlockSpec(memory_space=pltpu.SMEM),
            pl.BlockSpec((None, tq, SWA_QW), lambda i, j: (i, j, q_blk)),
            pl.BlockSpec((None, s, 2 * LANES), lambda i, j: (i, 0, k_blk)),
            pl.BlockSpec((None, s, 2 * LANES), lambda i, j: (i, 0, k_blk + 1)),
            _resident((2, N_SWA_HEADS, WINDOW, 2 * WINDOW), lambda i, j: (0, 0, 0, 0)),
        ],
        out_specs=pl.BlockSpec((None, tq, SWA_QW), lambda i, j: (i, j, 0)),
        out_shape=jax.ShapeDtypeStruct((b, s, SWA_QW), BF16),
        compiler_params=_params(("parallel", "arbitrary")),
        name="swa_attention",
    )(sinks, proj, proj, proj, bias)


def _fox_gate_kernel(f_ref, bf_ref, c_ref, *, chunk):
    s = f_ref.shape[0]
    r = lax.broadcasted_iota(jnp.int32, (chunk, chunk), 0)
    c = lax.broadcasted_iota(jnp.int32, (chunk, chunk), 1)
    upto = jnp.where(c <= r, 1.0, 0.0).astype(BF16)
    lane = lax.broadcasted_iota(jnp.int32, (chunk, LANES), 1)
    carry = jnp.zeros((1, LANES), F32)
    for j in range(s // chunk):
        rows = slice(j * chunk, (j + 1) * chunk)
        log_f = _log2_sigmoid((f_ref[rows, :] + bf_ref[...]) * LOG2E)
        cs = carry
        for part in _split_bf16(log_f, 3):
            cs = cs + _dot(upto, part)
        carry = cs[chunk - 1:chunk, :]
        out = jnp.zeros((chunk, LANES), BF16)
        for k, part in reversed(list(enumerate(_split_bf16(cs, GATE_PARTS)))):
            out = jnp.where(lane < (k + 1) * N_FOX_HEADS, part, out)
        c_ref[rows, :] = out


def _fox_gates(f, b_f, chunk=256):
    b, s, _ = f.shape
    return pl.pallas_call(
        functools.partial(_fox_gate_kernel, chunk=chunk),
        grid=(b,),
        in_specs=[
            pl.BlockSpec((None, s, LANES), lambda i: (i, 0, 0)),
            _resident((1, LANES), lambda i: (0, 0)),
        ],
        out_specs=pl.BlockSpec((None, s, LANES), lambda i: (i, 0, 0)),
        out_shape=jax.ShapeDtypeStruct((b, s, LANES), BF16),
        compiler_params=_params(("parallel",)),
        name="fox_gates",
    )(f, b_f)


def _fox_kernel(qt_ref, k_ref, vt_ref, c_ref, o_ref, sa_ref, sb_ref, pa_ref, pb_ref, acc_ref):
    qi = pl.program_id(2)
    q_heads = _head_operands_t(qt_ref[...])
    heads = range(len(q_heads))
    first_head = pl.program_id(1) * len(q_heads)
    gate_row = lax.broadcasted_iota(jnp.int32, (LANES, QUERY_TILE), 0)
    masks = _diagonal_masks(strict=False)
    w_heads = []
    for h in heads:
        picks = ((gate_row & (N_FOX_HEADS - 1)) == first_head + h) & (gate_row < GATE_PARTS * N_FOX_HEADS)
        minus_gate = jnp.where(picks, -1.0, 0.0).astype(BF16)
        w_heads.append(jnp.concatenate([q_heads[h], minus_gate], axis=0))

    def scores(kb, s_ref, cols=ALL):
        start = pl.multiple_of(kb * KEY_BLOCK, KEY_BLOCK)
        gates = c_ref[pl.ds(start, KEY_BLOCK), :]
        for h in heads:
            lhs = jnp.concatenate([_key_rows(k_ref, kb, h), gates], axis=1)
            s_ref[h, :, cols] = _dot(lhs, w_heads[h][:, cols])

    def accumulate(kb, p_ref, a, cols=ALL):
        for h in heads:
            pv = _dot(_value_rows_t(vt_ref, kb, h), pltpu.bitcast(p_ref[h, :, cols], BF16))
            acc_ref[h, :, cols] = a[h] * acc_ref[h, :, cols] + pv

    def softmax(s_ref, p_ref, stats, mask, cols=ALL):
        new_stats, a = [], []
        for h in heads:
            m, l = stats[h][0][:, cols], stats[h][1][:, cols]
            s2 = s_ref[h, :, cols] if mask is None else jnp.where(mask, s_ref[h, :, cols], NEG_BIG)
            m_new = jnp.maximum(m, jnp.max(s2, axis=0, keepdims=True))
            ah = jnp.exp2(m - m_new)
            p = jnp.exp2(s2 - m_new)
            p_ref[h, :, cols] = pltpu.bitcast(p.astype(BF16), jnp.uint32)
            l_new = ah * l + jnp.sum(p, axis=0, keepdims=True)
            if cols is not ALL:
                m_new, l_new = _widen(stats[h][0], m_new), _widen(stats[h][1], l_new)
            new_stats.append((m_new, l_new))
            a.append(ah)
        return new_stats, a

    def half_step(kb, cur, oth, stats, a_prev, *, mask=None, cols=ALL, next_cols=ALL):
        if next_cols is not None:
            scores(kb + 1, oth[0], next_cols)
        accumulate(jnp.maximum(kb - 1, 0), oth[1], a_prev)
        return softmax(cur[0], cur[1], stats, mask, cols)

    slot_a, slot_b = (sa_ref, pa_ref), (sb_ref, pb_ref)
    pb_ref[...] = jnp.zeros_like(pb_ref)
    acc_ref[...] = jnp.zeros_like(acc_ref)
    scores(0, sa_ref)
    stats = [(jnp.full((1, QUERY_TILE), NEG_BIG, F32), jnp.zeros((1, QUERY_TILE), F32)) for _ in heads]
    ones = [jnp.ones((1, QUERY_TILE), F32) for _ in heads]

    def body(i, carry):
        stats, a = carry
        stats, a = half_step(2 * i, slot_a, slot_b, stats, a)
        return half_step(2 * i + 1, slot_b, slot_a, stats, a)

    stats, a = lax.fori_loop(0, qi, body, (stats, ones))
    stats, a = half_step(2 * qi, slot_a, slot_b, stats, a, mask=masks[0], next_cols=LATE)
    stats, a = half_step(2 * qi + 1, slot_b, slot_a, stats, a, mask=masks[1], cols=LATE, next_cols=None)
    accumulate(2 * qi + 1, pb_ref, a, LATE)
    for h in heads:
        o_ref[h * HEAD_DIM:(h + 1) * HEAD_DIM, :] = (acc_ref[h] / stats[h][1]).astype(o_ref.dtype)


def _fox_attention(proj, q_t, v_t, gates, pairs=4):
    b, s, _ = proj.shape
    groups = N_FOX_PAIRS // pairs
    return pl.pallas_call(
        _fox_kernel,
        grid=(b, groups, s // QUERY_TILE),
        in_specs=[
            pl.BlockSpec((None, pairs * LANES, QUERY_TILE), lambda i, h, j: (i, h, j)),
            pl.BlockSpec((None, s, pairs * LANES), lambda i, h, j: (i, 0, h)),
            pl.BlockSpec((None, s // KEY_BLOCK, pairs * LANES, KEY_BLOCK), lambda i, h, j: (i, 0, h, 0)),
            pl.BlockSpec((None, s, LANES), lambda i, h, j: (i, 0, 0)),
        ],
        out_specs=pl.BlockSpec((None, pairs * LANES, QUERY_TILE), lambda i, h, j: (i, h, j)),
        out_shape=jax.ShapeDtypeStruct((b, FOX_W, s), BF16),
        scratch_shapes=_attention_scratch(pairs),
        compiler_params=_params(("parallel", "parallel", "arbitrary")),
        name="fox_attention",
    )(q_t, proj, v_t, gates)


def _even_mixer(h, w_in, sinks, rel_bias):
    kv0 = 3 * SB_W + SWA_QW
    dup = lambda w: jnp.repeat(w.reshape(D_MODEL, N_SWA_KV, HEAD_DIM), 2, axis=1).reshape(D_MODEL, -1)
    w_cat = jnp.concatenate(
        [w_in[:, SB_W:2 * SB_W], w_in[:, 3 * SB_W:kv0],
         dup(w_in[:, kv0:kv0 + SWA_KVW]), dup(w_in[:, kv0 + SWA_KVW:])], axis=1)
    proj, q_t, v_t = _proj(h, w_cat.astype(BF16), w_in[:, :SB_W].T.astype(BF16),
                           w_in[:, 2 * SB_W:3 * SB_W].T.astype(BF16), -QK_SCALE * LOG2E)
    o_sb = _sb_attention(proj, q_t, v_t)
    o_sw = _swa_attention(proj, _swa_bias(rel_bias), sinks.astype(F32))
    return o_sb, o_sw


def _odd_mixer(h, w_in, b_f):
    spread = lambda v: jnp.pad(jnp.tile(v, (1, GATE_PARTS)), ((0, 0), (0, LANES - GATE_PARTS * N_FOX_HEADS)))
    w_cat = jnp.concatenate([w_in[:, FOX_W:2 * FOX_W], spread(w_in[:, 3 * FOX_W:])], axis=1)
    proj, q_t, v_t, f = _proj(h, w_cat.astype(BF16), w_in[:, :FOX_W].T.astype(BF16),
                              w_in[:, 2 * FOX_W:3 * FOX_W].T.astype(BF16), QK_SCALE * LOG2E, tail=LANES)
    gates = _fox_gates(f, spread(b_f.reshape(1, -1).astype(F32)))
    return _fox_attention(proj, q_t, v_t, gates), None


def kernel(x, ln_g, ln_b, ffn1_in, ffn1_out, ffn2_in, ffn2_out, ab_w_in, ab_w_out, ab_sinks,
           fox_w_in, fox_b_f, fox_w_out, rel_bias):
    bsz, s, d = x.shape
    assert (s, d) == (SEQ, D_MODEL) and ln_g.shape[0] == DEPTH
    row = lambda v: v.reshape(1, D_MODEL).astype(F32)
    ffn1 = ffn1_in.astype(BF16), ffn1_out.astype(BF16)
    ffn2 = ffn2_in.astype(BF16), ffn2_out.astype(BF16)
    h = x
    for layer in range(DEPTH):
        ln = [(row(ln_g[layer, k]), row(ln_b[layer, k])) for k in range(3)]
        h = _ffn_ln(h.reshape(bsz * s, d), *ffn1, layer, *ln[0]).reshape(bsz, s, d)
        i = layer // 2
        if layer % 2 == 0:
            o_t, o = _even_mixer(h, ab_w_in[i], ab_sinks[i], rel_bias)
            w_mix = ab_w_out[i]
        else:
            o_t, o = _odd_mixer(h, fox_w_in[i], fox_b_f[i])
            w_mix = fox_w_out[i]
        h = _mix_ffn_ln(o_t, o, h, w_mix.astype(BF16), ln[1], *ffn2, layer, ln[2])
    return h
```

```python
import functools
import math

import numpy as np
import jax
import jax.numpy as jnp
from jax import lax
from jax.experimental import pallas as pl
from jax.experimental.pallas import tpu as pltpu

D_MODEL = 1024
SEQ = 2048
DEPTH = 2
HEAD_DIM = 64
LANES = 128
N_SB_PAIRS = 4
N_SWA_HEADS = 8
N_SWA_KV = 2
N_FOX_HEADS = 16
N_FOX_PAIRS = 8
WINDOW = 128
N_BUCKETS = 32
MAX_DISTANCE = 128
D_FF = 2816
ALPHA = (2 * DEPTH) ** 0.25
LN_EPS = 1e-5
SB_W = 512
SWA_QW = 512
SWA_KVW = 128
FOX_W = 1024
NEG_BIG = -1e30
LOG2E = math.log2(math.e)
QK_SCALE = HEAD_DIM ** -0.5
GATE_PARTS = 3
KEY_BLOCK = 256
QUERY_TILE = 2 * KEY_BLOCK
VMEM_LIMIT = 56 * 1024 * 1024

BF16 = jnp.bfloat16
F32 = jnp.float32


def _dot(a, b):
    return jnp.dot(a, b, preferred_element_type=F32)


def _dot_nt(a, b):
    return lax.dot_general(a, b, (((1,), (1,)), ((), ())), preferred_element_type=F32)


def _dot_tn(a, b):
    return lax.dot_general(a, b, (((0,), (0,)), ((), ())), preferred_element_type=F32)


def _layer_norm(z, g, b):
    mu = jnp.mean(z, axis=-1, keepdims=True)
    d = z - mu
    var = jnp.mean(d * d, axis=-1, keepdims=True)
    return d * lax.rsqrt(var + LN_EPS) * g + b


def _log2_sigmoid(x2):
    sign = jnp.uint32(0x80000000)
    neg_abs = lax.bitcast_convert_type(lax.bitcast_convert_type(x2, jnp.uint32) | sign, F32)
    return jnp.minimum(x2, 0.0) - jnp.log2(1.0 + jnp.exp2(neg_abs))


def _split_bf16(x, parts):
    out = []
    r = x
    for i in range(parts):
        t = r.astype(BF16)
        out.append(t)
        if i + 1 < parts:
            r = r - t.astype(F32)
    return out


def _params(sem):
    return pltpu.CompilerParams(dimension_semantics=sem, vmem_limit_bytes=VMEM_LIMIT)


def _resident(shape, index_map):
    return pl.BlockSpec(shape, index_map, pipeline_mode=pl.Buffered(1))


def _swiglu_ln(x, wg_ref, wu_ref, wo_ref, g_ref, b_ref):
    xb = x.astype(BF16)
    gate = _dot(xb, wg_ref[...])
    up = _dot(xb, wu_ref[...])
    h = (gate * jax.nn.sigmoid(gate) * up).astype(BF16)
    y = _dot(h, wo_ref[...])
    return _layer_norm(ALPHA * x + 0.5 * y, g_ref[...], b_ref[...])


def _mix_ffn_ln_kernel(*refs, n_normal):
    ot_ref, refs = refs[0], refs[1:]
    o_refs, refs = refs[:n_normal], refs[n_normal:]
    h_ref, wt_ref, refs = refs[0], refs[1], refs[2:]
    w_refs, refs = refs[:n_normal], refs[n_normal:]
    g1_ref, b1_ref, wg_ref, wu_ref, wo_ref, g2_ref, b2_ref, out_ref = refs
    mix = _dot_tn(ot_ref[...], wt_ref[...])
    for o_ref, w_ref in zip(o_refs, w_refs):
        mix = mix + _dot(o_ref[...], w_ref[...])
    half = h_ref.shape[0] // 2
    xs = []
    for r in range(2):
        rows = slice(r * half, (r + 1) * half)
        xs.append(_layer_norm(ALPHA * h_ref[rows, :] + mix[rows], g1_ref[...], b1_ref[...]))
    for r in range(2):
        rows = slice(r * half, (r + 1) * half)
        out_ref[rows, :] = _swiglu_ln(xs[r], wg_ref, wu_ref, wo_ref, g2_ref, b2_ref)


def _ffn_proj_kernel(x_ref, wg_ref, wu_ref, wo_ref, g_ref, b_ref, w_ref, wt_ref,
                     h_ref, o_ref, qt_ref, vt_ref, *tail_ref, q_scale):
    half = x_ref.shape[0] // 2
    for r in range(2):
        rows = slice(r * half, (r + 1) * half)
        h_ref[rows, :] = _swiglu_ln(x_ref[rows, :], wg_ref, wu_ref, wo_ref, g_ref, b_ref)
    hb = h_ref[...].astype(BF16)
    width = o_ref.shape[-1]
    acc = _dot(hb, w_ref[...])
    o_ref[...] = acc[:, :width].astype(o_ref.dtype)
    for ref in tail_ref:
        ref[...] = acc[:, width:]
    acc_t = _dot_nt(wt_ref[...], hb)
    qw = qt_ref.shape[0]
    qt_ref[...] = (acc_t[:qw] * q_scale).astype(qt_ref.dtype)
    for j in range(vt_ref.shape[0]):
        vt_ref[j] = acc_t[qw:, j * KEY_BLOCK:(j + 1) * KEY_BLOCK].astype(vt_ref.dtype)


def _ffn_proj(x, w_in, w_out, layer, g, b, w, wq_t, wv_t, q_scale, tail=0, tm=512):
    bsz, s, _ = x.shape
    width, qw, vw = w.shape[1] - tail, wq_t.shape[0], wv_t.shape[0]
    const = lambda i, j: (0, 0)
    tok = lambda n: pl.BlockSpec((None, tm, n), lambda i, j: (i, j, 0))
    out_specs = [tok(D_MODEL), tok(width), pl.BlockSpec((None, qw, tm), lambda i, j: (i, 0, j)),
                 pl.BlockSpec((None, tm // KEY_BLOCK, vw, KEY_BLOCK), lambda i, j: (i, j, 0, 0))]
    out_shape = [jax.ShapeDtypeStruct((bsz, s, D_MODEL), F32), jax.ShapeDtypeStruct((bsz, s, width), BF16),
                 jax.ShapeDtypeStruct((bsz, qw, s), BF16),
                 jax.ShapeDtypeStruct((bsz, s // KEY_BLOCK, vw, KEY_BLOCK), BF16)]
    if tail:
        out_specs.append(tok(tail))
        out_shape.append(jax.ShapeDtypeStruct((bsz, s, tail), F32))
    return pl.pallas_call(
        functools.partial(_ffn_proj_kernel, q_scale=q_scale),
        grid=(bsz, s // tm),
        in_specs=[
            tok(D_MODEL),
            _resident((None, D_MODEL, D_FF), lambda i, j: (layer, 0, 0)),
            _resident((None, D_MODEL, D_FF), lambda i, j: (layer, 0, 1)),
            _resident((None, D_FF, D_MODEL), lambda i, j: (layer, 0, 0)),
            _resident((1, D_MODEL), const), _resident((1, D_MODEL), const),
            _resident((D_MODEL, width + tail), const),
            _resident((qw + vw, D_MODEL), const),
        ],
        out_specs=out_specs,
        out_shape=out_shape,
        compiler_params=_params(("parallel", "parallel")),
        name="ffn_proj",
    )(x, w_in, w_in, w_out, g, b, w, jnp.concatenate([wq_t, wv_t], axis=0))


def _mix_ffn_ln(o_t, o, h, w_mix, ln1, w_in, w_out, layer, ln2, tm=512):
    bsz, s, _ = h.shape
    wt = o_t.shape[1]
    const = lambda i, j: (0, 0)
    tok = lambda width: pl.BlockSpec((None, tm, width), lambda i, j: (i, j, 0))
    row = _resident((1, D_MODEL), const)
    normal = [] if o is None else [o]
    in_specs = ([pl.BlockSpec((None, wt, tm), lambda i, j: (i, 0, j))] + [tok(a.shape[2]) for a in normal]
                + [tok(D_MODEL), _resident((wt, D_MODEL), const)]
                + [_resident((a.shape[2], D_MODEL), lambda i, j: (wt // a.shape[2], 0)) for a in normal]
                + [row, row, _resident((None, D_MODEL, D_FF), lambda i, j: (layer, 0, 0)),
                   _resident((None, D_MODEL, D_FF), lambda i, j: (layer, 0, 1)),
                   _resident((None, D_FF, D_MODEL), lambda i, j: (layer, 0, 0)), row, row])
    args = [o_t] + normal + [h, w_mix] + [w_mix for _ in normal] + [*ln1, w_in, w_in, w_out, *ln2]
    return pl.pallas_call(
        functools.partial(_mix_ffn_ln_kernel, n_normal=len(normal)),
        grid=(bsz, s // tm),
        in_specs=in_specs,
        out_specs=tok(D_MODEL),
        out_shape=jax.ShapeDtypeStruct((bsz, s, D_MODEL), F32),
        compiler_params=_params(("parallel", "parallel")),
        name="mix_ffn_ln",
    )(*args)


def _head_pair(q):
    lo = lax.broadcasted_iota(jnp.int32, q.shape, 1) < HEAD_DIM
    zero = jnp.zeros_like(q)
    return jnp.where(lo, q, zero), jnp.where(lo, zero, q)


def _merge_pair(a0, a1):
    lo = lax.broadcasted_iota(jnp.int32, a0.shape, 1) < HEAD_DIM
    return jnp.where(lo, a0, a1)


def _head_operands_t(q_t):
    top = lax.broadcasted_iota(jnp.int32, (LANES, q_t.shape[1]), 0) < HEAD_DIM
    out = []
    for p in range(q_t.shape[0] // LANES):
        tile = q_t[p * LANES:(p + 1) * LANES]
        zero = jnp.zeros_like(tile)
        out += [jnp.where(top, tile, zero), jnp.where(top, zero, tile)]
    return out


def _key_rows(k_ref, kb, h):
    start = pl.multiple_of(kb * KEY_BLOCK, KEY_BLOCK)
    return k_ref[pl.ds(start, KEY_BLOCK), (h // 2) * LANES:(h // 2 + 1) * LANES]


def _value_rows_t(vt_ref, kb, h):
    return vt_ref[kb, h * HEAD_DIM:(h + 1) * HEAD_DIM, :]


ALL = slice(None)
LATE = slice(KEY_BLOCK, QUERY_TILE)


def _diagonal_masks(strict):
    row = lax.broadcasted_iota(jnp.int32, (KEY_BLOCK, QUERY_TILE), 0)
    col = lax.broadcasted_iota(jnp.int32, (KEY_BLOCK, QUERY_TILE), 1)
    full = row < col if strict else row <= col
    return full, full[:, :KEY_BLOCK]


def _widen(row_vec, late_part):
    return jnp.concatenate([row_vec[:, :KEY_BLOCK], late_part], axis=1)


def _attention_scratch(pairs):
    tile = (2 * pairs, KEY_BLOCK, QUERY_TILE)
    packed = (2 * pairs, KEY_BLOCK // 2, QUERY_TILE)
    return [pltpu.VMEM(tile, F32), pltpu.VMEM(tile, F32),
            pltpu.VMEM(packed, jnp.uint32), pltpu.VMEM(packed, jnp.uint32),
            pltpu.VMEM((2 * pairs, HEAD_DIM, QUERY_TILE), F32)]


def _sb_kernel(qt_ref, k_ref, vt_ref, o_ref, nza_ref, nzb_ref, wa_ref, wb_ref, acc_ref):
    qi = pl.program_id(2)
    w_heads = _head_operands_t(qt_ref[...])
    heads = range(len(w_heads))
    masks = _diagonal_masks(strict=True)
    row = lax.broadcasted_iota(jnp.int32, (KEY_BLOCK, KEY_BLOCK), 0)
    col = lax.broadcasted_iota(jnp.int32, (KEY_BLOCK, KEY_BLOCK), 1)
    suffix = jnp.where(col >= row, 1.0, 0.0).astype(BF16)
    last = 2 * qi + 1

    def scores(kb, nz_ref, cols=ALL):
        for h in heads:
            nz_ref[h, :, cols] = _dot(_key_rows(k_ref, kb, h), w_heads[h][:, cols])

    def accumulate(kb, w_ref, cols=ALL):
        for h in heads:
            acc_ref[h, :, cols] += _dot(_value_rows_t(vt_ref, kb, h), pltpu.bitcast(w_ref[h, :, cols], BF16))

    def weights(nz_ref, w_ref, carries, mask, cols=ALL):
        incl = []
        for h in heads:
            log_keep = _log2_sigmoid(nz_ref[h, :, cols])
            if mask is not None:
                log_keep = jnp.where(mask, log_keep, 0.0)
            incl.append(_dot(suffix, log_keep.astype(BF16)))
        for h in heads:
            w = jnp.exp2(incl[h] + carries[h][:, cols] - nz_ref[h, :, cols])
            if mask is not None:
                w = jnp.where(mask, w, 0.0)
            w_ref[h, :, cols] = pltpu.bitcast(w.astype(BF16), jnp.uint32)
        totals = [carries[h][:, cols] + incl[h][0:1, :] for h in heads]
        return totals if cols is ALL else [_widen(carries[h], totals[h]) for h in heads]

    def half_step(j, cur, oth, carries, *, mask=None, cols=ALL, prev_cols=ALL, first=False):
        kb = last - j
        scores(jnp.maximum(kb - 1, 0), oth[0])
        if not first:
            accumulate(kb + 1, oth[1], prev_cols)
        return weights(cur[0], cur[1], carries, mask, cols)

    slot_a, slot_b = (nza_ref, wa_ref), (nzb_ref, wb_ref)
    acc_ref[...] = jnp.zeros_like(acc_ref)
    scores(last, nza_ref, LATE)
    carries = [jnp.zeros((1, QUERY_TILE), F32) for _ in heads]
    carries = half_step(0, slot_a, slot_b, carries, mask=masks[1], cols=LATE, first=True)
    carries = half_step(1, slot_b, slot_a, carries, mask=masks[0], prev_cols=LATE)

    def body(i, carries):
        carries = half_step(2 * i + 2, slot_a, slot_b, carries)
        return half_step(2 * i + 3, slot_b, slot_a, carries)

    lax.fori_loop(0, qi, body, carries)
    accumulate(0, wb_ref)
    for h in heads:
        o_ref[h * HEAD_DIM:(h + 1) * HEAD_DIM, :] = acc_ref[h].astype(o_ref.dtype)


def _sb_attention(proj, q_t, v_t, pairs=2):
    b, s, _ = proj.shape
    groups = N_SB_PAIRS // pairs
    return pl.pallas_call(
        _sb_kernel,
        grid=(b, groups, s // QUERY_TILE),
        in_specs=[
            pl.BlockSpec((None, pairs * LANES, QUERY_TILE), lambda i, h, j: (i, h, j)),
            pl.BlockSpec((None, s, pairs * LANES), lambda i, h, j: (i, 0, h)),
            pl.BlockSpec((None, s // KEY_BLOCK, pairs * LANES, KEY_BLOCK), lambda i, h, j: (i, 0, h, 0)),
        ],
        out_specs=pl.BlockSpec((None, pairs * LANES, QUERY_TILE), lambda i, h, j: (i, h, j)),
        out_shape=jax.ShapeDtypeStruct((b, SB_W, s), BF16),
        scratch_shapes=_attention_scratch(pairs),
        compiler_params=_params(("parallel", "parallel", "arbitrary")),
        name="sb_attention",
    )(q_t, proj, v_t)


def _t5_bucket_table():
    n = np.arange(WINDOW)
    max_exact = N_BUCKETS // 2
    nf = np.maximum(n, 1).astype(np.float64)
    val = np.log(nf / max_exact) / math.log(MAX_DISTANCE / max_exact) * (N_BUCKETS - max_exact)
    frac = np.abs(val - np.round(val))[n > max_exact]
    assert frac.min() > 1e-3, "bucket boundary too close to an integer distance for float32"
    large = np.minimum(max_exact + val.astype(np.int64), N_BUCKETS - 1)
    return np.where(n < max_exact, n, large)


def _swa_bucket_tiles():
    table = _t5_bucket_table()
    qi = np.arange(WINDOW)[:, None]
    kj = np.arange(2 * WINDOW)[None, :]
    tiles = []
    for offset in (WINDOW, 0):
        rel = qi + offset - kj
        valid = (rel >= 0) & (rel < WINDOW)
        tiles.append(np.where(valid, table[np.clip(rel, 0, WINDOW - 1)], -1))
    return np.stack(tiles).astype(np.int32)


def _swa_bias_kernel(rb_ref, idx_ref, o_ref):
    h = pl.program_id(0)
    for tile in range(2):
        idx = idx_ref[tile]
        acc = jnp.full(idx.shape, NEG_BIG, F32)
        for bucket in range(N_BUCKETS):
            acc = jnp.where(idx == bucket, rb_ref[bucket, h], acc)
        o_ref[tile, 0] = acc


def _swa_bias(rel_bias):
    idx = jnp.asarray(_swa_bucket_tiles())
    return pl.pallas_call(
        _swa_bias_kernel,
        grid=(N_SWA_HEADS,),
        in_specs=[
            pl.BlockSpec(memory_space=pltpu.SMEM),
            pl.BlockSpec((2, WINDOW, 2 * WINDOW), lambda h: (0, 0, 0)),
        ],
        out_specs=pl.BlockSpec((2, 1, WINDOW, 2 * WINDOW), lambda h: (0, h, 0, 0)),
        out_shape=jax.ShapeDtypeStruct((2, N_SWA_HEADS, WINDOW, 2 * WINDOW), F32),
        compiler_params=_params(("arbitrary",)),
        name="swa_bias",
    )(rel_bias, idx)


def _swa_kernel(sink_ref, q_ref, k_ref, v_ref, bias_ref, o_ref, *, tq):
    qt = pl.program_id(1)
    scale = jnp.asarray(QK_SCALE, BF16)
    group = N_SWA_HEADS // N_SWA_KV
    for j in range(tq // WINDOW):
        nb = qt * (tq // WINDOW) + j
        first = (nb == 0).astype(jnp.int32) if j == 0 else 0
        w0 = pl.multiple_of(jnp.maximum(nb - 1, 0) * WINDOW, WINDOW)
        rows = slice(j * WINDOW, (j + 1) * WINDOW)
        for g in range(N_SWA_KV):
            kwin = k_ref[pl.ds(w0, 2 * WINDOW), g * LANES:(g + 1) * LANES]
            vwin = v_ref[pl.ds(w0, 2 * WINDOW), g * LANES:(g + 1) * LANES]
            for c in range(group // 2):
                pair = g * (group // 2) + c
                cols = slice(pair * LANES, (pair + 1) * LANES)
                q_heads = _head_pair(q_ref[rows, cols] * scale)
                outs = []
                for i in range(2):
                    head = 2 * pair + i
                    sink = sink_ref[head]
                    logits = _dot_nt(q_heads[i], kwin) + bias_ref[first, head]
                    m = jnp.maximum(jnp.max(logits, axis=-1, keepdims=True), sink)
                    p = jnp.exp(logits - m)
                    denom = jnp.sum(p, axis=-1, keepdims=True) + jnp.exp(sink - m)
                    outs.append(_dot(p.astype(BF16), vwin) / denom)
                o_ref[rows, cols] = _merge_pair(outs[0], outs[1]).astype(o_ref.dtype)


def _swa_attention(proj, bias, sinks, tq=512):
    b, s, width = proj.shape
    q_blk = (width - SWA_QW - 4 * LANES) // SWA_QW
    k_blk = (width - 4 * LANES) // (2 * LANES)
    return pl.pallas_call(
        functools.partial(_swa_kernel, tq=tq),
        grid=(b, s // tq),
        in_specs=[
            pl.BlockSpec(memory_space=pltpu.SMEM),
            pl.BlockSpec((None, tq, SWA_QW), lambda i, j: (i, j, q_blk)),
            pl.BlockSpec((None, s, 2 * LANES), lambda i, j: (i, 0, k_blk)),
            pl.BlockSpec((None, s, 2 * LANES), lambda i, j: (i, 0, k_blk + 1)),
            _resident((2, N_SWA_HEADS, WINDOW, 2 * WINDOW), lambda i, j: (0, 0, 0, 0)),
        ],
        out_specs=pl.BlockSpec((None, tq, SWA_QW), lambda i, j: (i, j, 0)),
        out_shape=jax.ShapeDtypeStruct((b, s, SWA_QW), BF16),
        compiler_params=_params(("parallel", "arbitrary")),
        name="swa_attention",
    )(sinks, proj, proj, proj, bias)


def _fox_gate_kernel(f_ref, bf_ref, c_ref, *, chunk):
    s = f_ref.shape[0]
    r = lax.broadcasted_iota(jnp.int32, (chunk, chunk), 0)
    c = lax.broadcasted_iota(jnp.int32, (chunk, chunk), 1)
    upto = jnp.where(c <= r, 1.0, 0.0).astype(BF16)
    lane = lax.broadcasted_iota(jnp.int32, (chunk, LANES), 1)
    carry = jnp.zeros((1, LANES), F32)
    for j in range(s // chunk):
        rows = slice(j * chunk, (j + 1) * chunk)
        log_f = _log2_sigmoid((f_ref[rows, :] + bf_ref[...]) * LOG2E)
        cs = carry
        for part in _split_bf16(log_f, 3):
            cs = cs + _dot(upto, part)
        carry = cs[chunk - 1:chunk, :]
        out = jnp.zeros((chunk, LANES), BF16)
        for k, part in reversed(list(enumerate(_split_bf16(cs, GATE_PARTS)))):
            out = jnp.where(lane < (k + 1) * N_FOX_HEADS, part, out)
        c_ref[rows, :] = out


def _fox_gates(f, b_f, chunk=256):
    b, s, _ = f.shape
    return pl.pallas_call(
        functools.partial(_fox_gate_kernel, chunk=chunk),
        grid=(b,),
        in_specs=[
            pl.BlockSpec((None, s, LANES), lambda i: (i, 0, 0)),
            _resident((1, LANES), lambda i: (0, 0)),
        ],
        out_specs=pl.BlockSpec((None, s, LANES), lambda i: (i, 0, 0)),
        out_shape=jax.ShapeDtypeStruct((b, s, LANES), BF16),
        compiler_params=_params(("parallel",)),
        name="fox_gates",
    )(f, b_f)


def _fox_kernel(qt_ref, k_ref, vt_ref, c_ref, o_ref, sa_ref, sb_ref, pa_ref, pb_ref, acc_ref):
    qi = pl.program_id(2)
    q_heads = _head_operands_t(qt_ref[...])
    heads = range(len(q_heads))
    first_head = pl.program_id(1) * len(q_heads)
    gate_row = lax.broadcasted_iota(jnp.int32, (LANES, QUERY_TILE), 0)
    masks = _diagonal_masks(strict=False)
    w_heads = []
    for h in heads:
        picks = ((gate_row & (N_FOX_HEADS - 1)) == first_head + h) & (gate_row < GATE_PARTS * N_FOX_HEADS)
        minus_gate = jnp.where(picks, -1.0, 0.0).astype(BF16)
        w_heads.append(jnp.concatenate([q_heads[h], minus_gate], axis=0))

    def scores(kb, s_ref, cols=ALL):
        start = pl.multiple_of(kb * KEY_BLOCK, KEY_BLOCK)
        gates = c_ref[pl.ds(start, KEY_BLOCK), :]
        for h in heads:
            lhs = jnp.concatenate([_key_rows(k_ref, kb, h), gates], axis=1)
            s_ref[h, :, cols] = _dot(lhs, w_heads[h][:, cols])

    def accumulate(kb, p_ref, a, cols=ALL):
        for h in heads:
            pv = _dot(_value_rows_t(vt_ref, kb, h), pltpu.bitcast(p_ref[h, :, cols], BF16))
            acc_ref[h, :, cols] = a[h] * acc_ref[h, :, cols] + pv

    def softmax(s_ref, p_ref, stats, mask, cols=ALL):
        new_stats, a = [], []
        for h in heads:
            m, l = stats[h][0][:, cols], stats[h][1][:, cols]
            s2 = s_ref[h, :, cols] if mask is None else jnp.where(mask, s_ref[h, :, cols], NEG_BIG)
            m_new = jnp.maximum(m, jnp.max(s2, axis=0, keepdims=True))
            ah = jnp.exp2(m - m_new)
            p = jnp.exp2(s2 - m_new)
            p_ref[h, :, cols] = pltpu.bitcast(p.astype(BF16), jnp.uint32)
            l_new = ah * l + jnp.sum(p, axis=0, keepdims=True)
            if cols is not ALL:
                m_new, l_new = _widen(stats[h][0], m_new), _widen(stats[h][1], l_new)
            new_stats.append((m_new, l_new))
            a.append(ah)
        return new_stats, a

    def half_step(kb, cur, oth, stats, a_prev, *, mask=None, cols=ALL, next_cols=ALL):
        if next_cols is not None:
            scores(kb + 1, oth[0], next_cols)
        accumulate(jnp.maximum(kb - 1, 0), oth[1], a_prev)
        return softmax(cur[0], cur[1], stats, mask, cols)

    slot_a, slot_b = (sa_ref, pa_ref), (sb_ref, pb_ref)
    pb_ref[...] = jnp.zeros_like(pb_ref)
    acc_ref[...] = jnp.zeros_like(acc_ref)
    scores(0, sa_ref)
    stats = [(jnp.full((1, QUERY_TILE), NEG_BIG, F32), jnp.zeros((1, QUERY_TILE), F32)) for _ in heads]
    ones = [jnp.ones((1, QUERY_TILE), F32) for _ in heads]

    def body(i, carry):
        stats, a = carry
        stats, a = half_step(2 * i, slot_a, slot_b, stats, a)
        return half_step(2 * i + 1, slot_b, slot_a, stats, a)

    stats, a = lax.fori_loop(0, qi, body, (stats, ones))
    stats, a = half_step(2 * qi, slot_a, slot_b, stats, a, mask=masks[0], next_cols=LATE)
    stats, a = half_step(2 * qi + 1, slot_b, slot_a, stats, a, mask=masks[1], cols=LATE, next_cols=None)
    accumulate(2 * qi + 1, pb_ref, a, LATE)
    for h in heads:
        o_ref[h * HEAD_DIM:(h + 1) * HEAD_DIM, :] = (acc_ref[h] / stats[h][1]).astype(o_ref.dtype)


def _fox_attention(proj, q_t, v_t, gates, pairs=4):
    b, s, _ = proj.shape
    groups = N_FOX_PAIRS // pairs
    return pl.pallas_call(
        _fox_kernel,
        grid=(b, groups, s // QUERY_TILE),
        in_specs=[
            pl.BlockSpec((None, pairs * LANES, QUERY_TILE), lambda i, h, j: (i, h, j)),
            pl.BlockSpec((None, s, pairs * LANES), lambda i, h, j: (i, 0, h)),
            pl.BlockSpec((None, s // KEY_BLOCK, pairs * LANES, KEY_BLOCK), lambda i, h, j: (i, 0, h, 0)),
            pl.BlockSpec((None, s, LANES), lambda i, h, j: (i, 0, 0)),
        ],
        out_specs=pl.BlockSpec((None, pairs * LANES, QUERY_TILE), lambda i, h, j: (i, h, j)),
        out_shape=jax.ShapeDtypeStruct((b, FOX_W, s), BF16),
        scratch_shapes=_attention_scratch(pairs),
        compiler_params=_params(("parallel", "parallel", "arbitrary")),
        name="fox_attention",
    )(q_t, proj, v_t, gates)


def _even_projection(w_in):
    kv0 = 3 * SB_W + SWA_QW
    dup = lambda w: jnp.repeat(w.reshape(D_MODEL, N_SWA_KV, HEAD_DIM), 2, axis=1).reshape(D_MODEL, -1)
    w_cat = jnp.concatenate(
        [w_in[:, SB_W:2 * SB_W], w_in[:, 3 * SB_W:kv0],
         dup(w_in[:, kv0:kv0 + SWA_KVW]), dup(w_in[:, kv0 + SWA_KVW:])], axis=1)
    return (w_cat.astype(BF16), w_in[:, :SB_W].T.astype(BF16), w_in[:, 2 * SB_W:3 * SB_W].T.astype(BF16),
            -QK_SCALE * LOG2E)


def _even_attention(proj, q_t, v_t, sinks, rel_bias):
    return _sb_attention(proj, q_t, v_t), _swa_attention(proj, _swa_bias(rel_bias), sinks.astype(F32))


def _spread_gates(v):
    return jnp.pad(jnp.tile(v, (1, GATE_PARTS)), ((0, 0), (0, LANES - GATE_PARTS * N_FOX_HEADS)))


def _odd_projection(w_in):
    w_cat = jnp.concatenate([w_in[:, FOX_W:2 * FOX_W], _spread_gates(w_in[:, 3 * FOX_W:])], axis=1)
    return (w_cat.astype(BF16), w_in[:, :FOX_W].T.astype(BF16), w_in[:, 2 * FOX_W:3 * FOX_W].T.astype(BF16),
            QK_SCALE * LOG2E)


def _odd_attention(proj, q_t, v_t, f, b_f):
    gates = _fox_gates(f, _spread_gates(b_f.reshape(1, -1).astype(F32)))
    return _fox_attention(proj, q_t, v_t, gates), None


def kernel(x, ln_g, ln_b, ffn1_in, ffn1_out, ffn2_in, ffn2_out, ab_w_in, ab_w_out, ab_sinks,
           fox_w_in, fox_b_f, fox_w_out, rel_bias):
    bsz, s, d = x.shape
    assert (s, d) == (SEQ, D_MODEL) and ln_g.shape[0] == DEPTH
    row = lambda v: v.reshape(1, D_MODEL).astype(F32)
    ffn1 = ffn1_in.astype(BF16), ffn1_out.astype(BF16)
    ffn2 = ffn2_in.astype(BF16), ffn2_out.astype(BF16)
    h = x
    for layer in range(DEPTH):
        ln = [(row(ln_g[layer, k]), row(ln_b[layer, k])) for k in range(3)]
        i = layer // 2
        if layer % 2 == 0:
            h, *p = _ffn_proj(h, *ffn1, layer, *ln[0], *_even_projection(ab_w_in[i]))
            o_t, o = _even_attention(*p, ab_sinks[i], rel_bias)
            w_mix = ab_w_out[i]
        else:
            h, *p = _ffn_proj(h, *ffn1, layer, *ln[0], *_odd_projection(fox_w_in[i]), tail=LANES)
            o_t, o = _odd_attention(*p, fox_b_f[i])
            w_mix = fox_w_out[i]
        h = _mix_ffn_ln(o_t, o, h, w_mix.astype(BF16), ln[1], *ffn2, layer, ln[2])
    return h
```

```python
import functools
import math

import numpy as np
import jax
import jax.numpy as jnp
from jax import lax
from jax.experimental import pallas as pl
from jax.experimental.pallas import tpu as pltpu

D_MODEL = 1024
SEQ = 2048
DEPTH = 2
HEAD_DIM = 64
LANES = 128
N_SB_PAIRS = 4
N_SWA_HEADS = 8
N_SWA_KV = 2
N_FOX_HEADS = 16
N_FOX_PAIRS = 8
WINDOW = 128
N_BUCKETS = 32
MAX_DISTANCE = 128
D_FF = 2816
ALPHA = (2 * DEPTH) ** 0.25
LN_EPS = 1e-5
SB_W = 512
SWA_QW = 512
SWA_KVW = 128
FOX_W = 1024
NEG_BIG = -1e30
LOG2E = math.log2(math.e)
QK_SCALE = HEAD_DIM ** -0.5
GATE_PARTS = 3
KEY_BLOCK = 256
QUERY_TILE = 2 * KEY_BLOCK
VMEM_LIMIT = 56 * 1024 * 1024

BF16 = jnp.bfloat16
F32 = jnp.float32


def _dot(a, b):
    return jnp.dot(a, b, preferred_element_type=F32)


def _dot_nt(a, b):
    return lax.dot_general(a, b, (((1,), (1,)), ((), ())), preferred_element_type=F32)


def _dot_tn(a, b):
    return lax.dot_general(a, b, (((0,), (0,)), ((), ())), preferred_element_type=F32)


def _layer_norm(z, g, b):
    mu = jnp.mean(z, axis=-1, keepdims=True)
    d = z - mu
    var = jnp.mean(d * d, axis=-1, keepdims=True)
    return d * lax.rsqrt(var + LN_EPS) * g + b


def _log2_sigmoid(x2):
    sign = jnp.uint32(0x80000000)
    neg_abs = lax.bitcast_convert_type(lax.bitcast_convert_type(x2, jnp.uint32) | sign, F32)
    return jnp.minimum(x2, 0.0) - jnp.log2(1.0 + jnp.exp2(neg_abs))


def _split_bf16(x, parts):
    out = []
    r = x
    for i in range(parts):
        t = r.astype(BF16)
        out.append(t)
        if i + 1 < parts:
            r = r - t.astype(F32)
    return out


def _params(sem):
    return pltpu.CompilerParams(dimension_semantics=sem, vmem_limit_bytes=VMEM_LIMIT)


def _resident(shape, index_map):
    return pl.BlockSpec(shape, index_map, pipeline_mode=pl.Buffered(1))


def _swiglu_ln(x, wg_ref, wu_ref, wo_ref, g_ref, b_ref):
    xb = x.astype(BF16)
    gate = _dot(xb, wg_ref[...])
    up = _dot(xb, wu_ref[...])
    h = (gate * jax.nn.sigmoid(gate) * up).astype(BF16)
    y = _dot(h, wo_ref[...])
    return _layer_norm(ALPHA * x + 0.5 * y, g_ref[...], b_ref[...])


def _mix_ffn_ln_kernel(*refs, n_normal):
    ot_ref, refs = refs[0], refs[1:]
    o_refs, refs = refs[:n_normal], refs[n_normal:]
    h_ref, wt_ref, refs = refs[0], refs[1], refs[2:]
    w_refs, refs = refs[:n_normal], refs[n_normal:]
    g1_ref, b1_ref, wg_ref, wu_ref, wo_ref, g2_ref, b2_ref, out_ref = refs
    mix = _dot_tn(ot_ref[...], wt_ref[...])
    for o_ref, w_ref in zip(o_refs, w_refs):
        mix = mix + _dot(o_ref[...], w_ref[...])
    half = h_ref.shape[0] // 2
    xs = []
    for r in range(2):
        rows = slice(r * half, (r + 1) * half)
        xs.append(_layer_norm(ALPHA * h_ref[rows, :] + mix[rows], g1_ref[...], b1_ref[...]))
    for r in range(2):
        rows = slice(r * half, (r + 1) * half)
        out_ref[rows, :] = _swiglu_ln(xs[r], wg_ref, wu_ref, wo_ref, g2_ref, b2_ref)


def _ffn_proj_kernel(x_ref, wg_ref, wu_ref, wo_ref, g_ref, b_ref, w_ref, wt_ref,
                     h_ref, o_ref, qt_ref, vt_ref, *tail_ref, q_scale):
    half = x_ref.shape[0] // 2
    for r in range(2):
        rows = slice(r * half, (r + 1) * half)
        h_ref[rows, :] = _swiglu_ln(x_ref[rows, :], wg_ref, wu_ref, wo_ref, g_ref, b_ref)
    hb = h_ref[...].astype(BF16)
    width = o_ref.shape[-1]
    acc = _dot(hb, w_ref[...])
    o_ref[...] = acc[:, :width].astype(o_ref.dtype)
    for ref in tail_ref:
        ref[...] = acc[:, width:]
    acc_t = _dot_nt(wt_ref[...], hb)
    qw = qt_ref.shape[0]
    qt_ref[...] = (acc_t[:qw] * q_scale).astype(qt_ref.dtype)
    for j in range(vt_ref.shape[0]):
        vt_ref[j] = acc_t[qw:, j * KEY_BLOCK:(j + 1) * KEY_BLOCK].astype(vt_ref.dtype)


def _ffn_proj(x, w_in, w_out, layer, g, b, w, wq_t, wv_t, q_scale, tail=0, tm=512):
    bsz, s, _ = x.shape
    width, qw, vw = w.shape[1] - tail, wq_t.shape[0], wv_t.shape[0]
    const = lambda i, j: (0, 0)
    tok = lambda n: pl.BlockSpec((None, tm, n), lambda i, j: (i, j, 0))
    out_specs = [tok(D_MODEL), tok(width), pl.BlockSpec((None, qw, tm), lambda i, j: (i, 0, j)),
                 pl.BlockSpec((None, tm // KEY_BLOCK, vw, KEY_BLOCK), lambda i, j: (i, j, 0, 0))]
    out_shape = [jax.ShapeDtypeStruct((bsz, s, D_MODEL), F32), jax.ShapeDtypeStruct((bsz, s, width), BF16),
                 jax.ShapeDtypeStruct((bsz, qw, s), BF16),
                 jax.ShapeDtypeStruct((bsz, s // KEY_BLOCK, vw, KEY_BLOCK), BF16)]
    if tail:
        out_specs.append(tok(tail))
        out_shape.append(jax.ShapeDtypeStruct((bsz, s, tail), F32))
    return pl.pallas_call(
        functools.partial(_ffn_proj_kernel, q_scale=q_scale),
        grid=(bsz, s // tm),
        in_specs=[
            tok(D_MODEL),
            _resident((None, D_MODEL, D_FF), lambda i, j: (layer, 0, 0)),
            _resident((None, D_MODEL, D_FF), lambda i, j: (layer, 0, 1)),
            _resident((None, D_FF, D_MODEL), lambda i, j: (layer, 0, 0)),
            _resident((1, D_MODEL), const), _resident((1, D_MODEL), const),
            _resident((D_MODEL, width + tail), const),
            _resident((qw + vw, D_MODEL), const),
        ],
        out_specs=out_specs,
        out_shape=out_shape,
        compiler_params=_params(("parallel", "parallel")),
        name="ffn_proj",
    )(x, w_in, w_in, w_out, g, b, w, jnp.concatenate([wq_t, wv_t], axis=0))


def _mix_ffn_ln(o_t, o, h, w_mix, ln1, w_in, w_out, layer, ln2, tm=512):
    bsz, s, _ = h.shape
    wt = o_t.shape[1]
    const = lambda i, j: (0, 0)
    tok = lambda width: pl.BlockSpec((None, tm, width), lambda i, j: (i, j, 0))
    row = _resident((1, D_MODEL), const)
    normal = [] if o is None else [o]
    in_specs = ([pl.BlockSpec((None, wt, tm), lambda i, j: (i, 0, j))] + [tok(a.shape[2]) for a in normal]
                + [tok(D_MODEL), _resident((wt, D_MODEL), const)]
                + [_resident((a.shape[2], D_MODEL), lambda i, j: (wt // a.shape[2], 0)) for a in normal]
                + [row, row, _resident((None, D_MODEL, D_FF), lambda i, j: (layer, 0, 0)),
                   _resident((None, D_MODEL, D_FF), lambda i, j: (layer, 0, 1)),
                   _resident((None, D_FF, D_MODEL), lambda i, j: (layer, 0, 0)), row, row])
    args = [o_t] + normal + [h, w_mix] + [w_mix for _ in normal] + [*ln1, w_in, w_in, w_out, *ln2]
    return pl.pallas_call(
        functools.partial(_mix_ffn_ln_kernel, n_normal=len(normal)),
        grid=(bsz, s // tm),
        in_specs=in_specs,
        out_specs=tok(D_MODEL),
        out_shape=jax.ShapeDtypeStruct((bsz, s, D_MODEL), F32),
        compiler_params=_params(("parallel", "parallel")),
        name="mix_ffn_ln",
    )(*args)


def _head_pair(q):
    lo = lax.broadcasted_iota(jnp.int32, q.shape, 1) < HEAD_DIM
    zero = jnp.zeros_like(q)
    return jnp.where(lo, q, zero), jnp.where(lo, zero, q)


def _merge_pair(a0, a1):
    lo = lax.broadcasted_iota(jnp.int32, a0.shape, 1) < HEAD_DIM
    return jnp.where(lo, a0, a1)


def _head_operands_t(q_t):
    top = lax.broadcasted_iota(jnp.int32, (LANES, q_t.shape[1]), 0) < HEAD_DIM
    out = []
    for p in range(q_t.shape[0] // LANES):
        tile = q_t[p * LANES:(p + 1) * LANES]
        zero = jnp.zeros_like(tile)
        out += [jnp.where(top, tile, zero), jnp.where(top, zero, tile)]
    return out


def _key_rows(k_ref, kb, h):
    start = pl.multiple_of(kb * KEY_BLOCK, KEY_BLOCK)
    return k_ref[pl.ds(start, KEY_BLOCK), (h // 2) * LANES:(h // 2 + 1) * LANES]


def _value_rows_t(vt_ref, kb, h):
    return vt_ref[kb, h * HEAD_DIM:(h + 1) * HEAD_DIM, :]


ALL = slice(None)
LATE = slice(KEY_BLOCK, QUERY_TILE)


def _diagonal_masks(strict):
    row = lax.broadcasted_iota(jnp.int32, (KEY_BLOCK, QUERY_TILE), 0)
    col = lax.broadcasted_iota(jnp.int32, (KEY_BLOCK, QUERY_TILE), 1)
    full = row < col if strict else row <= col
    return full, full[:, :KEY_BLOCK]


def _widen(row_vec, late_part):
    return jnp.concatenate([row_vec[:, :KEY_BLOCK], late_part], axis=1)


def _attention_scratch(pairs):
    tile = (2 * pairs, KEY_BLOCK, QUERY_TILE)
    packed = (2 * pairs, KEY_BLOCK // 2, QUERY_TILE)
    return [pltpu.VMEM(tile, F32), pltpu.VMEM(tile, F32),
            pltpu.VMEM(packed, jnp.uint32), pltpu.VMEM(packed, jnp.uint32),
            pltpu.VMEM((2 * pairs, HEAD_DIM, QUERY_TILE), F32)]


def _sb_kernel(qt_ref, k_ref, vt_ref, o_ref, nza_ref, nzb_ref, wa_ref, wb_ref, acc_ref):
    qi = pl.program_id(2)
    w_heads = _head_operands_t(qt_ref[...])
    heads = range(len(w_heads))
    masks = _diagonal_masks(strict=True)
    row = lax.broadcasted_iota(jnp.int32, (KEY_BLOCK, KEY_BLOCK), 0)
    col = lax.broadcasted_iota(jnp.int32, (KEY_BLOCK, KEY_BLOCK), 1)
    suffix = jnp.where(col >= row, 1.0, 0.0).astype(BF16)
    last = 2 * qi + 1

    def scores(kb, nz_ref, cols=ALL):
        for h in heads:
            nz_ref[h, :, cols] = _dot(_key_rows(k_ref, kb, h), w_heads[h][:, cols])

    def accumulate(kb, w_ref, cols=ALL):
        for h in heads:
            acc_ref[h, :, cols] += _dot(_value_rows_t(vt_ref, kb, h), pltpu.bitcast(w_ref[h, :, cols], BF16))

    def weights(nz_ref, w_ref, carries, mask, cols=ALL):
        incl = []
        for h in heads:
            log_keep = _log2_sigmoid(nz_ref[h, :, cols])
            if mask is not None:
                log_keep = jnp.where(mask, log_keep, 0.0)
            incl.append(_dot(suffix, log_keep.astype(BF16)))
        for h in heads:
            w = jnp.exp2(incl[h] + carries[h][:, cols] - nz_ref[h, :, cols])
            if mask is not None:
                w = jnp.where(mask, w, 0.0)
            w_ref[h, :, cols] = pltpu.bitcast(w.astype(BF16), jnp.uint32)
        totals = [carries[h][:, cols] + incl[h][0:1, :] for h in heads]
        return totals if cols is ALL else [_widen(carries[h], totals[h]) for h in heads]

    def half_step(j, cur, oth, carries, *, mask=None, cols=ALL, prev_cols=ALL, first=False):
        kb = last - j
        scores(jnp.maximum(kb - 1, 0), oth[0])
        if not first:
            accumulate(kb + 1, oth[1], prev_cols)
        return weights(cur[0], cur[1], carries, mask, cols)

    slot_a, slot_b = (nza_ref, wa_ref), (nzb_ref, wb_ref)
    acc_ref[...] = jnp.zeros_like(acc_ref)
    scores(last, nza_ref, LATE)
    carries = [jnp.zeros((1, QUERY_TILE), F32) for _ in heads]
    carries = half_step(0, slot_a, slot_b, carries, mask=masks[1], cols=LATE, first=True)
    carries = half_step(1, slot_b, slot_a, carries, mask=masks[0], prev_cols=LATE)

    def body(i, carries):
        carries = half_step(2 * i + 2, slot_a, slot_b, carries)
        return half_step(2 * i + 3, slot_b, slot_a, carries)

    lax.fori_loop(0, qi, body, carries)
    accumulate(0, wb_ref)
    for h in heads:
        o_ref[h * HEAD_DIM:(h + 1) * HEAD_DIM, :] = acc_ref[h].astype(o_ref.dtype)


def _sb_attention(proj, q_t, v_t, pairs=4):
    b, s, _ = proj.shape
    groups = N_SB_PAIRS // pairs
    return pl.pallas_call(
        _sb_kernel,
        grid=(b, groups, s // QUERY_TILE),
        in_specs=[
            pl.BlockSpec((None, pairs * LANES, QUERY_TILE), lambda i, h, j: (i, h, j)),
            pl.BlockSpec((None, s, pairs * LANES), lambda i, h, j: (i, 0, h)),
            pl.BlockSpec((None, s // KEY_BLOCK, pairs * LANES, KEY_BLOCK), lambda i, h, j: (i, 0, h, 0)),
        ],
        out_specs=pl.BlockSpec((None, pairs * LANES, QUERY_TILE), lambda i, h, j: (i, h, j)),
        out_shape=jax.ShapeDtypeStruct((b, SB_W, s), BF16),
        scratch_shapes=_attention_scratch(pairs),
        compiler_params=_params(("parallel", "parallel", "arbitrary")),
        name="sb_attention",
    )(q_t, proj, v_t)


def _t5_bucket_table():
    n = np.arange(WINDOW)
    max_exact = N_BUCKETS // 2
    nf = np.maximum(n, 1).astype(np.float64)
    val = np.log(nf / max_exact) / math.log(MAX_DISTANCE / max_exact) * (N_BUCKETS - max_exact)
    frac = np.abs(val - np.round(val))[n > max_exact]
    assert frac.min() > 1e-3, "bucket boundary too close to an integer distance for float32"
    large = np.minimum(max_exact + val.astype(np.int64), N_BUCKETS - 1)
    return np.where(n < max_exact, n, large)


def _swa_bucket_tiles():
    table = _t5_bucket_table()
    qi = np.arange(WINDOW)[:, None]
    kj = np.arange(2 * WINDOW)[None, :]
    tiles = []
    for offset in (WINDOW, 0):
        rel = qi + offset - kj
        valid = (rel >= 0) & (rel < WINDOW)
        tiles.append(np.where(valid, table[np.clip(rel, 0, WINDOW - 1)], -1))
    return np.stack(tiles).astype(np.int32)


def _swa_bias_kernel(rb_ref, idx_ref, o_ref):
    h = pl.program_id(0)
    for tile in range(2):
        idx = idx_ref[tile]
        acc = jnp.full(idx.shape, NEG_BIG, F32)
        for bucket in range(N_BUCKETS):
            acc = jnp.where(idx == bucket, rb_ref[bucket, h], acc)
        o_ref[tile, 0] = acc


def _swa_bias(rel_bias):
    idx = jnp.asarray(_swa_bucket_tiles())
    return pl.pallas_call(
        _swa_bias_kernel,
        grid=(N_SWA_HEADS,),
        in_specs=[
            pl.BlockSpec(memory_space=pltpu.SMEM),
            pl.BlockSpec((2, WINDOW, 2 * WINDOW), lambda h: (0, 0, 0)),
        ],
        out_specs=pl.BlockSpec((2, 1, WINDOW, 2 * WINDOW), lambda h: (0, h, 0, 0)),
        out_shape=jax.ShapeDtypeStruct((2, N_SWA_HEADS, WINDOW, 2 * WINDOW), F32),
        compiler_params=_params(("arbitrary",)),
        name="swa_bias",
    )(rel_bias, idx)


def _swa_kernel(sink_ref, q_ref, k_ref, v_ref, bias_ref, o_ref, *, tq):
    qt = pl.program_id(1)
    scale = jnp.asarray(QK_SCALE, BF16)
    group = N_SWA_HEADS // N_SWA_KV
    for j in range(tq // WINDOW):
        nb = qt * (tq // WINDOW) + j
        first = (nb == 0).astype(jnp.int32) if j == 0 else 0
        w0 = pl.multiple_of(jnp.maximum(nb - 1, 0) * WINDOW, WINDOW)
        rows = slice(j * WINDOW, (j + 1) * WINDOW)
        for g in range(N_SWA_KV):
            kwin = k_ref[pl.ds(w0, 2 * WINDOW), g * LANES:(g + 1) * LANES]
            vwin = v_ref[pl.ds(w0, 2 * WINDOW), g * LANES:(g + 1) * LANES]
            for c in range(group // 2):
                pair = g * (group // 2) + c
                cols = slice(pair * LANES, (pair + 1) * LANES)
                q_heads = _head_pair(q_ref[rows, cols] * scale)
                outs = []
                for i in range(2):
                    head = 2 * pair + i
                    sink = sink_ref[head]
                    logits = _dot_nt(q_heads[i], kwin) + bias_ref[first, head]
                    m = jnp.maximum(jnp.max(logits, axis=-1, keepdims=True), sink)
                    p = jnp.exp(logits - m)
                    denom = jnp.sum(p, axis=-1, keepdims=True) + jnp.exp(sink - m)
                    outs.append(_dot(p.astype(BF16), vwin) / denom)
                o_ref[rows, cols] = _merge_pair(outs[0], outs[1]).astype(o_ref.dtype)


def _swa_attention(proj, bias, sinks, tq=512):
    b, s, width = proj.shape
    q_blk = (width - SWA_QW - 4 * LANES) // SWA_QW
    k_blk = (width - 4 * LANES) // (2 * LANES)
    return pl.pallas_call(
        functools.partial(_swa_kernel, tq=tq),
        grid=(b, s // tq),
        in_specs=[
            pl.BlockSpec(memory_space=pltpu.SMEM),
            pl.BlockSpec((None, tq, SWA_QW), lambda i, j: (i, j, q_blk)),
            pl.BlockSpec((None, s, 2 * LANES), lambda i, j: (i, 0, k_blk)),
            pl.BlockSpec((None, s, 2 * LANES), lambda i, j: (i, 0, k_blk + 1)),
            _resident((2, N_SWA_HEADS, WINDOW, 2 * WINDOW), lambda i, j: (0, 0, 0, 0)),
        ],
        out_specs=pl.BlockSpec((None, tq, SWA_QW), lambda i, j: (i, j, 0)),
        out_shape=jax.ShapeDtypeStruct((b, s, SWA_QW), BF16),
        compiler_params=_params(("parallel", "arbitrary")),
        name="swa_attention",
    )(sinks, proj, proj, proj, bias)


def _fox_gate_kernel(f_ref, bf_ref, c_ref, *, chunk):
    s = f_ref.shape[0]
    r = lax.broadcasted_iota(jnp.int32, (chunk, chunk), 0)
    c = lax.broadcasted_iota(jnp.int32, (chunk, chunk), 1)
    upto = jnp.where(c <= r, 1.0, 0.0).astype(BF16)
    lane = lax.broadcasted_iota(jnp.int32, (chunk, LANES), 1)
    carry = jnp.zeros((1, LANES), F32)
    for j in range(s // chunk):
        rows = slice(j * chunk, (j + 1) * chunk)
        log_f = _log2_sigmoid((f_ref[rows, :] + bf_ref[...]) * LOG2E)
        cs = carry
        for part in _split_bf16(log_f, 3):
            cs = cs + _dot(upto, part)
        carry = cs[chunk - 1:chunk, :]
        out = jnp.zeros((chunk, LANES), BF16)
        for k, part in reversed(list(enumerate(_split_bf16(cs, GATE_PARTS)))):
            out = jnp.where(lane < (k + 1) * N_FOX_HEADS, part, out)
        c_ref[rows, :] = out


def _fox_gates(f, b_f, chunk=256):
    b, s, _ = f.shape
    return pl.pallas_call(
        functools.partial(_fox_gate_kernel, chunk=chunk),
        grid=(b,),
        in_specs=[
            pl.BlockSpec((None, s, LANES), lambda i: (i, 0, 0)),
            _resident((1, LANES), lambda i: (0, 0)),
        ],
        out_specs=pl.BlockSpec((None, s, LANES), lambda i: (i, 0, 0)),
        out_shape=jax.ShapeDtypeStruct((b, s, LANES), BF16),
        compiler_params=_params(("parallel",)),
        name="fox_gates",
    )(f, b_f)


def _fox_kernel(qt_ref, k_ref, vt_ref, c_ref, o_ref, sa_ref, sb_ref, pa_ref, pb_ref, acc_ref):
    qi = pl.program_id(2)
    q_heads = _head_operands_t(qt_ref[...])
    heads = range(len(q_heads))
    first_head = pl.program_id(1) * len(q_heads)
    gate_row = lax.broadcasted_iota(jnp.int32, (LANES, QUERY_TILE), 0)
    masks = _diagonal_masks(strict=False)
    w_heads = []
    for h in heads:
        picks = ((gate_row & (N_FOX_HEADS - 1)) == first_head + h) & (gate_row < GATE_PARTS * N_FOX_HEADS)
        minus_gate = jnp.where(picks, -1.0, 0.0).astype(BF16)
        w_heads.append(jnp.concatenate([q_heads[h], minus_gate], axis=0))

    def scores(kb, s_ref, cols=ALL):
        start = pl.multiple_of(kb * KEY_BLOCK, KEY_BLOCK)
        gates = c_ref[pl.ds(start, KEY_BLOCK), :]
        for h in heads:
            lhs = jnp.concatenate([_key_rows(k_ref, kb, h), gates], axis=1)
            s_ref[h, :, cols] = _dot(lhs, w_heads[h][:, cols])

    ones_rows = jnp.ones((16, KEY_BLOCK), BF16)

    def accumulate(kb, p_ref, a, ls, cols=ALL):
        new_ls = []
        for h in heads:
            lhs = jnp.concatenate([_value_rows_t(vt_ref, kb, h), ones_rows], axis=0)
            pv = _dot(lhs, pltpu.bitcast(p_ref[h, :, cols], BF16))
            acc_ref[h, :, cols] = a[h] * acc_ref[h, :, cols] + pv[:HEAD_DIM]
            l_new = a[h] * ls[h][:, cols] + pv[HEAD_DIM:HEAD_DIM + 1]
            new_ls.append(l_new if cols is ALL else _widen(ls[h], l_new))
        return new_ls

    def softmax(s_ref, p_ref, ms, mask, cols=ALL):
        new_ms, a = [], []
        for h in heads:
            m = ms[h][:, cols]
            s2 = s_ref[h, :, cols] if mask is None else jnp.where(mask, s_ref[h, :, cols], NEG_BIG)
            m_new = jnp.maximum(m, jnp.max(s2, axis=0, keepdims=True))
            p_ref[h, :, cols] = pltpu.bitcast(jnp.exp2(s2 - m_new).astype(BF16), jnp.uint32)
            new_ms.append(m_new if cols is ALL else _widen(ms[h], m_new))
            a.append(jnp.exp2(m - m_new))
        return new_ms, a

    def half_step(kb, cur, oth, state, *, mask=None, cols=ALL, next_cols=ALL):
        ms, ls, a_prev = state
        if next_cols is not None:
            scores(kb + 1, oth[0], next_cols)
        ls = accumulate(jnp.maximum(kb - 1, 0), oth[1], a_prev, ls)
        ms, a = softmax(cur[0], cur[1], ms, mask, cols)
        return ms, ls, a

    slot_a, slot_b = (sa_ref, pa_ref), (sb_ref, pb_ref)
    pb_ref[...] = jnp.zeros_like(pb_ref)
    acc_ref[...] = jnp.zeros_like(acc_ref)
    scores(0, sa_ref)
    state = ([jnp.full((1, QUERY_TILE), NEG_BIG, F32) for _ in heads],
             [jnp.zeros((1, QUERY_TILE), F32) for _ in heads],
             [jnp.ones((1, QUERY_TILE), F32) for _ in heads])

    def body(i, state):
        state = half_step(2 * i, slot_a, slot_b, state)
        return half_step(2 * i + 1, slot_b, slot_a, state)

    state = lax.fori_loop(0, qi, body, state)
    state = half_step(2 * qi, slot_a, slot_b, state, mask=masks[0], next_cols=LATE)
    _, ls, a = half_step(2 * qi + 1, slot_b, slot_a, state, mask=masks[1], cols=LATE, next_cols=None)
    ls = accumulate(2 * qi + 1, pb_ref, a, ls, LATE)
    for h in heads:
        o_ref[h * HEAD_DIM:(h + 1) * HEAD_DIM, :] = (acc_ref[h] / ls[h]).astype(o_ref.dtype)


def _fox_attention(proj, q_t, v_t, gates, pairs=4):
    b, s, _ = proj.shape
    groups = N_FOX_PAIRS // pairs
    return pl.pallas_call(
        _fox_kernel,
        grid=(b, groups, s // QUERY_TILE),
        in_specs=[
            pl.BlockSpec((None, pairs * LANES, QUERY_TILE), lambda i, h, j: (i, h, j)),
            pl.BlockSpec((None, s, pairs * LANES), lambda i, h, j: (i, 0, h)),
            pl.BlockSpec((None, s // KEY_BLOCK, pairs * LANES, KEY_BLOCK), lambda i, h, j: (i, 0, h, 0)),
            pl.BlockSpec((None, s, LANES), lambda i, h, j: (i, 0, 0)),
        ],
        out_specs=pl.BlockSpec((None, pairs * LANES, QUERY_TILE), lambda i, h, j: (i, h, j)),
        out_shape=jax.ShapeDtypeStruct((b, FOX_W, s), BF16),
        scratch_shapes=_attention_scratch(pairs),
        compiler_params=_params(("parallel", "parallel", "arbitrary")),
        name="fox_attention",
    )(q_t, proj, v_t, gates)


def _even_projection(w_in):
    kv0 = 3 * SB_W + SWA_QW
    dup = lambda w: jnp.repeat(w.reshape(D_MODEL, N_SWA_KV, HEAD_DIM), 2, axis=1).reshape(D_MODEL, -1)
    w_cat = jnp.concatenate(
        [w_in[:, SB_W:2 * SB_W], w_in[:, 3 * SB_W:kv0],
         dup(w_in[:, kv0:kv0 + SWA_KVW]), dup(w_in[:, kv0 + SWA_KVW:])], axis=1)
    return (w_cat.astype(BF16), w_in[:, :SB_W].T.astype(BF16), w_in[:, 2 * SB_W:3 * SB_W].T.astype(BF16),
            -QK_SCALE * LOG2E)


def _even_attention(proj, q_t, v_t, sinks, rel_bias):
    return _sb_attention(proj, q_t, v_t), _swa_attention(proj, _swa_bias(rel_bias), sinks.astype(F32))


def _spread_gates(v):
    return jnp.pad(jnp.tile(v, (1, GATE_PARTS)), ((0, 0), (0, LANES - GATE_PARTS * N_FOX_HEADS)))


def _odd_projection(w_in):
    w_cat = jnp.concatenate([w_in[:, FOX_W:2 * FOX_W], _spread_gates(w_in[:, 3 * FOX_W:])], axis=1)
    return (w_cat.astype(BF16), w_in[:, :FOX_W].T.astype(BF16), w_in[:, 2 * FOX_W:3 * FOX_W].T.astype(BF16),
            QK_SCALE * LOG2E)


def _odd_attention(proj, q_t, v_t, f, b_f):
    gates = _fox_gates(f, _spread_gates(b_f.reshape(1, -1).astype(F32)))
    return _fox_attention(proj, q_t, v_t, gates), None


def kernel(x, ln_g, ln_b, ffn1_in, ffn1_out, ffn2_in, ffn2_out, ab_w_in, ab_w_out, ab_sinks,
           fox_w_in, fox_b_f, fox_w_out, rel_bias):
    bsz, s, d = x.shape
    assert (s, d) == (SEQ, D_MODEL) and ln_g.shape[0] == DEPTH
    row = lambda v: v.reshape(1, D_MODEL).astype(F32)
    ffn1 = ffn1_in.astype(BF16), ffn1_out.astype(BF16)
    ffn2 = ffn2_in.astype(BF16), ffn2_out.astype(BF16)
    h = x
    for layer in range(DEPTH):
        ln = [(row(ln_g[layer, k]), row(ln_b[layer, k])) for k in range(3)]
        i = layer // 2
        if layer % 2 == 0:
            h, *p = _ffn_proj(h, *ffn1, layer, *ln[0], *_even_projection(ab_w_in[i]))
            o_t, o = _even_attention(*p, ab_sinks[i], rel_bias)
            w_mix = ab_w_out[i]
        else:
            h, *p = _ffn_proj(h, *ffn1, layer, *ln[0], *_odd_projection(fox_w_in[i]), tail=LANES)
            o_t, o = _odd_attention(*p, fox_b_f[i])
            w_mix = fox_w_out[i]
        h = _mix_ffn_ln(o_t, o, h, w_mix.astype(BF16), ln[1], *ffn2, layer, ln[2])
    return h
```

```python
import functools
import math

import numpy as np
import jax
import jax.numpy as jnp
from jax import lax
from jax.experimental import pallas as pl
from jax.experimental.pallas import tpu as pltpu

D_MODEL = 1024
SEQ = 2048
DEPTH = 2
HEAD_DIM = 64
LANES = 128
N_SB_PAIRS = 4
N_SWA_HEADS = 8
N_SWA_KV = 2
N_FOX_HEADS = 16
N_FOX_PAIRS = 8
WINDOW = 128
N_BUCKETS = 32
MAX_DISTANCE = 128
D_FF = 2816
ALPHA = (2 * DEPTH) ** 0.25
LN_EPS = 1e-5
SB_W = 512
SWA_QW = 512
SWA_KVW = 128
FOX_W = 1024
NEG_BIG = -1e30
LOG2E = math.log2(math.e)
QK_SCALE = HEAD_DIM ** -0.5
GATE_PARTS = 3
KEY_BLOCK = 256
QUERY_TILE = 2 * KEY_BLOCK
VMEM_LIMIT = 56 * 1024 * 1024

BF16 = jnp.bfloat16
F32 = jnp.float32


def _dot(a, b):
    return jnp.dot(a, b, preferred_element_type=F32)


def _dot_nt(a, b):
    return lax.dot_general(a, b, (((1,), (1,)), ((), ())), preferred_element_type=F32)


def _dot_tn(a, b):
    return lax.dot_general(a, b, (((0,), (0,)), ((), ())), preferred_element_type=F32)


def _layer_norm(z, g, b):
    mu = jnp.mean(z, axis=-1, keepdims=True)
    d = z - mu
    var = jnp.mean(d * d, axis=-1, keepdims=True)
    return d * lax.rsqrt(var + LN_EPS) * g + b


def _log2_sigmoid(x2):
    sign = jnp.uint32(0x80000000)
    neg_abs = lax.bitcast_convert_type(lax.bitcast_convert_type(x2, jnp.uint32) | sign, F32)
    return jnp.minimum(x2, 0.0) - jnp.log2(1.0 + jnp.exp2(neg_abs))


def _split_bf16(x, parts):
    out = []
    r = x
    for i in range(parts):
        t = r.astype(BF16)
        out.append(t)
        if i + 1 < parts:
            r = r - t.astype(F32)
    return out


def _params(sem):
    return pltpu.CompilerParams(dimension_semantics=sem, vmem_limit_bytes=VMEM_LIMIT)


def _resident(shape, index_map):
    return pl.BlockSpec(shape, index_map, pipeline_mode=pl.Buffered(1))


def _swiglu_ln(x, wg_ref, wu_ref, wo_ref, g_ref, b_ref):
    xb = x.astype(BF16)
    gate = _dot(xb, wg_ref[...])
    up = _dot(xb, wu_ref[...])
    h = (gate * jax.nn.sigmoid(gate) * up).astype(BF16)
    y = _dot(h, wo_ref[...])
    return _layer_norm(ALPHA * x + 0.5 * y, g_ref[...], b_ref[...])


def _mix_ffn_ln_kernel(*refs, n_normal):
    ot_ref, refs = refs[0], refs[1:]
    o_refs, refs = refs[:n_normal], refs[n_normal:]
    h_ref, wt_ref, refs = refs[0], refs[1], refs[2:]
    w_refs, refs = refs[:n_normal], refs[n_normal:]
    g1_ref, b1_ref, wg_ref, wu_ref, wo_ref, g2_ref, b2_ref, out_ref = refs
    mix = _dot_tn(ot_ref[...], wt_ref[...])
    for o_ref, w_ref in zip(o_refs, w_refs):
        mix = mix + _dot(o_ref[...], w_ref[...])
    half = h_ref.shape[0] // 2
    xs = []
    for r in range(2):
        rows = slice(r * half, (r + 1) * half)
        xs.append(_layer_norm(ALPHA * h_ref[rows, :] + mix[rows], g1_ref[...], b1_ref[...]))
    for r in range(2):
        rows = slice(r * half, (r + 1) * half)
        out_ref[rows, :] = _swiglu_ln(xs[r], wg_ref, wu_ref, wo_ref, g2_ref, b2_ref)


def _ffn_proj_kernel(x_ref, wg_ref, wu_ref, wo_ref, g_ref, b_ref, w_ref, wt_ref,
                     h_ref, o_ref, qt_ref, vt_ref, *tail_ref, q_scale):
    half = x_ref.shape[0] // 2
    for r in range(2):
        rows = slice(r * half, (r + 1) * half)
        h_ref[rows, :] = _swiglu_ln(x_ref[rows, :], wg_ref, wu_ref, wo_ref, g_ref, b_ref)
    hb = h_ref[...].astype(BF16)
    width = o_ref.shape[-1]
    acc = _dot(hb, w_ref[...])
    o_ref[...] = acc[:, :width].astype(o_ref.dtype)
    for ref in tail_ref:
        ref[...] = acc[:, width:]
    acc_t = _dot_nt(wt_ref[...], hb)
    qw = qt_ref.shape[0]
    qt_ref[...] = (acc_t[:qw] * q_scale).astype(qt_ref.dtype)
    for j in range(vt_ref.shape[0]):
        vt_ref[j] = acc_t[qw:, j * KEY_BLOCK:(j + 1) * KEY_BLOCK].astype(vt_ref.dtype)


def _ffn_proj(x, w_in, w_out, layer, g, b, w, wq_t, wv_t, q_scale, tail=0, tm=512):
    bsz, s, _ = x.shape
    width, qw, vw = w.shape[1] - tail, wq_t.shape[0], wv_t.shape[0]
    const = lambda i, j: (0, 0)
    tok = lambda n: pl.BlockSpec((None, tm, n), lambda i, j: (i, j, 0))
    out_specs = [tok(D_MODEL), tok(width), pl.BlockSpec((None, qw, tm), lambda i, j: (i, 0, j)),
                 pl.BlockSpec((None, tm // KEY_BLOCK, vw, KEY_BLOCK), lambda i, j: (i, j, 0, 0))]
    out_shape = [jax.ShapeDtypeStruct((bsz, s, D_MODEL), F32), jax.ShapeDtypeStruct((bsz, s, width), BF16),
                 jax.ShapeDtypeStruct((bsz, qw, s), BF16),
                 jax.ShapeDtypeStruct((bsz, s // KEY_BLOCK, vw, KEY_BLOCK), BF16)]
    if tail:
        out_specs.append(tok(tail))
        out_shape.append(jax.ShapeDtypeStruct((bsz, s, tail), F32))
    return pl.pallas_call(
        functools.partial(_ffn_proj_kernel, q_scale=q_scale),
        grid=(bsz, s // tm),
        in_specs=[
            tok(D_MODEL),
            _resident((None, D_MODEL, D_FF), lambda i, j: (layer, 0, 0)),
            _resident((None, D_MODEL, D_FF), lambda i, j: (layer, 0, 1)),
            _resident((None, D_FF, D_MODEL), lambda i, j: (layer, 0, 0)),
            _resident((1, D_MODEL), const), _resident((1, D_MODEL), const),
            _resident((D_MODEL, width + tail), const),
            _resident((qw + vw, D_MODEL), const),
        ],
        out_specs=out_specs,
        out_shape=out_shape,
        compiler_params=_params(("parallel", "parallel")),
        name="ffn_proj",
    )(x, w_in, w_in, w_out, g, b, w, jnp.concatenate([wq_t, wv_t], axis=0))


def _mix_ffn_ln(o_t, o, h, w_mix, ln1, w_in, w_out, layer, ln2, tm=512):
    bsz, s, _ = h.shape
    wt = o_t.shape[1]
    const = lambda i, j: (0, 0)
    tok = lambda width: pl.BlockSpec((None, tm, width), lambda i, j: (i, j, 0))
    row = _resident((1, D_MODEL), const)
    normal = [] if o is None else [o]
    in_specs = ([pl.BlockSpec((None, wt, tm), lambda i, j: (i, 0, j))] + [tok(a.shape[2]) for a in normal]
                + [tok(D_MODEL), _resident((wt, D_MODEL), const)]
                + [_resident((a.shape[2], D_MODEL), lambda i, j: (wt // a.shape[2], 0)) for a in normal]
                + [row, row, _resident((None, D_MODEL, D_FF), lambda i, j: (layer, 0, 0)),
                   _resident((None, D_MODEL, D_FF), lambda i, j: (layer, 0, 1)),
                   _resident((None, D_FF, D_MODEL), lambda i, j: (layer, 0, 0)), row, row])
    args = [o_t] + normal + [h, w_mix] + [w_mix for _ in normal] + [*ln1, w_in, w_in, w_out, *ln2]
    return pl.pallas_call(
        functools.partial(_mix_ffn_ln_kernel, n_normal=len(normal)),
        grid=(bsz, s // tm),
        in_specs=in_specs,
        out_specs=tok(D_MODEL),
        out_shape=jax.ShapeDtypeStruct((bsz, s, D_MODEL), F32),
        compiler_params=_params(("parallel", "parallel")),
        name="mix_ffn_ln",
    )(*args)


def _head_pair(q):
    lo = lax.broadcasted_iota(jnp.int32, q.shape, 1) < HEAD_DIM
    zero = jnp.zeros_like(q)
    return jnp.where(lo, q, zero), jnp.where(lo, zero, q)


def _merge_pair(a0, a1):
    lo = lax.broadcasted_iota(jnp.int32, a0.shape, 1) < HEAD_DIM
    return jnp.where(lo, a0, a1)


def _head_operands_t(q_t):
    top = lax.broadcasted_iota(jnp.int32, (LANES, q_t.shape[1]), 0) < HEAD_DIM
    out = []
    for p in range(q_t.shape[0] // LANES):
        tile = q_t[p * LANES:(p + 1) * LANES]
        zero = jnp.zeros_like(tile)
        out += [jnp.where(top, tile, zero), jnp.where(top, zero, tile)]
    return out


def _key_rows(k_ref, kb, h):
    start = pl.multiple_of(kb * KEY_BLOCK, KEY_BLOCK)
    return k_ref[pl.ds(start, KEY_BLOCK), (h // 2) * LANES:(h // 2 + 1) * LANES]


def _value_rows_t(vt_ref, kb, h):
    return vt_ref[kb, h * HEAD_DIM:(h + 1) * HEAD_DIM, :]


ALL = slice(None)
LATE = slice(KEY_BLOCK, QUERY_TILE)


def _diagonal_masks(strict):
    row = lax.broadcasted_iota(jnp.int32, (KEY_BLOCK, QUERY_TILE), 0)
    col = lax.broadcasted_iota(jnp.int32, (KEY_BLOCK, QUERY_TILE), 1)
    full = row < col if strict else row <= col
    return full, full[:, :KEY_BLOCK]


def _widen(row_vec, late_part):
    return jnp.concatenate([row_vec[:, :KEY_BLOCK], late_part], axis=1)


def _attention_scratch(pairs):
    tile = (2 * pairs, KEY_BLOCK, QUERY_TILE)
    packed = (2 * pairs, KEY_BLOCK // 2, QUERY_TILE)
    return [pltpu.VMEM(tile, F32), pltpu.VMEM(tile, F32),
            pltpu.VMEM(packed, jnp.uint32), pltpu.VMEM(packed, jnp.uint32),
            pltpu.VMEM((2 * pairs, HEAD_DIM, QUERY_TILE), F32)]


def _sb_kernel(qt_ref, k_ref, vt_ref, o_ref, nza_ref, nzb_ref, wa_ref, wb_ref, acc_ref):
    qi = pl.program_id(2)
    w_heads = _head_operands_t(qt_ref[...])
    heads = range(len(w_heads))
    masks = _diagonal_masks(strict=True)
    row = lax.broadcasted_iota(jnp.int32, (KEY_BLOCK, KEY_BLOCK), 0)
    col = lax.broadcasted_iota(jnp.int32, (KEY_BLOCK, KEY_BLOCK), 1)
    suffix = jnp.where(col >= row, 1.0, 0.0).astype(BF16)
    last = 2 * qi + 1

    def scores(kb, nz_ref, cols=ALL):
        for h in heads:
            nz_ref[h, :, cols] = _dot(_key_rows(k_ref, kb, h), w_heads[h][:, cols])

    def accumulate(kb, w_ref, cols=ALL):
        for h in heads:
            acc_ref[h, :, cols] += _dot(_value_rows_t(vt_ref, kb, h), pltpu.bitcast(w_ref[h, :, cols], BF16))

    def weights(nz_ref, w_ref, carries, mask, cols=ALL):
        incl = []
        for h in heads:
            log_keep = _log2_sigmoid(nz_ref[h, :, cols])
            if mask is not None:
                log_keep = jnp.where(mask, log_keep, 0.0)
            incl.append(_dot(suffix, log_keep.astype(BF16)))
        for h in heads:
            w = jnp.exp2(incl[h] + carries[h][:, cols] - nz_ref[h, :, cols])
            if mask is not None:
                w = jnp.where(mask, w, 0.0)
            w_ref[h, :, cols] = pltpu.bitcast(w.astype(BF16), jnp.uint32)
        totals = [carries[h][:, cols] + incl[h][0:1, :] for h in heads]
        return totals if cols is ALL else [_widen(carries[h], totals[h]) for h in heads]

    def half_step(j, cur, oth, carries, *, mask=None, cols=ALL, prev_cols=ALL, first=False):
        kb = last - j
        scores(jnp.maximum(kb - 1, 0), oth[0])
        if not first:
            accumulate(kb + 1, oth[1], prev_cols)
        return weights(cur[0], cur[1], carries, mask, cols)

    slot_a, slot_b = (nza_ref, wa_ref), (nzb_ref, wb_ref)
    acc_ref[...] = jnp.zeros_like(acc_ref)
    scores(last, nza_ref, LATE)
    carries = [jnp.zeros((1, QUERY_TILE), F32) for _ in heads]
    carries = half_step(0, slot_a, slot_b, carries, mask=masks[1], cols=LATE, first=True)
    carries = half_step(1, slot_b, slot_a, carries, mask=masks[0], prev_cols=LATE)

    def body(i, carries):
        carries = half_step(2 * i + 2, slot_a, slot_b, carries)
        return half_step(2 * i + 3, slot_b, slot_a, carries)

    lax.fori_loop(0, qi, body, carries)
    accumulate(0, wb_ref)
    for h in heads:
        o_ref[h * HEAD_DIM:(h + 1) * HEAD_DIM, :] = acc_ref[h].astype(o_ref.dtype)


def _sb_attention(proj, q_t, v_t, pairs=4):
    b, s, _ = proj.shape
    groups = N_SB_PAIRS // pairs
    return pl.pallas_call(
        _sb_kernel,
        grid=(b, groups, s // QUERY_TILE),
        in_specs=[
            pl.BlockSpec((None, pairs * LANES, QUERY_TILE), lambda i, h, j: (i, h, j)),
            pl.BlockSpec((None, s, pairs * LANES), lambda i, h, j: (i, 0, h)),
            pl.BlockSpec((None, s // KEY_BLOCK, pairs * LANES, KEY_BLOCK), lambda i, h, j: (i, 0, h, 0)),
        ],
        out_specs=pl.BlockSpec((None, pairs * LANES, QUERY_TILE), lambda i, h, j: (i, h, j)),
        out_shape=jax.ShapeDtypeStruct((b, SB_W, s), BF16),
        scratch_shapes=_attention_scratch(pairs),
        compiler_params=_params(("parallel", "parallel", "arbitrary")),
        name="sb_attention",
    )(q_t, proj, v_t)


def _t5_bucket_table():
    n = np.arange(WINDOW)
    max_exact = N_BUCKETS // 2
    nf = np.maximum(n, 1).astype(np.float64)
    val = np.log(nf / max_exact) / math.log(MAX_DISTANCE / max_exact) * (N_BUCKETS - max_exact)
    frac = np.abs(val - np.round(val))[n > max_exact]
    assert frac.min() > 1e-3, "bucket boundary too close to an integer distance for float32"
    large = np.minimum(max_exact + val.astype(np.int64), N_BUCKETS - 1)
    return np.where(n < max_exact, n, large)


def _swa_bucket_tiles():
    table = _t5_bucket_table()
    qi = np.arange(WINDOW)[:, None]
    kj = np.arange(2 * WINDOW)[None, :]
    tiles = []
    for offset in (WINDOW, 0):
        rel = qi + offset - kj
        valid = (rel >= 0) & (rel < WINDOW)
        tiles.append(np.where(valid, table[np.clip(rel, 0, WINDOW - 1)], -1))
    return np.stack(tiles).astype(np.int32)


def _swa_bias_kernel(rb_ref, idx_ref, o_ref):
    h = pl.program_id(0)
    for tile in range(2):
        idx = idx_ref[tile]
        acc = jnp.full(idx.shape, NEG_BIG, F32)
        for bucket in range(N_BUCKETS):
            acc = jnp.where(idx == bucket, rb_ref[bucket, h], acc)
        o_ref[tile, 0] = acc


def _swa_bias(rel_bias):
    idx = jnp.asarray(_swa_bucket_tiles())
    return pl.pallas_call(
        _swa_bias_kernel,
        grid=(N_SWA_HEADS,),
        in_specs=[
            pl.BlockSpec(memory_space=pltpu.SMEM),
            pl.BlockSpec((2, WINDOW, 2 * WINDOW), lambda h: (0, 0, 0)),
        ],
        out_specs=pl.BlockSpec((2, 1, WINDOW, 2 * WINDOW), lambda h: (0, h, 0, 0)),
        out_shape=jax.ShapeDtypeStruct((2, N_SWA_HEADS, WINDOW, 2 * WINDOW), F32),
        compiler_params=_params(("arbitrary",)),
        name="swa_bias",
    )(rel_bias, idx)


def _swa_kernel(sink_ref, q_ref, k_ref, v_ref, bias_ref, o_ref, *, tq):
    qt = pl.program_id(1)
    scale = jnp.asarray(QK_SCALE, BF16)
    group = N_SWA_HEADS // N_SWA_KV
    head_of_row = lax.broadcasted_iota(jnp.int32, (group * WINDOW, 1), 0) // WINDOW
    for j in range(tq // WINDOW):
        nb = qt * (tq // WINDOW) + j
        first = (nb == 0).astype(jnp.int32) if j == 0 else 0
        w0 = pl.multiple_of(jnp.maximum(nb - 1, 0) * WINDOW, WINDOW)
        rows = slice(j * WINDOW, (j + 1) * WINDOW)
        for g in range(N_SWA_KV):
            kwin = k_ref[pl.ds(w0, 2 * WINDOW), g * LANES:(g + 1) * LANES]
            vwin = v_ref[pl.ds(w0, 2 * WINDOW), g * LANES:(g + 1) * LANES]
            pairs = [g * (group // 2) + c for c in range(group // 2)]
            q_rows = []
            for pair in pairs:
                q_rows += _head_pair(q_ref[rows, pair * LANES:(pair + 1) * LANES] * scale)
            sink = jnp.zeros((group * WINDOW, 1), F32)
            for i in range(group):
                sink = jnp.where(head_of_row == i, sink_ref[g * group + i], sink)
            bias = bias_ref[first, g * group:(g + 1) * group].reshape(group * WINDOW, 2 * WINDOW)
            logits = _dot_nt(jnp.concatenate(q_rows, axis=0), kwin) + bias
            m = jnp.maximum(jnp.max(logits, axis=-1, keepdims=True), sink)
            p = jnp.exp(logits - m)
            denom = jnp.sum(p, axis=-1, keepdims=True) + jnp.exp(sink - m)
            out = _dot(p.astype(BF16), vwin) / denom
            for c, pair in enumerate(pairs):
                lo, hi = out[2 * c * WINDOW:(2 * c + 1) * WINDOW], out[(2 * c + 1) * WINDOW:(2 * c + 2) * WINDOW]
                o_ref[rows, pair * LANES:(pair + 1) * LANES] = _merge_pair(lo, hi).astype(o_ref.dtype)


def _swa_attention(proj, bias, sinks, tq=512):
    b, s, width = proj.shape
    q_blk = (width - SWA_QW - 4 * LANES) // SWA_QW
    k_blk = (width - 4 * LANES) // (2 * LANES)
    return pl.pallas_call(
        functools.partial(_swa_kernel, tq=tq),
        grid=(b, s // tq),
        in_specs=[
            pl.BlockSpec(memory_space=pltpu.SMEM),
            pl.BlockSpec((None, tq, SWA_QW), lambda i, j: (i, j, q_blk)),
            pl.BlockSpec((None, s, 2 * LANES), lambda i, j: (i, 0, k_blk)),
            pl.BlockSpec((None, s, 2 * LANES), lambda i, j: (i, 0, k_blk + 1)),
            _resident((2, N_SWA_HEADS, WINDOW, 2 * WINDOW), lambda i, j: (0, 0, 0, 0)),
        ],
        out_specs=pl.BlockSpec((None, tq, SWA_QW), lambda i, j: (i, j, 0)),
        out_shape=jax.ShapeDtypeStruct((b, s, SWA_QW), BF16),
        compiler_params=_params(("parallel", "arbitrary")),
        name="swa_attention",
    )(sinks, proj, proj, proj, bias)


def _fox_gate_kernel(f_ref, bf_ref, c_ref, *, chunk):
    s = f_ref.shape[0]
    r = lax.broadcasted_iota(jnp.int32, (chunk, chunk), 0)
    c = lax.broadcasted_iota(jnp.int32, (chunk, chunk), 1)
    upto = jnp.where(c <= r, 1.0, 0.0).astype(BF16)
    lane = lax.broadcasted_iota(jnp.int32, (chunk, LANES), 1)
    carry = jnp.zeros((1, LANES), F32)
    for j in range(s // chunk):
        rows = slice(j * chunk, (j + 1) * chunk)
        log_f = _log2_sigmoid((f_ref[rows, :] + bf_ref[...]) * LOG2E)
        cs = carry
        for part in _split_bf16(log_f, 3):
            cs = cs + _dot(upto, part)
        carry = cs[chunk - 1:chunk, :]
        out = jnp.zeros((chunk, LANES), BF16)
        for k, part in reversed(list(enumerate(_split_bf16(cs, GATE_PARTS)))):
            out = jnp.where(lane < (k + 1) * N_FOX_HEADS, part, out)
        c_ref[rows, :] = out


def _fox_gates(f, b_f, chunk=256):
    b, s, _ = f.shape
    return pl.pallas_call(
        functools.partial(_fox_gate_kernel, chunk=chunk),
        grid=(b,),
        in_specs=[
            pl.BlockSpec((None, s, LANES), lambda i: (i, 0, 0)),
            _resident((1, LANES), lambda i: (0, 0)),
        ],
        out_specs=pl.BlockSpec((None, s, LANES), lambda i: (i, 0, 0)),
        out_shape=jax.ShapeDtypeStruct((b, s, LANES), BF16),
        compiler_params=_params(("parallel",)),
        name="fox_gates",
    )(f, b_f)


def _fox_kernel(qt_ref, k_ref, vt_ref, c_ref, o_ref, sa_ref, sb_ref, pa_ref, pb_ref, acc_ref):
    qi = pl.program_id(2)
    q_heads = _head_operands_t(qt_ref[...])
    heads = range(len(q_heads))
    first_head = pl.program_id(1) * len(q_heads)
    gate_row = lax.broadcasted_iota(jnp.int32, (LANES, QUERY_TILE), 0)
    masks = _diagonal_masks(strict=False)
    w_heads = []
    for h in heads:
        picks = ((gate_row & (N_FOX_HEADS - 1)) == first_head + h) & (gate_row < GATE_PARTS * N_FOX_HEADS)
        minus_gate = jnp.where(picks, -1.0, 0.0).astype(BF16)
        w_heads.append(jnp.concatenate([q_heads[h], minus_gate], axis=0))

    def scores(kb, s_ref, cols=ALL):
        start = pl.multiple_of(kb * KEY_BLOCK, KEY_BLOCK)
        gates = c_ref[pl.ds(start, KEY_BLOCK), :]
        for h in heads:
            lhs = jnp.concatenate([_key_rows(k_ref, kb, h), gates], axis=1)
            s_ref[h, :, cols] = _dot(lhs, w_heads[h][:, cols])

    ones_rows = jnp.ones((16, KEY_BLOCK), BF16)

    def accumulate(kb, p_ref, a, ls, cols=ALL):
        new_ls = []
        for h in heads:
            lhs = jnp.concatenate([_value_rows_t(vt_ref, kb, h), ones_rows], axis=0)
            pv = _dot(lhs, pltpu.bitcast(p_ref[h, :, cols], BF16))
            acc_ref[h, :, cols] = a[h] * acc_ref[h, :, cols] + pv[:HEAD_DIM]
            l_new = a[h] * ls[h][:, cols] + pv[HEAD_DIM:HEAD_DIM + 1]
            new_ls.append(l_new if cols is ALL else _widen(ls[h], l_new))
        return new_ls

    def softmax(s_ref, p_ref, ms, mask, cols=ALL):
        new_ms, a = [], []
        for h in heads:
            m = ms[h][:, cols]
            s2 = s_ref[h, :, cols] if mask is None else jnp.where(mask, s_ref[h, :, cols], NEG_BIG)
            m_new = jnp.maximum(m, jnp.max(s2, axis=0, keepdims=True))
            p_ref[h, :, cols] = pltpu.bitcast(jnp.exp2(s2 - m_new).astype(BF16), jnp.uint32)
            new_ms.append(m_new if cols is ALL else _widen(ms[h], m_new))
            a.append(jnp.exp2(m - m_new))
        return new_ms, a

    def half_step(kb, cur, oth, state, *, mask=None, cols=ALL, next_cols=ALL):
        ms, ls, a_prev = state
        if next_cols is not None:
            scores(kb + 1, oth[0], next_cols)
        ls = accumulate(jnp.maximum(kb - 1, 0), oth[1], a_prev, ls)
        ms, a = softmax(cur[0], cur[1], ms, mask, cols)
        return ms, ls, a

    slot_a, slot_b = (sa_ref, pa_ref), (sb_ref, pb_ref)
    pb_ref[...] = jnp.zeros_like(pb_ref)
    acc_ref[...] = jnp.zeros_like(acc_ref)
    scores(0, sa_ref)
    state = ([jnp.full((1, QUERY_TILE), NEG_BIG, F32) for _ in heads],
             [jnp.zeros((1, QUERY_TILE), F32) for _ in heads],
             [jnp.ones((1, QUERY_TILE), F32) for _ in heads])

    def body(i, state):
        state = half_step(2 * i, slot_a, slot_b, state)
        return half_step(2 * i + 1, slot_b, slot_a, state)

    state = lax.fori_loop(0, qi, body, state)
    state = half_step(2 * qi, slot_a, slot_b, state, mask=masks[0], next_cols=LATE)
    _, ls, a = half_step(2 * qi + 1, slot_b, slot_a, state, mask=masks[1], cols=LATE, next_cols=None)
    ls = accumulate(2 * qi + 1, pb_ref, a, ls, LATE)
    for h in heads:
        o_ref[h * HEAD_DIM:(h + 1) * HEAD_DIM, :] = (acc_ref[h] / ls[h]).astype(o_ref.dtype)


def _fox_attention(proj, q_t, v_t, gates, pairs=4):
    b, s, _ = proj.shape
    groups = N_FOX_PAIRS // pairs
    return pl.pallas_call(
        _fox_kernel,
        grid=(b, groups, s // QUERY_TILE),
        in_specs=[
            pl.BlockSpec((None, pairs * LANES, QUERY_TILE), lambda i, h, j: (i, h, j)),
            pl.BlockSpec((None, s, pairs * LANES), lambda i, h, j: (i, 0, h)),
            pl.BlockSpec((None, s // KEY_BLOCK, pairs * LANES, KEY_BLOCK), lambda i, h, j: (i, 0, h, 0)),
            pl.BlockSpec((None, s, LANES), lambda i, h, j: (i, 0, 0)),
        ],
        out_specs=pl.BlockSpec((None, pairs * LANES, QUERY_TILE), lambda i, h, j: (i, h, j)),
        out_shape=jax.ShapeDtypeStruct((b, FOX_W, s), BF16),
        scratch_shapes=_attention_scratch(pairs),
        compiler_params=_params(("parallel", "parallel", "arbitrary")),
        name="fox_attention",
    )(q_t, proj, v_t, gates)


def _even_projection(w_in):
    kv0 = 3 * SB_W + SWA_QW
    dup = lambda w: jnp.repeat(w.reshape(D_MODEL, N_SWA_KV, HEAD_DIM), 2, axis=1).reshape(D_MODEL, -1)
    w_cat = jnp.concatenate(
        [w_in[:, SB_W:2 * SB_W], w_in[:, 3 * SB_W:kv0],
         dup(w_in[:, kv0:kv0 + SWA_KVW]), dup(w_in[:, kv0 + SWA_KVW:])], axis=1)
    return (w_cat.astype(BF16), w_in[:, :SB_W].T.astype(BF16), w_in[:, 2 * SB_W:3 * SB_W].T.astype(BF16),
            -QK_SCALE * LOG2E)


def _even_attention(proj, q_t, v_t, sinks, rel_bias):
    return _sb_attention(proj, q_t, v_t), _swa_attention(proj, _swa_bias(rel_bias), sinks.astype(F32))


def _spread_gates(v):
    return jnp.pad(jnp.tile(v, (1, GATE_PARTS)), ((0, 0), (0, LANES - GATE_PARTS * N_FOX_HEADS)))


def _odd_projection(w_in):
    w_cat = jnp.concatenate([w_in[:, FOX_W:2 * FOX_W], _spread_gates(w_in[:, 3 * FOX_W:])], axis=1)
    return (w_cat.astype(BF16), w_in[:, :FOX_W].T.astype(BF16), w_in[:, 2 * FOX_W:3 * FOX_W].T.astype(BF16),
            QK_SCALE * LOG2E)


def _odd_attention(proj, q_t, v_t, f, b_f):
    gates = _fox_gates(f, _spread_gates(b_f.reshape(1, -1).astype(F32)))
    return _fox_attention(proj, q_t, v_t, gates), None


def kernel(x, ln_g, ln_b, ffn1_in, ffn1_out, ffn2_in, ffn2_out, ab_w_in, ab_w_out, ab_sinks,
           fox_w_in, fox_b_f, fox_w_out, rel_bias):
    bsz, s, d = x.shape
    assert (s, d) == (SEQ, D_MODEL) and ln_g.shape[0] == DEPTH
    row = lambda v: v.reshape(1, D_MODEL).astype(F32)
    ffn1 = ffn1_in.astype(BF16), ffn1_out.astype(BF16)
    ffn2 = ffn2_in.astype(BF16), ffn2_out.astype(BF16)
    h = x
    for layer in range(DEPTH):
        ln = [(row(ln_g[layer, k]), row(ln_b[layer, k])) for k in range(3)]
        i = layer // 2
        if layer % 2 == 0:
            h, *p = _ffn_proj(h, *ffn1, layer, *ln[0], *_even_projection(ab_w_in[i]))
            o_t, o = _even_attention(*p, ab_sinks[i], rel_bias)
            w_mix = ab_w_out[i]
        else:
            h, *p = _ffn_proj(h, *ffn1, layer, *ln[0], *_odd_projection(fox_w_in[i]), tail=LANES)
            o_t, o = _odd_attention(*p, fox_b_f[i])
            w_mix = fox_w_out[i]
        h = _mix_ffn_ln(o_t, o, h, w_mix.astype(BF16), ln[1], *ffn2, layer, ln[2])
    return h
```

```python
import functools
import math

import numpy as np
import jax
import jax.numpy as jnp
from jax import lax
from jax.experimental import pallas as pl
from jax.experimental.pallas import tpu as pltpu

D_MODEL = 1024
SEQ = 2048
DEPTH = 2
HEAD_DIM = 64
LANES = 128
N_SB_PAIRS = 4
N_SWA_HEADS = 8
N_SWA_KV = 2
N_FOX_HEADS = 16
N_FOX_PAIRS = 8
WINDOW = 128
N_BUCKETS = 32
MAX_DISTANCE = 128
D_FF = 2816
ALPHA = (2 * DEPTH) ** 0.25
LN_EPS = 1e-5
SB_W = 512
SWA_QW = 512
SWA_KVW = 128
FOX_W = 1024
NEG_BIG = -1e30
LOG2E = math.log2(math.e)
QK_SCALE = HEAD_DIM ** -0.5
GATE_PARTS = 3
KEY_BLOCK = 256
QUERY_TILE = 2 * KEY_BLOCK
VMEM_LIMIT = 56 * 1024 * 1024

BF16 = jnp.bfloat16
F32 = jnp.float32


def _dot(a, b):
    return jnp.dot(a, b, preferred_element_type=F32)


def _dot_nt(a, b):
    return lax.dot_general(a, b, (((1,), (1,)), ((), ())), preferred_element_type=F32)


def _dot_tn(a, b):
    return lax.dot_general(a, b, (((0,), (0,)), ((), ())), preferred_element_type=F32)


def _layer_norm(z, g, b):
    mu = jnp.mean(z, axis=-1, keepdims=True)
    d = z - mu
    var = jnp.mean(d * d, axis=-1, keepdims=True)
    return d * lax.rsqrt(var + LN_EPS) * g + b


def _log2_sigmoid(x2):
    sign = jnp.uint32(0x80000000)
    neg_abs = lax.bitcast_convert_type(lax.bitcast_convert_type(x2, jnp.uint32) | sign, F32)
    return jnp.minimum(x2, 0.0) - jnp.log2(1.0 + jnp.exp2(neg_abs))


def _split_bf16(x, parts):
    out = []
    r = x
    for i in range(parts):
        t = r.astype(BF16)
        out.append(t)
        if i + 1 < parts:
            r = r - t.astype(F32)
    return out


def _params(sem):
    return pltpu.CompilerParams(dimension_semantics=sem, vmem_limit_bytes=VMEM_LIMIT)


def _resident(shape, index_map):
    return pl.BlockSpec(shape, index_map, pipeline_mode=pl.Buffered(1))


def _swiglu_ln(x, wg_ref, wu_ref, wo_ref, g_ref, b_ref):
    xb = x.astype(BF16)
    gate = _dot(xb, wg_ref[...])
    up = _dot(xb, wu_ref[...])
    h = (gate * jax.nn.sigmoid(gate) * up).astype(BF16)
    y = _dot(h, wo_ref[...])
    return _layer_norm(ALPHA * x + 0.5 * y, g_ref[...], b_ref[...])


def _mix_ffn_ln_kernel(*refs, n_normal):
    ot_ref, refs = refs[0], refs[1:]
    o_refs, refs = refs[:n_normal], refs[n_normal:]
    h_ref, wt_ref, refs = refs[0], refs[1], refs[2:]
    w_refs, refs = refs[:n_normal], refs[n_normal:]
    g1_ref, b1_ref, wg_ref, wu_ref, wo_ref, g2_ref, b2_ref, out_ref = refs
    mix = _dot_tn(ot_ref[...], wt_ref[...])
    for o_ref, w_ref in zip(o_refs, w_refs):
        mix = mix + _dot(o_ref[...], w_ref[...])
    half = h_ref.shape[0] // 2
    xs = []
    for r in range(2):
        rows = slice(r * half, (r + 1) * half)
        xs.append(_layer_norm(ALPHA * h_ref[rows, :] + mix[rows], g1_ref[...], b1_ref[...]))
    for r in range(2):
        rows = slice(r * half, (r + 1) * half)
        out_ref[rows, :] = _swiglu_ln(xs[r], wg_ref, wu_ref, wo_ref, g2_ref, b2_ref)


def _ffn_proj_kernel(x_ref, wg_ref, wu_ref, wo_ref, g_ref, b_ref, w_ref, wt_ref,
                     h_ref, o_ref, qt_ref, vt_ref, *tail_ref, q_scale):
    half = x_ref.shape[0] // 2
    for r in range(2):
        rows = slice(r * half, (r + 1) * half)
        h_ref[rows, :] = _swiglu_ln(x_ref[rows, :], wg_ref, wu_ref, wo_ref, g_ref, b_ref)
    hb = h_ref[...].astype(BF16)
    width = o_ref.shape[-1]
    acc = _dot(hb, w_ref[...])
    o_ref[...] = acc[:, :width].astype(o_ref.dtype)
    for ref in tail_ref:
        ref[...] = acc[:, width:]
    acc_t = _dot_nt(wt_ref[...], hb)
    qw = qt_ref.shape[0]
    qt_ref[...] = (acc_t[:qw] * q_scale).astype(qt_ref.dtype)
    for j in range(vt_ref.shape[0]):
        vt_ref[j] = acc_t[qw:, j * KEY_BLOCK:(j + 1) * KEY_BLOCK].astype(vt_ref.dtype)


def _ffn_proj(x, w_in, w_out, layer, g, b, w, wq_t, wv_t, q_scale, tail=0, tm=512):
    bsz, s, _ = x.shape
    width, qw, vw = w.shape[1] - tail, wq_t.shape[0], wv_t.shape[0]
    const = lambda i, j: (0, 0)
    tok = lambda n: pl.BlockSpec((None, tm, n), lambda i, j: (i, j, 0))
    out_specs = [tok(D_MODEL), tok(width), pl.BlockSpec((None, qw, tm), lambda i, j: (i, 0, j)),
                 pl.BlockSpec((None, tm // KEY_BLOCK, vw, KEY_BLOCK), lambda i, j: (i, j, 0, 0))]
    out_shape = [jax.ShapeDtypeStruct((bsz, s, D_MODEL), F32), jax.ShapeDtypeStruct((bsz, s, width), BF16),
                 jax.ShapeDtypeStruct((bsz, qw, s), BF16),
                 jax.ShapeDtypeStruct((bsz, s // KEY_BLOCK, vw, KEY_BLOCK), BF16)]
    if tail:
        out_specs.append(tok(tail))
        out_shape.append(jax.ShapeDtypeStruct((bsz, s, tail), F32))
    return pl.pallas_call(
        functools.partial(_ffn_proj_kernel, q_scale=q_scale),
        grid=(bsz, s // tm),
        in_specs=[
            tok(D_MODEL),
            _resident((None, D_MODEL, D_FF), lambda i, j: (layer, 0, 0)),
            _resident((None, D_MODEL, D_FF), lambda i, j: (layer, 0, 1)),
            _resident((None, D_FF, D_MODEL), lambda i, j: (layer, 0, 0)),
            _resident((1, D_MODEL), const), _resident((1, D_MODEL), const),
            _resident((D_MODEL, width + tail), const),
            _resident((qw + vw, D_MODEL), const),
        ],
        out_specs=out_specs,
        out_shape=out_shape,
        compiler_params=_params(("parallel", "parallel")),
        name="ffn_proj",
    )(x, w_in, w_in, w_out, g, b, w, jnp.concatenate([wq_t, wv_t], axis=0))


def _mix_ffn_ln(o_t, o, h, w_mix, ln1, w_in, w_out, layer, ln2, tm=512):
    bsz, s, _ = h.shape
    wt = o_t.shape[1]
    const = lambda i, j: (0, 0)
    tok = lambda width: pl.BlockSpec((None, tm, width), lambda i, j: (i, j, 0))
    row = _resident((1, D_MODEL), const)
    normal = [] if o is None else [o]
    in_specs = ([pl.BlockSpec((None, wt, tm), lambda i, j: (i, 0, j))] + [tok(a.shape[2]) for a in normal]
                + [tok(D_MODEL), _resident((wt, D_MODEL), const)]
                + [_resident((a.shape[2], D_MODEL), lambda i, j: (wt // a.shape[2], 0)) for a in normal]
                + [row, row, _resident((None, D_MODEL, D_FF), lambda i, j: (layer, 0, 0)),
                   _resident((None, D_MODEL, D_FF), lambda i, j: (layer, 0, 1)),
                   _resident((None, D_FF, D_MODEL), lambda i, j: (layer, 0, 0)), row, row])
    args = [o_t] + normal + [h, w_mix] + [w_mix for _ in normal] + [*ln1, w_in, w_in, w_out, *ln2]
    return pl.pallas_call(
        functools.partial(_mix_ffn_ln_kernel, n_normal=len(normal)),
        grid=(bsz, s // tm),
        in_specs=in_specs,
        out_specs=tok(D_MODEL),
        out_shape=jax.ShapeDtypeStruct((bsz, s, D_MODEL), F32),
        compiler_params=_params(("parallel", "parallel")),
        name="mix_ffn_ln",
    )(*args)


def _head_pair(q):
    lo = lax.broadcasted_iota(jnp.int32, q.shape, 1) < HEAD_DIM
    zero = jnp.zeros_like(q)
    return jnp.where(lo, q, zero), jnp.where(lo, zero, q)


def _merge_pair(a0, a1):
    lo = lax.broadcasted_iota(jnp.int32, a0.shape, 1) < HEAD_DIM
    return jnp.where(lo, a0, a1)


def _head_operands_t(q_t):
    top = lax.broadcasted_iota(jnp.int32, (LANES, q_t.shape[1]), 0) < HEAD_DIM
    out = []
    for p in range(q_t.shape[0] // LANES):
        tile = q_t[p * LANES:(p + 1) * LANES]
        zero = jnp.zeros_like(tile)
        out += [jnp.where(top, tile, zero), jnp.where(top, zero, tile)]
    return out


def _key_rows(k_ref, kb, h):
    start = pl.multiple_of(kb * KEY_BLOCK, KEY_BLOCK)
    return k_ref[pl.ds(start, KEY_BLOCK), (h // 2) * LANES:(h // 2 + 1) * LANES]


def _value_rows_t(vt_ref, kb, h):
    return vt_ref[kb, h * HEAD_DIM:(h + 1) * HEAD_DIM, :]


ALL = slice(None)
LATE = slice(KEY_BLOCK, QUERY_TILE)


def _diagonal_masks(strict):
    row = lax.broadcasted_iota(jnp.int32, (KEY_BLOCK, QUERY_TILE), 0)
    col = lax.broadcasted_iota(jnp.int32, (KEY_BLOCK, QUERY_TILE), 1)
    full = row < col if strict else row <= col
    return full, full[:, :KEY_BLOCK]


def _widen(row_vec, late_part):
    return jnp.concatenate([row_vec[:, :KEY_BLOCK], late_part], axis=1)


def _attention_scratch(pairs):
    tile = (2 * pairs, KEY_BLOCK, QUERY_TILE)
    packed = (2 * pairs, KEY_BLOCK // 2, QUERY_TILE)
    return [pltpu.VMEM(tile, F32), pltpu.VMEM(tile, F32),
            pltpu.VMEM(packed, jnp.uint32), pltpu.VMEM(packed, jnp.uint32),
            pltpu.VMEM((2 * pairs, HEAD_DIM, QUERY_TILE), F32)]


def _sb_kernel(qt_ref, k_ref, vt_ref, o_ref, nza_ref, nzb_ref, wa_ref, wb_ref, acc_ref):
    qi = pl.program_id(2)
    w_heads = _head_operands_t(qt_ref[...])
    heads = range(len(w_heads))
    masks = _diagonal_masks(strict=True)
    row = lax.broadcasted_iota(jnp.int32, (KEY_BLOCK, KEY_BLOCK), 0)
    col = lax.broadcasted_iota(jnp.int32, (KEY_BLOCK, KEY_BLOCK), 1)
    suffix = jnp.where(col >= row, 1.0, 0.0).astype(BF16)
    last = 2 * qi + 1

    def scores(kb, nz_ref, cols=ALL):
        for h in heads:
            nz_ref[h, :, cols] = _dot(_key_rows(k_ref, kb, h), w_heads[h][:, cols])

    def accumulate(kb, w_ref, cols=ALL):
        for h in heads:
            acc_ref[h, :, cols] += _dot(_value_rows_t(vt_ref, kb, h), pltpu.bitcast(w_ref[h, :, cols], BF16))

    def weights(nz_ref, w_ref, carries, mask, cols=ALL):
        incl = []
        for h in heads:
            log_keep = _log2_sigmoid(nz_ref[h, :, cols])
            if mask is not None:
                log_keep = jnp.where(mask, log_keep, 0.0)
            incl.append(_dot(suffix, log_keep.astype(BF16)))
        for h in heads:
            w = jnp.exp2(incl[h] + carries[h][:, cols] - nz_ref[h, :, cols])
            if mask is not None:
                w = jnp.where(mask, w, 0.0)
            w_ref[h, :, cols] = pltpu.bitcast(w.astype(BF16), jnp.uint32)
        totals = [carries[h][:, cols] + incl[h][0:1, :] for h in heads]
        return totals if cols is ALL else [_widen(carries[h], totals[h]) for h in heads]

    def half_step(j, cur, oth, carries, *, mask=None, cols=ALL, prev_cols=ALL, first=False):
        kb = last - j
        scores(jnp.maximum(kb - 1, 0), oth[0])
        if not first:
            accumulate(kb + 1, oth[1], prev_cols)
        return weights(cur[0], cur[1], carries, mask, cols)

    slot_a, slot_b = (nza_ref, wa_ref), (nzb_ref, wb_ref)
    acc_ref[...] = jnp.zeros_like(acc_ref)
    scores(last, nza_ref, LATE)
    carries = [jnp.zeros((1, QUERY_TILE), F32) for _ in heads]
    carries = half_step(0, slot_a, slot_b, carries, mask=masks[1], cols=LATE, first=True)
    carries = half_step(1, slot_b, slot_a, carries, mask=masks[0], prev_cols=LATE)

    def body(i, carries):
        carries = half_step(2 * i + 2, slot_a, slot_b, carries)
        return half_step(2 * i + 3, slot_b, slot_a, carries)

    lax.fori_loop(0, qi, body, carries)
    accumulate(0, wb_ref)
    for h in heads:
        o_ref[h * HEAD_DIM:(h + 1) * HEAD_DIM, :] = acc_ref[h].astype(o_ref.dtype)


def _sb_attention(proj, q_t, v_t, pairs=4):
    b, s, _ = proj.shape
    groups = N_SB_PAIRS // pairs
    return pl.pallas_call(
        _sb_kernel,
        grid=(b, groups, s // QUERY_TILE),
        in_specs=[
            pl.BlockSpec((None, pairs * LANES, QUERY_TILE), lambda i, h, j: (i, h, j)),
            pl.BlockSpec((None, s, pairs * LANES), lambda i, h, j: (i, 0, h)),
            pl.BlockSpec((None, s // KEY_BLOCK, pairs * LANES, KEY_BLOCK), lambda i, h, j: (i, 0, h, 0)),
        ],
        out_specs=pl.BlockSpec((None, pairs * LANES, QUERY_TILE), lambda i, h, j: (i, h, j)),
        out_shape=jax.ShapeDtypeStruct((b, SB_W, s), BF16),
        scratch_shapes=_attention_scratch(pairs),
        compiler_params=_params(("parallel", "parallel", "arbitrary")),
        name="sb_attention",
    )(q_t, proj, v_t)


def _t5_bucket_table():
    n = np.arange(WINDOW)
    max_exact = N_BUCKETS // 2
    nf = np.maximum(n, 1).astype(np.float64)
    val = np.log(nf / max_exact) / math.log(MAX_DISTANCE / max_exact) * (N_BUCKETS - max_exact)
    frac = np.abs(val - np.round(val))[n > max_exact]
    assert frac.min() > 1e-3, "bucket boundary too close to an integer distance for float32"
    large = np.minimum(max_exact + val.astype(np.int64), N_BUCKETS - 1)
    return np.where(n < max_exact, n, large)


def _swa_bucket_tiles():
    table = _t5_bucket_table()
    qi = np.arange(WINDOW)[:, None]
    kj = np.arange(2 * WINDOW)[None, :]
    tiles = []
    for offset in (WINDOW, 0):
        rel = qi + offset - kj
        valid = (rel >= 0) & (rel < WINDOW)
        tiles.append(np.where(valid, table[np.clip(rel, 0, WINDOW - 1)], -1))
    return np.stack(tiles).astype(np.int32)


def _swa_bias_kernel(rb_ref, idx_ref, o_ref):
    h = pl.program_id(0)
    for tile in range(2):
        idx = idx_ref[tile]
        acc = jnp.full(idx.shape, NEG_BIG, F32)
        for bucket in range(N_BUCKETS):
            acc = jnp.where(idx == bucket, rb_ref[bucket, h], acc)
        o_ref[tile, 0] = acc


def _swa_bias(rel_bias):
    idx = jnp.asarray(_swa_bucket_tiles())
    return pl.pallas_call(
        _swa_bias_kernel,
        grid=(N_SWA_HEADS,),
        in_specs=[
            pl.BlockSpec(memory_space=pltpu.SMEM),
            pl.BlockSpec((2, WINDOW, 2 * WINDOW), lambda h: (0, 0, 0)),
        ],
        out_specs=pl.BlockSpec((2, 1, WINDOW, 2 * WINDOW), lambda h: (0, h, 0, 0)),
        out_shape=jax.ShapeDtypeStruct((2, N_SWA_HEADS, WINDOW, 2 * WINDOW), F32),
        compiler_params=_params(("arbitrary",)),
        name="swa_bias",
    )(rel_bias, idx)


def _swa_kernel(sink_ref, q_ref, k_ref, v_ref, bias_ref, o_ref, *, tq):
    qt = pl.program_id(1)
    scale = jnp.asarray(QK_SCALE, BF16)
    group = N_SWA_HEADS // N_SWA_KV
    head_of_row = lax.broadcasted_iota(jnp.int32, (group * WINDOW, 1), 0) // WINDOW
    for j in range(tq // WINDOW):
        nb = qt * (tq // WINDOW) + j
        first = (nb == 0).astype(jnp.int32) if j == 0 else 0
        w0 = pl.multiple_of(jnp.maximum(nb - 1, 0) * WINDOW, WINDOW)
        rows = slice(j * WINDOW, (j + 1) * WINDOW)
        for g in range(N_SWA_KV):
            kwin = k_ref[pl.ds(w0, 2 * WINDOW), g * LANES:(g + 1) * LANES]
            vwin = v_ref[pl.ds(w0, 2 * WINDOW), g * LANES:(g + 1) * LANES]
            pairs = [g * (group // 2) + c for c in range(group // 2)]
            q_rows = []
            for pair in pairs:
                q_rows += _head_pair(q_ref[rows, pair * LANES:(pair + 1) * LANES] * scale)
            sink = jnp.zeros((group * WINDOW, 1), F32)
            for i in range(group):
                sink = jnp.where(head_of_row == i, sink_ref[g * group + i], sink)
            bias = bias_ref[first, g * group:(g + 1) * group].reshape(group * WINDOW, 2 * WINDOW)
            logits = _dot_nt(jnp.concatenate(q_rows, axis=0), kwin) + bias
            m = jnp.maximum(jnp.max(logits, axis=-1, keepdims=True), sink)
            p = jnp.exp(logits - m)
            denom = jnp.sum(p, axis=-1, keepdims=True) + jnp.exp(sink - m)
            out = _dot(p.astype(BF16), vwin) / denom
            for c, pair in enumerate(pairs):
                lo, hi = out[2 * c * WINDOW:(2 * c + 1) * WINDOW], out[(2 * c + 1) * WINDOW:(2 * c + 2) * WINDOW]
                o_ref[rows, pair * LANES:(pair + 1) * LANES] = _merge_pair(lo, hi).astype(o_ref.dtype)


def _swa_attention(proj, bias, sinks, tq=512):
    b, s, width = proj.shape
    q_blk = (width - SWA_QW - 4 * LANES) // SWA_QW
    k_blk = (width - 4 * LANES) // (2 * LANES)
    return pl.pallas_call(
        functools.partial(_swa_kernel, tq=tq),
        grid=(b, s // tq),
        in_specs=[
            pl.BlockSpec(memory_space=pltpu.SMEM),
            pl.BlockSpec((None, tq, SWA_QW), lambda i, j: (i, j, q_blk)),
            pl.BlockSpec((None, s, 2 * LANES), lambda i, j: (i, 0, k_blk)),
            pl.BlockSpec((None, s, 2 * LANES), lambda i, j: (i, 0, k_blk + 1)),
            _resident((2, N_SWA_HEADS, WINDOW, 2 * WINDOW), lambda i, j: (0, 0, 0, 0)),
        ],
        out_specs=pl.BlockSpec((None, tq, SWA_QW), lambda i, j: (i, j, 0)),
        out_shape=jax.ShapeDtypeStruct((b, s, SWA_QW), BF16),
        compiler_params=_params(("parallel", "arbitrary")),
        name="swa_attention",
    )(sinks, proj, proj, proj, bias)


def _fox_gate_kernel(f_ref, bf_ref, c_ref, *, chunk):
    s = f_ref.shape[0]
    r = lax.broadcasted_iota(jnp.int32, (chunk, chunk), 0)
    c = lax.broadcasted_iota(jnp.int32, (chunk, chunk), 1)
    upto = jnp.where(c <= r, 1.0, 0.0).astype(BF16)
    lane = lax.broadcasted_iota(jnp.int32, (chunk, LANES), 1)
    carry = jnp.zeros((1, LANES), F32)
    for j in range(s // chunk):
        rows = slice(j * chunk, (j + 1) * chunk)
        log_f = _log2_sigmoid((f_ref[rows, :] + bf_ref[...]) * LOG2E)
        cs = carry
        for part in _split_bf16(log_f, 3):
            cs = cs + _dot(upto, part)
        carry = cs[chunk - 1:chunk, :]
        out = jnp.zeros((chunk, LANES), BF16)
        for k, part in reversed(list(enumerate(_split_bf16(cs, GATE_PARTS)))):
            out = jnp.where(lane < (k + 1) * N_FOX_HEADS, part, out)
        c_ref[rows, :] = out


def _fox_gates(f, b_f, chunk=256):
    b, s, _ = f.shape
    return pl.pallas_call(
        functools.partial(_fox_gate_kernel, chunk=chunk),
        grid=(b,),
        in_specs=[
            pl.BlockSpec((None, s, LANES), lambda i: (i, 0, 0)),
            _resident((1, LANES), lambda i: (0, 0)),
        ],
        out_specs=pl.BlockSpec((None, s, LANES), lambda i: (i, 0, 0)),
        out_shape=jax.ShapeDtypeStruct((b, s, LANES), BF16),
        compiler_params=_params(("parallel",)),
        name="fox_gates",
    )(f, b_f)


def _fox_kernel(qt_ref, k_ref, vt_ref, c_ref, o_ref, sa_ref, sb_ref, pa_ref, pb_ref, acc_ref):
    qi = pl.program_id(2)
    q_heads = _head_operands_t(qt_ref[...])
    heads = range(len(q_heads))
    first_head = pl.program_id(1) * len(q_heads)
    gate_row = lax.broadcasted_iota(jnp.int32, (LANES, QUERY_TILE), 0)
    masks = _diagonal_masks(strict=False)
    w_heads = []
    for h in heads:
        picks = ((gate_row & (N_FOX_HEADS - 1)) == first_head + h) & (gate_row < GATE_PARTS * N_FOX_HEADS)
        minus_gate = jnp.where(picks, -1.0, 0.0).astype(BF16)
        w_heads.append(jnp.concatenate([q_heads[h], minus_gate], axis=0))

    def scores(kb, s_ref, cols=ALL):
        start = pl.multiple_of(kb * KEY_BLOCK, KEY_BLOCK)
        gates = c_ref[pl.ds(start, KEY_BLOCK), :]
        for h in heads:
            lhs = jnp.concatenate([_key_rows(k_ref, kb, h), gates], axis=1)
            s_ref[h, :, cols] = _dot(lhs, w_heads[h][:, cols])

    ones_rows = jnp.ones((16, KEY_BLOCK), BF16)

    def accumulate(kb, p_ref, a, ls, cols=ALL):
        new_ls = []
        for h in heads:
            lhs = jnp.concatenate([_value_rows_t(vt_ref, kb, h), ones_rows], axis=0)
            pv = _dot(lhs, pltpu.bitcast(p_ref[h, :, cols], BF16))
            acc_ref[h, :, cols] = a[h] * acc_ref[h, :, cols] + pv[:HEAD_DIM]
            l_new = a[h] * ls[h][:, cols] + pv[HEAD_DIM:HEAD_DIM + 1]
            new_ls.append(l_new if cols is ALL else _widen(ls[h], l_new))
        return new_ls

    def softmax(s_ref, p_ref, ms, mask, cols=ALL):
        new_ms, a = [], []
        for h in heads:
            m = ms[h][:, cols]
            s2 = s_ref[h, :, cols] if mask is None else jnp.where(mask, s_ref[h, :, cols], NEG_BIG)
            m_new = jnp.maximum(m, jnp.max(s2, axis=0, keepdims=True))
            p_ref[h, :, cols] = pltpu.bitcast(jnp.exp2(s2 - m_new).astype(BF16), jnp.uint32)
            new_ms.append(m_new if cols is ALL else _widen(ms[h], m_new))
            a.append(jnp.exp2(m - m_new))
        return new_ms, a

    def half_step(kb, cur, oth, state, *, mask=None, cols=ALL, next_cols=ALL):
        ms, ls, a_prev = state
        if next_cols is not None:
            scores(kb + 1, oth[0], next_cols)
        ls = accumulate(jnp.maximum(kb - 1, 0), oth[1], a_prev, ls)
        ms, a = softmax(cur[0], cur[1], ms, mask, cols)
        return ms, ls, a

    slot_a, slot_b = (sa_ref, pa_ref), (sb_ref, pb_ref)
    pb_ref[...] = jnp.zeros_like(pb_ref)
    acc_ref[...] = jnp.zeros_like(acc_ref)
    scores(0, sa_ref)
    state = ([jnp.full((1, QUERY_TILE), NEG_BIG, F32) for _ in heads],
             [jnp.zeros((1, QUERY_TILE), F32) for _ in heads],
             [jnp.ones((1, QUERY_TILE), F32) for _ in heads])

    def body(i, state):
        state = half_step(2 * i, slot_a, slot_b, state)
        return half_step(2 * i + 1, slot_b, slot_a, state)

    state = lax.fori_loop(0, qi, body, state)
    state = half_step(2 * qi, slot_a, slot_b, state, mask=masks[0], next_cols=LATE)
    _, ls, a = half_step(2 * qi + 1, slot_b, slot_a, state, mask=masks[1], cols=LATE, next_cols=None)
    ls = accumulate(2 * qi + 1, pb_ref, a, ls, LATE)
    for h in heads:
        o_ref[h * HEAD_DIM:(h + 1) * HEAD_DIM, :] = (acc_ref[h] / ls[h]).astype(o_ref.dtype)


def _fox_attention(proj, q_t, v_t, gates, pairs=8):
    b, s, _ = proj.shape
    groups = N_FOX_PAIRS // pairs
    return pl.pallas_call(
        _fox_kernel,
        grid=(b, groups, s // QUERY_TILE),
        in_specs=[
            pl.BlockSpec((None, pairs * LANES, QUERY_TILE), lambda i, h, j: (i, h, j)),
            pl.BlockSpec((None, s, pairs * LANES), lambda i, h, j: (i, 0, h)),
            pl.BlockSpec((None, s // KEY_BLOCK, pairs * LANES, KEY_BLOCK), lambda i, h, j: (i, 0, h, 0)),
            pl.BlockSpec((None, s, LANES), lambda i, h, j: (i, 0, 0)),
        ],
        out_specs=pl.BlockSpec((None, pairs * LANES, QUERY_TILE), lambda i, h, j: (i, h, j)),
        out_shape=jax.ShapeDtypeStruct((b, FOX_W, s), BF16),
        scratch_shapes=_attention_scratch(pairs),
        compiler_params=_params(("parallel", "parallel", "arbitrary")),
        name="fox_attention",
    )(q_t, proj, v_t, gates)


def _even_projection(w_in):
    kv0 = 3 * SB_W + SWA_QW
    dup = lambda w: jnp.repeat(w.reshape(D_MODEL, N_SWA_KV, HEAD_DIM), 2, axis=1).reshape(D_MODEL, -1)
    w_cat = jnp.concatenate(
        [w_in[:, SB_W:2 * SB_W], w_in[:, 3 * SB_W:kv0],
         dup(w_in[:, kv0:kv0 + SWA_KVW]), dup(w_in[:, kv0 + SWA_KVW:])], axis=1)
    return (w_cat.astype(BF16), w_in[:, :SB_W].T.astype(BF16), w_in[:, 2 * SB_W:3 * SB_W].T.astype(BF16),
            -QK_SCALE * LOG2E)


def _even_attention(proj, q_t, v_t, sinks, rel_bias):
    return _sb_attention(proj, q_t, v_t), _swa_attention(proj, _swa_bias(rel_bias), sinks.astype(F32))


def _spread_gates(v):
    return jnp.pad(jnp.tile(v, (1, GATE_PARTS)), ((0, 0), (0, LANES - GATE_PARTS * N_FOX_HEADS)))


def _odd_projection(w_in):
    w_cat = jnp.concatenate([w_in[:, FOX_W:2 * FOX_W], _spread_gates(w_in[:, 3 * FOX_W:])], axis=1)
    return (w_cat.astype(BF16), w_in[:, :FOX_W].T.astype(BF16), w_in[:, 2 * FOX_W:3 * FOX_W].T.astype(BF16),
            QK_SCALE * LOG2E)


def _odd_attention(proj, q_t, v_t, f, b_f):
    gates = _fox_gates(f, _spread_gates(b_f.reshape(1, -1).astype(F32)))
    return _fox_attention(proj, q_t, v_t, gates), None


def kernel(x, ln_g, ln_b, ffn1_in, ffn1_out, ffn2_in, ffn2_out, ab_w_in, ab_w_out, ab_sinks,
           fox_w_in, fox_b_f, fox_w_out, rel_bias):
    bsz, s, d = x.shape
    assert (s, d) == (SEQ, D_MODEL) and ln_g.shape[0] == DEPTH
    row = lambda v: v.reshape(1, D_MODEL).astype(F32)
    ffn1 = ffn1_in.astype(BF16), ffn1_out.astype(BF16)
    ffn2 = ffn2_in.astype(BF16), ffn2_out.astype(BF16)
    h = x
    for layer in range(DEPTH):
        ln = [(row(ln_g[layer, k]), row(ln_b[layer, k])) for k in range(3)]
        i = layer // 2
        if layer % 2 == 0:
            h, *p = _ffn_proj(h, *ffn1, layer, *ln[0], *_even_projection(ab_w_in[i]))
            o_t, o = _even_attention(*p, ab_sinks[i], rel_bias)
            w_mix = ab_w_out[i]
        else:
            h, *p = _ffn_proj(h, *ffn1, layer, *ln[0], *_odd_projection(fox_w_in[i]), tail=LANES)
            o_t, o = _odd_attention(*p, fox_b_f[i])
            w_mix = fox_w_out[i]
        h = _mix_ffn_ln(o_t, o, h, w_mix.astype(BF16), ln[1], *ffn2, layer, ln[2])
    return h
```

```python
import functools
import math

import numpy as np
import jax
import jax.numpy as jnp
from jax import lax
from jax.experimental import pallas as pl
from jax.experimental.pallas import tpu as pltpu

D_MODEL = 1024
SEQ = 2048
DEPTH = 2
HEAD_DIM = 64
LANES = 128
BF16_ROWS = 16
N_SB_PAIRS = 4
N_SWA_HEADS = 8
N_SWA_KV = 2
N_FOX_HEADS = 16
N_FOX_PAIRS = 8
WINDOW = 128
N_BUCKETS = 32
MAX_DISTANCE = 128
D_FF = 2816
ALPHA = (2 * DEPTH) ** 0.25
LN_EPS = 1e-5
SB_W = 512
SWA_QW = 512
SWA_KVW = 128
FOX_W = 1024
NEG_BIG = -1e30
LOG2E = math.log2(math.e)
QK_SCALE = HEAD_DIM ** -0.5
GATE_PARTS = 3
KEY_BLOCK = 256
QUERY_TILE = 2 * KEY_BLOCK
VMEM_LIMIT = 56 * 1024 * 1024

BF16 = jnp.bfloat16
F32 = jnp.float32


def _dot(a, b):
    return jnp.dot(a, b, preferred_element_type=F32)


def _dot_nt(a, b):
    return lax.dot_general(a, b, (((1,), (1,)), ((), ())), preferred_element_type=F32)


def _dot_tn(a, b):
    return lax.dot_general(a, b, (((0,), (0,)), ((), ())), preferred_element_type=F32)


def _layer_norm(z, g, b):
    mu = jnp.mean(z, axis=-1, keepdims=True)
    d = z - mu
    var = jnp.mean(d * d, axis=-1, keepdims=True)
    return d * lax.rsqrt(var + LN_EPS) * g + b


def _log2_sigmoid(x2):
    sign = jnp.uint32(0x80000000)
    neg_abs = lax.bitcast_convert_type(lax.bitcast_convert_type(x2, jnp.uint32) | sign, F32)
    return jnp.minimum(x2, 0.0) - jnp.log2(1.0 + jnp.exp2(neg_abs))


def _split_bf16(x, parts):
    out = []
    r = x
    for i in range(parts):
        t = r.astype(BF16)
        out.append(t)
        if i + 1 < parts:
            r = r - t.astype(F32)
    return out


def _params(sem):
    return pltpu.CompilerParams(dimension_semantics=sem, vmem_limit_bytes=VMEM_LIMIT)


def _resident(shape, index_map):
    return pl.BlockSpec(shape, index_map, pipeline_mode=pl.Buffered(1))


def _swiglu_ln(x, wg_ref, wu_ref, wo_ref, g_ref, b_ref):
    xb = x.astype(BF16)
    gate = _dot(xb, wg_ref[...])
    up = _dot(xb, wu_ref[...])
    h = (gate * jax.nn.sigmoid(gate) * up).astype(BF16)
    y = _dot(h, wo_ref[...])
    return _layer_norm(ALPHA * x + 0.5 * y, g_ref[...], b_ref[...])


def _mix_ffn_ln_kernel(*refs, n_normal):
    ot_ref, refs = refs[0], refs[1:]
    o_refs, refs = refs[:n_normal], refs[n_normal:]
    h_ref, wt_ref, refs = refs[0], refs[1], refs[2:]
    w_refs, refs = refs[:n_normal], refs[n_normal:]
    g1_ref, b1_ref, wg_ref, wu_ref, wo_ref, g2_ref, b2_ref, out_ref = refs
    mix = _dot_tn(ot_ref[...], wt_ref[...])
    for o_ref, w_ref in zip(o_refs, w_refs):
        mix = mix + _dot(o_ref[...], w_ref[...])
    half = h_ref.shape[0] // 2
    xs = []
    for r in range(2):
        rows = slice(r * half, (r + 1) * half)
        xs.append(_layer_norm(ALPHA * h_ref[rows, :] + mix[rows], g1_ref[...], b1_ref[...]))
    for r in range(2):
        rows = slice(r * half, (r + 1) * half)
        out_ref[rows, :] = _swiglu_ln(xs[r], wg_ref, wu_ref, wo_ref, g2_ref, b2_ref)


def _ffn_proj_kernel(x_ref, wg_ref, wu_ref, wo_ref, g_ref, b_ref, w_ref, wt_ref,
                     h_ref, o_ref, qt_ref, vt_ref, *tail_ref, q_scale):
    half = x_ref.shape[0] // 2
    for r in range(2):
        rows = slice(r * half, (r + 1) * half)
        h_ref[rows, :] = _swiglu_ln(x_ref[rows, :], wg_ref, wu_ref, wo_ref, g_ref, b_ref)
    hb = h_ref[...].astype(BF16)
    width = o_ref.shape[-1]
    acc = _dot(hb, w_ref[...])
    o_ref[...] = acc[:, :width].astype(o_ref.dtype)
    for ref in tail_ref:
        ref[...] = acc[:, width:]
    acc_t = _dot_nt(wt_ref[...], hb)
    qw = qt_ref.shape[0]
    qt_ref[...] = (acc_t[:qw] * q_scale).astype(qt_ref.dtype)
    for j in range(vt_ref.shape[0]):
        vt_ref[j] = acc_t[qw:, j * KEY_BLOCK:(j + 1) * KEY_BLOCK].astype(vt_ref.dtype)


def _ffn_proj(x, w_in, w_out, layer, g, b, w, wq_t, wv_t, q_scale, tail=0, tm=512):
    bsz, s, _ = x.shape
    width, qw, vw = w.shape[1] - tail, wq_t.shape[0], wv_t.shape[0]
    const = lambda i, j: (0, 0)
    tok = lambda n: pl.BlockSpec((None, tm, n), lambda i, j: (i, j, 0))
    out_specs = [tok(D_MODEL), tok(width), pl.BlockSpec((None, qw, tm), lambda i, j: (i, 0, j)),
                 pl.BlockSpec((None, tm // KEY_BLOCK, vw, KEY_BLOCK), lambda i, j: (i, j, 0, 0))]
    out_shape = [jax.ShapeDtypeStruct((bsz, s, D_MODEL), F32), jax.ShapeDtypeStruct((bsz, s, width), BF16),
                 jax.ShapeDtypeStruct((bsz, qw, s), BF16),
                 jax.ShapeDtypeStruct((bsz, s // KEY_BLOCK, vw, KEY_BLOCK), BF16)]
    if tail:
        out_specs.append(tok(tail))
        out_shape.append(jax.ShapeDtypeStruct((bsz, s, tail), F32))
    return pl.pallas_call(
        functools.partial(_ffn_proj_kernel, q_scale=q_scale),
        grid=(bsz, s // tm),
        in_specs=[
            tok(D_MODEL),
            _resident((None, D_MODEL, D_FF), lambda i, j: (layer, 0, 0)),
            _resident((None, D_MODEL, D_FF), lambda i, j: (layer, 0, 1)),
            _resident((None, D_FF, D_MODEL), lambda i, j: (layer, 0, 0)),
            _resident((1, D_MODEL), const), _resident((1, D_MODEL), const),
            _resident((D_MODEL, width + tail), const),
            _resident((qw + vw, D_MODEL), const),
        ],
        out_specs=out_specs,
        out_shape=out_shape,
        compiler_params=_params(("parallel", "parallel")),
        name="ffn_proj",
    )(x, w_in, w_in, w_out, g, b, w, jnp.concatenate([wq_t, wv_t], axis=0))


def _mix_ffn_ln(o_t, o, h, w_mix, ln1, w_in, w_out, layer, ln2, tm=512):
    bsz, s, _ = h.shape
    wt = o_t.shape[1]
    const = lambda i, j: (0, 0)
    tok = lambda width: pl.BlockSpec((None, tm, width), lambda i, j: (i, j, 0))
    row = _resident((1, D_MODEL), const)
    normal = [] if o is None else [o]
    in_specs = ([pl.BlockSpec((None, wt, tm), lambda i, j: (i, 0, j))] + [tok(a.shape[2]) for a in normal]
                + [tok(D_MODEL), _resident((wt, D_MODEL), const)]
                + [_resident((a.shape[2], D_MODEL), lambda i, j: (wt // a.shape[2], 0)) for a in normal]
                + [row, row, _resident((None, D_MODEL, D_FF), lambda i, j: (layer, 0, 0)),
                   _resident((None, D_MODEL, D_FF), lambda i, j: (layer, 0, 1)),
                   _resident((None, D_FF, D_MODEL), lambda i, j: (layer, 0, 0)), row, row])
    args = [o_t] + normal + [h, w_mix] + [w_mix for _ in normal] + [*ln1, w_in, w_in, w_out, *ln2]
    return pl.pallas_call(
        functools.partial(_mix_ffn_ln_kernel, n_normal=len(normal)),
        grid=(bsz, s // tm),
        in_specs=in_specs,
        out_specs=tok(D_MODEL),
        out_shape=jax.ShapeDtypeStruct((bsz, s, D_MODEL), F32),
        compiler_params=_params(("parallel", "parallel")),
        name="mix_ffn_ln",
    )(*args)


def _head_pair(q):
    lo = lax.broadcasted_iota(jnp.int32, q.shape, 1) < HEAD_DIM
    zero = jnp.zeros_like(q)
    return jnp.where(lo, q, zero), jnp.where(lo, zero, q)


def _merge_pair(a0, a1):
    lo = lax.broadcasted_iota(jnp.int32, a0.shape, 1) < HEAD_DIM
    return jnp.where(lo, a0, a1)


def _head_operands_t(q_t):
    top = lax.broadcasted_iota(jnp.int32, (LANES, q_t.shape[1]), 0) < HEAD_DIM
    out = []
    for p in range(q_t.shape[0] // LANES):
        tile = q_t[p * LANES:(p + 1) * LANES]
        zero = jnp.zeros_like(tile)
        out += [jnp.where(top, tile, zero), jnp.where(top, zero, tile)]
    return out


def _key_rows(k_ref, kb, h):
    start = pl.multiple_of(kb * KEY_BLOCK, KEY_BLOCK)
    return k_ref[pl.ds(start, KEY_BLOCK), (h // 2) * LANES:(h // 2 + 1) * LANES]


def _value_rows_t(vt_ref, kb, h):
    return vt_ref[kb, h * HEAD_DIM:(h + 1) * HEAD_DIM, :]


ALL = slice(None)
LATE = slice(KEY_BLOCK, QUERY_TILE)


def _diagonal_masks(strict):
    row = lax.broadcasted_iota(jnp.int32, (KEY_BLOCK, QUERY_TILE), 0)
    col = lax.broadcasted_iota(jnp.int32, (KEY_BLOCK, QUERY_TILE), 1)
    full = row < col if strict else row <= col
    return full, full[:, :KEY_BLOCK]


def _widen(row_vec, late_part):
    return jnp.concatenate([row_vec[:, :KEY_BLOCK], late_part], axis=1)


def _attention_scratch(pairs):
    tile = (2 * pairs, KEY_BLOCK, QUERY_TILE)
    packed = (2 * pairs, KEY_BLOCK // 2, QUERY_TILE)
    return [pltpu.VMEM(tile, F32), pltpu.VMEM(tile, F32),
            pltpu.VMEM(packed, jnp.uint32), pltpu.VMEM(packed, jnp.uint32),
            pltpu.VMEM((2 * pairs, HEAD_DIM, QUERY_TILE), F32)]


def _sb_kernel(qt_ref, k_ref, vt_ref, o_ref, nza_ref, nzb_ref, wa_ref, wb_ref, acc_ref):
    qi = pl.program_id(2)
    w_heads = _head_operands_t(qt_ref[...])
    heads = range(len(w_heads))
    masks = _diagonal_masks(strict=True)
    row = lax.broadcasted_iota(jnp.int32, (KEY_BLOCK, KEY_BLOCK), 0)
    col = lax.broadcasted_iota(jnp.int32, (KEY_BLOCK, KEY_BLOCK), 1)
    suffix = jnp.where(col >= row, 1.0, 0.0).astype(BF16)
    last = 2 * qi + 1

    def scores(kb, nz_ref, cols=ALL):
        for h in heads:
            nz_ref[h, :, cols] = _dot(_key_rows(k_ref, kb, h), w_heads[h][:, cols])

    def accumulate(kb, w_ref, cols=ALL):
        for h in heads:
            acc_ref[h, :, cols] += _dot(_value_rows_t(vt_ref, kb, h), pltpu.bitcast(w_ref[h, :, cols], BF16))

    def weights(nz_ref, w_ref, carries, mask, cols=ALL):
        incl = []
        for h in heads:
            log_keep = _log2_sigmoid(nz_ref[h, :, cols])
            if mask is not None:
                log_keep = jnp.where(mask, log_keep, 0.0)
            incl.append(_dot(suffix, log_keep.astype(BF16)))
        for h in heads:
            w = jnp.exp2(incl[h] + carries[h][:, cols] - nz_ref[h, :, cols])
            if mask is not None:
                w = jnp.where(mask, w, 0.0)
            w_ref[h, :, cols] = pltpu.bitcast(w.astype(BF16), jnp.uint32)
        totals = [carries[h][:, cols] + incl[h][0:1, :] for h in heads]
        return totals if cols is ALL else [_widen(carries[h], totals[h]) for h in heads]

    def half_step(j, cur, oth, carries, *, mask=None, cols=ALL, prev_cols=ALL, first=False):
        kb = last - j
        scores(jnp.maximum(kb - 1, 0), oth[0])
        if not first:
            accumulate(kb + 1, oth[1], prev_cols)
        return weights(cur[0], cur[1], carries, mask, cols)

    slot_a, slot_b = (nza_ref, wa_ref), (nzb_ref, wb_ref)
    acc_ref[...] = jnp.zeros_like(acc_ref)
    scores(last, nza_ref, LATE)
    carries = [jnp.zeros((1, QUERY_TILE), F32) for _ in heads]
    carries = half_step(0, slot_a, slot_b, carries, mask=masks[1], cols=LATE, first=True)
    carries = half_step(1, slot_b, slot_a, carries, mask=masks[0], prev_cols=LATE)

    def body(i, carries):
        carries = half_step(2 * i + 2, slot_a, slot_b, carries)
        return half_step(2 * i + 3, slot_b, slot_a, carries)

    lax.fori_loop(0, qi, body, carries)
    accumulate(0, wb_ref)
    for h in heads:
        o_ref[h * HEAD_DIM:(h + 1) * HEAD_DIM, :] = acc_ref[h].astype(o_ref.dtype)


def _sb_attention(proj, q_t, v_t, pairs=4):
    b, s, _ = proj.shape
    groups = N_SB_PAIRS // pairs
    return pl.pallas_call(
        _sb_kernel,
        grid=(b, groups, s // QUERY_TILE),
        in_specs=[
            pl.BlockSpec((None, pairs * LANES, QUERY_TILE), lambda i, h, j: (i, h, j)),
            pl.BlockSpec((None, s, pairs * LANES), lambda i, h, j: (i, 0, h)),
            pl.BlockSpec((None, s // KEY_BLOCK, pairs * LANES, KEY_BLOCK), lambda i, h, j: (i, 0, h, 0)),
        ],
        out_specs=pl.BlockSpec((None, pairs * LANES, QUERY_TILE), lambda i, h, j: (i, h, j)),
        out_shape=jax.ShapeDtypeStruct((b, SB_W, s), BF16),
        scratch_shapes=_attention_scratch(pairs),
        compiler_params=_params(("parallel", "parallel", "arbitrary")),
        name="sb_attention",
    )(q_t, proj, v_t)


def _t5_bucket_table():
    n = np.arange(WINDOW)
    max_exact = N_BUCKETS // 2
    nf = np.maximum(n, 1).astype(np.float64)
    val = np.log(nf / max_exact) / math.log(MAX_DISTANCE / max_exact) * (N_BUCKETS - max_exact)
    frac = np.abs(val - np.round(val))[n > max_exact]
    assert frac.min() > 1e-3, "bucket boundary too close to an integer distance for float32"
    large = np.minimum(max_exact + val.astype(np.int64), N_BUCKETS - 1)
    return np.where(n < max_exact, n, large)


def _swa_bucket_tiles():
    table = _t5_bucket_table()
    qi = np.arange(WINDOW)[:, None]
    kj = np.arange(2 * WINDOW)[None, :]
    tiles = []
    for offset in (WINDOW, 0):
        rel = qi + offset - kj
        valid = (rel >= 0) & (rel < WINDOW)
        tiles.append(np.where(valid, table[np.clip(rel, 0, WINDOW - 1)], -1))
    return np.stack(tiles).astype(np.int32)


def _swa_bias_kernel(rb_ref, idx_ref, o_ref):
    h = pl.program_id(0)
    for tile in range(2):
        idx = idx_ref[tile]
        acc = jnp.full(idx.shape, NEG_BIG, F32)
        for bucket in range(N_BUCKETS):
            acc = jnp.where(idx == bucket, rb_ref[bucket, h], acc)
        o_ref[tile, 0] = acc


def _swa_bias(rel_bias):
    idx = jnp.asarray(_swa_bucket_tiles())
    return pl.pallas_call(
        _swa_bias_kernel,
        grid=(N_SWA_HEADS,),
        in_specs=[
            pl.BlockSpec(memory_space=pltpu.SMEM),
            pl.BlockSpec((2, WINDOW, 2 * WINDOW), lambda h: (0, 0, 0)),
        ],
        out_specs=pl.BlockSpec((2, 1, WINDOW, 2 * WINDOW), lambda h: (0, h, 0, 0)),
        out_shape=jax.ShapeDtypeStruct((2, N_SWA_HEADS, WINDOW, 2 * WINDOW), F32),
        compiler_params=_params(("arbitrary",)),
        name="swa_bias",
    )(rel_bias, idx)


def _swa_kernel(sink_ref, q_ref, k_ref, v_ref, bias_ref, o_ref, *, tq):
    qt = pl.program_id(1)
    scale = jnp.asarray(QK_SCALE, BF16)
    group = N_SWA_HEADS // N_SWA_KV
    head_of_row = lax.broadcasted_iota(jnp.int32, (group * WINDOW, 1), 0) // WINDOW
    for j in range(tq // WINDOW):
        nb = qt * (tq // WINDOW) + j
        first = (nb == 0).astype(jnp.int32) if j == 0 else 0
        w0 = pl.multiple_of(jnp.maximum(nb - 1, 0) * WINDOW, WINDOW)
        rows = slice(j * WINDOW, (j + 1) * WINDOW)
        for g in range(N_SWA_KV):
            kwin = k_ref[pl.ds(w0, 2 * WINDOW), g * LANES:(g + 1) * LANES]
            vwin = v_ref[pl.ds(w0, 2 * WINDOW), g * LANES:(g + 1) * LANES]
            pairs = [g * (group // 2) + c for c in range(group // 2)]
            q_rows = []
            for pair in pairs:
                q_rows += _head_pair(q_ref[rows, pair * LANES:(pair + 1) * LANES] * scale)
            sink = jnp.zeros((group * WINDOW, 1), F32)
            for i in range(group):
                sink = jnp.where(head_of_row == i, sink_ref[g * group + i], sink)
            bias = bias_ref[first, g * group:(g + 1) * group].reshape(group * WINDOW, 2 * WINDOW)
            logits = _dot_nt(jnp.concatenate(q_rows, axis=0), kwin) + bias
            m = jnp.maximum(jnp.max(logits, axis=-1, keepdims=True), sink)
            p = jnp.exp(logits - m)
            denom = jnp.sum(p, axis=-1, keepdims=True) + jnp.exp(sink - m)
            out = _dot(p.astype(BF16), vwin) / denom
            for c, pair in enumerate(pairs):
                lo, hi = out[2 * c * WINDOW:(2 * c + 1) * WINDOW], out[(2 * c + 1) * WINDOW:(2 * c + 2) * WINDOW]
                o_ref[rows, pair * LANES:(pair + 1) * LANES] = _merge_pair(lo, hi).astype(o_ref.dtype)


def _swa_attention(proj, bias, sinks, tq=1024):
    b, s, width = proj.shape
    q_blk = (width - SWA_QW - 4 * LANES) // SWA_QW
    k_blk = (width - 4 * LANES) // (2 * LANES)
    return pl.pallas_call(
        functools.partial(_swa_kernel, tq=tq),
        grid=(b, s // tq),
        in_specs=[
            pl.BlockSpec(memory_space=pltpu.SMEM),
            pl.BlockSpec((None, tq, SWA_QW), lambda i, j: (i, j, q_blk)),
            pl.BlockSpec((None, s, 2 * LANES), lambda i, j: (i, 0, k_blk)),
            pl.BlockSpec((None, s, 2 * LANES), lambda i, j: (i, 0, k_blk + 1)),
            _resident((2, N_SWA_HEADS, WINDOW, 2 * WINDOW), lambda i, j: (0, 0, 0, 0)),
        ],
        out_specs=pl.BlockSpec((None, tq, SWA_QW), lambda i, j: (i, j, 0)),
        out_shape=jax.ShapeDtypeStruct((b, s, SWA_QW), BF16),
        compiler_params=_params(("parallel", "arbitrary")),
        name="swa_attention",
    )(sinks, proj, proj, proj, bias)


def _fox_gate_kernel(f_ref, bf_ref, c_ref, *, chunk):
    s = f_ref.shape[0]
    r = lax.broadcasted_iota(jnp.int32, (chunk, chunk), 0)
    c = lax.broadcasted_iota(jnp.int32, (chunk, chunk), 1)
    upto = jnp.where(c <= r, 1.0, 0.0).astype(BF16)
    lane = lax.broadcasted_iota(jnp.int32, (chunk, LANES), 1)
    carry = jnp.zeros((1, LANES), F32)
    for j in range(s // chunk):
        rows = slice(j * chunk, (j + 1) * chunk)
        log_f = _log2_sigmoid((f_ref[rows, :] + bf_ref[...]) * LOG2E)
        cs = carry
        for part in _split_bf16(log_f, 3):
            cs = cs + _dot(upto, part)
        carry = cs[chunk - 1:chunk, :]
        out = jnp.zeros((chunk, LANES), BF16)
        for k, part in reversed(list(enumerate(_split_bf16(cs, GATE_PARTS)))):
            out = jnp.where(lane < (k + 1) * N_FOX_HEADS, part, out)
        c_ref[rows, :] = out


def _fox_gates(f, b_f, chunk=256):
    b, s, _ = f.shape
    return pl.pallas_call(
        functools.partial(_fox_gate_kernel, chunk=chunk),
        grid=(b,),
        in_specs=[
            pl.BlockSpec((None, s, LANES), lambda i: (i, 0, 0)),
            _resident((1, LANES), lambda i: (0, 0)),
        ],
        out_specs=pl.BlockSpec((None, s, LANES), lambda i: (i, 0, 0)),
        out_shape=jax.ShapeDtypeStruct((b, s, LANES), BF16),
        compiler_params=_params(("parallel",)),
        name="fox_gates",
    )(f, b_f)


def _fox_kernel(qt_ref, k_ref, vt_ref, c_ref, o_ref, sa_ref, sb_ref, pa_ref, pb_ref, acc_ref):
    qi = pl.program_id(2)
    q_heads = _head_operands_t(qt_ref[...])
    heads = range(len(q_heads))
    first_head = pl.program_id(1) * len(q_heads)
    gate_row = lax.broadcasted_iota(jnp.int32, (LANES, QUERY_TILE), 0)
    masks = _diagonal_masks(strict=False)
    w_heads = []
    for h in heads:
        picks = ((gate_row & (N_FOX_HEADS - 1)) == first_head + h) & (gate_row < GATE_PARTS * N_FOX_HEADS)
        minus_gate = jnp.where(picks, -1.0, 0.0).astype(BF16)
        w_heads.append(jnp.concatenate([q_heads[h], minus_gate], axis=0))

    def scores(kb, s_ref, cols=ALL):
        start = pl.multiple_of(kb * KEY_BLOCK, KEY_BLOCK)
        gates = c_ref[pl.ds(start, KEY_BLOCK), :]
        for h in heads:
            lhs = jnp.concatenate([_key_rows(k_ref, kb, h), gates], axis=1)
            s_ref[h, :, cols] = _dot(lhs, w_heads[h][:, cols])

    ones_rows = jnp.ones((BF16_ROWS, KEY_BLOCK), BF16)

    def accumulate(kb, p_ref, a, ls, cols=ALL):
        new_ls = []
        for h in heads:
            lhs = jnp.concatenate([_value_rows_t(vt_ref, kb, h), ones_rows], axis=0)
            pv = _dot(lhs, pltpu.bitcast(p_ref[h, :, cols], BF16))
            acc_ref[h, :, cols] = a[h] * acc_ref[h, :, cols] + pv[:HEAD_DIM]
            l_new = a[h] * ls[h][:, cols] + pv[HEAD_DIM:HEAD_DIM + 1]
            new_ls.append(l_new if cols is ALL else _widen(ls[h], l_new))
        return new_ls

    def softmax(s_ref, p_ref, ms, mask, cols=ALL):
        new_ms, a = [], []
        for h in heads:
            m = ms[h][:, cols]
            s2 = s_ref[h, :, cols] if mask is None else jnp.where(mask, s_ref[h, :, cols], NEG_BIG)
            m_new = jnp.maximum(m, jnp.max(s2, axis=0, keepdims=True))
            p_ref[h, :, cols] = pltpu.bitcast(jnp.exp2(s2 - m_new).astype(BF16), jnp.uint32)
            new_ms.append(m_new if cols is ALL else _widen(ms[h], m_new))
            a.append(jnp.exp2(m - m_new))
        return new_ms, a

    def half_step(kb, cur, oth, state, *, mask=None, cols=ALL, next_cols=ALL):
        ms, ls, a_prev = state
        if next_cols is not None:
            scores(kb + 1, oth[0], next_cols)
        ls = accumulate(jnp.maximum(kb - 1, 0), oth[1], a_prev, ls)
        ms, a = softmax(cur[0], cur[1], ms, mask, cols)
        return ms, ls, a

    slot_a, slot_b = (sa_ref, pa_ref), (sb_ref, pb_ref)
    pb_ref[...] = jnp.zeros_like(pb_ref)
    acc_ref[...] = jnp.zeros_like(acc_ref)
    scores(0, sa_ref)
    state = ([jnp.full((1, QUERY_TILE), NEG_BIG, F32) for _ in heads],
             [jnp.zeros((1, QUERY_TILE), F32) for _ in heads],
             [jnp.ones((1, QUERY_TILE), F32) for _ in heads])

    def body(i, state):
        state = half_step(2 * i, slot_a, slot_b, state)
        return half_step(2 * i + 1, slot_b, slot_a, state)

    state = lax.fori_loop(0, qi, body, state)
    state = half_step(2 * qi, slot_a, slot_b, state, mask=masks[0], next_cols=LATE)
    _, ls, a = half_step(2 * qi + 1, slot_b, slot_a, state, mask=masks[1], cols=LATE, next_cols=None)
    ls = accumulate(2 * qi + 1, pb_ref, a, ls, LATE)
    for h in heads:
        o_ref[h * HEAD_DIM:(h + 1) * HEAD_DIM, :] = (acc_ref[h] / ls[h]).astype(o_ref.dtype)


def _fox_attention(proj, q_t, v_t, gates, pairs=8):
    b, s, _ = proj.shape
    groups = N_FOX_PAIRS // pairs
    return pl.pallas_call(
        _fox_kernel,
        grid=(b, groups, s // QUERY_TILE),
        in_specs=[
            pl.BlockSpec((None, pairs * LANES, QUERY_TILE), lambda i, h, j: (i, h, j)),
            pl.BlockSpec((None, s, pairs * LANES), lambda i, h, j: (i, 0, h)),
            pl.BlockSpec((None, s // KEY_BLOCK, pairs * LANES, KEY_BLOCK), lambda i, h, j: (i, 0, h, 0)),
            pl.BlockSpec((None, s, LANES), lambda i, h, j: (i, 0, 0)),
        ],
        out_specs=pl.BlockSpec((None, pairs * LANES, QUERY_TILE), lambda i, h, j: (i, h, j)),
        out_shape=jax.ShapeDtypeStruct((b, FOX_W, s), BF16),
        scratch_shapes=_attention_scratch(pairs),
        compiler_params=_params(("parallel", "parallel", "arbitrary")),
        name="fox_attention",
    )(q_t, proj, v_t, gates)


def _even_projection(w_in):
    kv0 = 3 * SB_W + SWA_QW
    dup = lambda w: jnp.repeat(w.reshape(D_MODEL, N_SWA_KV, HEAD_DIM), 2, axis=1).reshape(D_MODEL, -1)
    w_cat = jnp.concatenate(
        [w_in[:, SB_W:2 * SB_W], w_in[:, 3 * SB_W:kv0],
         dup(w_in[:, kv0:kv0 + SWA_KVW]), dup(w_in[:, kv0 + SWA_KVW:])], axis=1)
    return (w_cat.astype(BF16), w_in[:, :SB_W].T.astype(BF16), w_in[:, 2 * SB_W:3 * SB_W].T.astype(BF16),
            -QK_SCALE * LOG2E)


def _even_attention(proj, q_t, v_t, sinks, rel_bias):
    return _sb_attention(proj, q_t, v_t), _swa_attention(proj, _swa_bias(rel_bias), sinks.astype(F32))


def _spread_gates(v):
    return jnp.pad(jnp.tile(v, (1, GATE_PARTS)), ((0, 0), (0, LANES - GATE_PARTS * N_FOX_HEADS)))


def _odd_projection(w_in):
    w_cat = jnp.concatenate([w_in[:, FOX_W:2 * FOX_W], _spread_gates(w_in[:, 3 * FOX_W:])], axis=1)
    return (w_cat.astype(BF16), w_in[:, :FOX_W].T.astype(BF16), w_in[:, 2 * FOX_W:3 * FOX_W].T.astype(BF16),
            QK_SCALE * LOG2E)


def _odd_attention(proj, q_t, v_t, f, b_f):
    gates = _fox_gates(f, _spread_gates(b_f.reshape(1, -1).astype(F32)))
    return _fox_attention(proj, q_t, v_t, gates), None


def kernel(x, ln_g, ln_b, ffn1_in, ffn1_out, ffn2_in, ffn2_out, ab_w_in, ab_w_out, ab_sinks,
           fox_w_in, fox_b_f, fox_w_out, rel_bias):
    bsz, s, d = x.shape
    assert (s, d) == (SEQ, D_MODEL) and ln_g.shape[0] == DEPTH
    row = lambda v: v.reshape(1, D_MODEL).astype(F32)
    ffn1 = ffn1_in.astype(BF16), ffn1_out.astype(BF16)
    ffn2 = ffn2_in.astype(BF16), ffn2_out.astype(BF16)
    h = x
    for layer in range(DEPTH):
        ln = [(row(ln_g[layer, k]), row(ln_b[layer, k])) for k in range(3)]
        i = layer // 2
        if layer % 2 == 0:
            h, *p = _ffn_proj(h, *ffn1, layer, *ln[0], *_even_projection(ab_w_in[i]))
            o_t, o = _even_attention(*p, ab_sinks[i], rel_bias)
            w_mix = ab_w_out[i]
        else:
            h, *p = _ffn_proj(h, *ffn1, layer, *ln[0], *_odd_projection(fox_w_in[i]), tail=LANES)
            o_t, o = _odd_attention(*p, fox_b_f[i])
            w_mix = fox_w_out[i]
        h = _mix_ffn_ln(o_t, o, h, w_mix.astype(BF16), ln[1], *ffn2, layer, ln[2])
    return h
```
